```python
import jax, jax.numpy as jnp
from jax import lax
import numpy as np


D_MODEL = 1024
BATCH = 2
SEQ = 8192
DEPTH = 2

MIX_WIDTH = D_MODEL
N_BRANCHES = 4
BRANCH_WIDTH = MIX_WIDTH // N_BRANCHES
HEAD_DIM = 64
N_HEADS_A = BRANCH_WIDTH // HEAD_DIM
N_HEADS_B = BRANCH_WIDTH // HEAD_DIM
MOBA_BLOCK = 256
MOBA_TOPK = 3
Q_BLOCK = 128
DIL_PATTERNS = ((128, 1), (512, 4), (2048, 16))
CONV_WIDTH = 31
GLA_HEADS = 4
GLA_DV = BRANCH_WIDTH // GLA_HEADS
GLA_DK = GLA_DV // 2
GLA_RANK = 16
GLA_TAU = 16.0
GLA_CHUNK = 64
EPS = 1e-6
NEG = -1e30
IN_SIZES = (BRANCH_WIDTH,) * 4 + (BRANCH_WIDTH,) * 4 + (BRANCH_WIDTH,) * 3 + (GLA_HEADS * GLA_DK, GLA_HEADS * GLA_DK, BRANCH_WIDTH, GLA_RANK, BRANCH_WIDTH)
IN_COLS = sum(IN_SIZES)

kernel_name = 'hybrid_moba_dilated_conformer_gla'


def rms_norm(x, g):
    xf = x.astype(jnp.float32)
    y = xf * lax.rsqrt(jnp.mean(xf * xf, axis=-1, keepdims=True) + EPS)
    return (y * g.astype(jnp.float32)).astype(x.dtype)


def alibi_slopes():
    n = N_HEADS_A + N_HEADS_B
    s = 2.0 ** (-8.0 * np.arange(1, n + 1) / n)
    return jnp.asarray(s[0::2], jnp.float32), jnp.asarray(s[1::2], jnp.float32)


def moba_attention(q, k, v, slopes):
    Bsz, S, H, hd = q.shape
    f32 = jnp.float32
    scale = hd ** -0.5
    nb = -(-S // MOBA_BLOCK)
    pad = nb * MOBA_BLOCK - S
    n_sel = min(MOBA_TOPK, nb)
    n_s = n_sel * MOBA_BLOCK

    def blocks(t):
        t = jnp.pad(t, ((0, 0), (0, pad), (0, 0), (0, 0)))
        return t.reshape(Bsz, nb, MOBA_BLOCK, H, hd).transpose(0, 3, 1, 2, 4)

    kb, vb = blocks(k), blocks(v)
    k_mean = jnp.mean(kb.astype(f32), axis=3).astype(q.dtype)
    n_q = S // Q_BLOCK
    q_chunks = q.reshape(Bsz, n_q, Q_BLOCK, H, hd).transpose(1, 0, 3, 2, 4)
    b_idx = jnp.arange(Bsz)[:, None, None, None]
    h_idx = jnp.arange(H)[None, :, None, None]
    blk_pos = jnp.arange(MOBA_BLOCK)
    blk_ids = jnp.arange(nb)

    def attend(args):
        c, qc = args
        pos = c * Q_BLOCK + jnp.arange(Q_BLOCK)
        own = (c * Q_BLOCK) // MOBA_BLOCK
        gate = jnp.einsum('bhqd,bhnd->bhqn', qc, k_mean).astype(f32)
        gate = jnp.where(blk_ids < own, gate, -jnp.inf)
        _, idx = lax.top_k(gate, n_sel)
        k_sel = kb[b_idx, h_idx, idx]
        v_sel = vb[b_idx, h_idx, idx]
        s_sel = jnp.einsum('bhqd,bhqnkd->bhqnk', qc, k_sel).astype(f32) * scale
        dist_sel = pos[:, None, None] - (idx[..., None] * MOBA_BLOCK + blk_pos)
        s_sel = s_sel - slopes[:, None, None, None] * dist_sel.astype(f32)
        s_sel = jnp.where((idx < own)[..., None], s_sel, NEG)
        k_own = lax.dynamic_index_in_dim(kb, own, axis=2, keepdims=False)
        v_own = lax.dynamic_index_in_dim(vb, own, axis=2, keepdims=False)
        s_own = jnp.einsum('bhqd,bhkd->bhqk', qc, k_own).astype(f32) * scale
        dist_own = pos[:, None] - (own * MOBA_BLOCK + blk_pos)[None, :]
        s_own = s_own - slopes[:, None, None] * dist_own.astype(f32)
        s_own = jnp.where(dist_own >= 0, s_own, NEG)
        s = jnp.concatenate([s_sel.reshape(Bsz, H, Q_BLOCK, n_s), s_own], axis=-1)
        p = jax.nn.softmax(s, axis=-1).astype(v.dtype)
        o = jnp.einsum('bhqnk,bhqnkd->bqhd', p[..., :n_s].reshape(Bsz, H, Q_BLOCK, n_sel, MOBA_BLOCK), v_sel)
        o = o + jnp.einsum('bhqk,bhkd->bqhd', p[..., n_s:], v_own)
        return o

    out = lax.map(attend, (jnp.arange(n_q), q_chunks))
    return out.transpose(1, 0, 2, 3, 4).reshape(Bsz, S, H, hd)


def banded_window_attention(q, k, v, span, slope):
    Bsz, G, H, L, hd = q.shape
    f32 = jnp.float32
    nbk = -(-L // span)
    pad = nbk * span - L

    def blocks(t):
        t = jnp.pad(t, ((0, 0), (0, 0), (0, 0), (0, pad), (0, 0)))
        return t.reshape(Bsz, G, H, nbk, span, hd)

    def band(t):
        prev = jnp.pad(t, ((0, 0), (0, 0), (0, 0), (1, 0), (0, 0), (0, 0)))[:, :, :, :-1]
        return jnp.concatenate([prev, t], axis=4)

    qb = blocks(q)
    kband, vband = band(blocks(k)), band(blocks(v))
    s = jnp.einsum('bghnqd,bghnkd->bghnqk', qb, kband).astype(f32) * hd ** -0.5
    qi = jnp.arange(span)
    kr = jnp.arange(2 * span) - span
    dist = qi[:, None] - kr[None, :]
    kglob = jnp.arange(nbk)[:, None] * span + kr[None, :]
    valid = ((dist >= 0) & (dist <= span))[None] & (kglob >= 0)[:, None, :]
    s = s - slope[:, None, None, None] * dist.astype(f32)
    s = jnp.where(valid, s, NEG)
    m = jnp.max(s, axis=-1, keepdims=True)
    p = jnp.exp(s - m)
    l = jnp.sum(p, axis=-1)
    o = jnp.einsum('bghnqk,bghnkd->bghnqd', p, vband.astype(f32)) / l[..., None]
    o = o.reshape(Bsz, G, H, nbk * span, hd)[:, :, :, :L]
    m = m.reshape(Bsz, G, H, nbk * span)[..., :L]
    l = l.reshape(Bsz, G, H, nbk * span)[..., :L]
    return o, m, l


def dilated_attention(q, k, v, slopes):
    Bsz, S, H, hd = q.shape
    outs, maxes, dens = [], [], []
    for window, dil in DIL_PATTERNS:
        span = window // dil
        L = S // dil

        def sub(t):
            return t.reshape(Bsz, L, dil, H, hd).transpose(0, 2, 3, 1, 4)

        o, m, l = banded_window_attention(sub(q), sub(k), sub(v), span, slopes * dil)
        outs.append(o.transpose(0, 3, 1, 2, 4).reshape(Bsz, S, H, hd))
        maxes.append(m.transpose(0, 3, 1, 2).reshape(Bsz, S, H))
        dens.append(l.transpose(0, 3, 1, 2).reshape(Bsz, S, H))
    o = jnp.stack(outs)
    m = jnp.stack(maxes)
    l = jnp.stack(dens)
    w = l * jnp.exp(m - jnp.max(m, axis=0))
    out = jnp.sum(w[..., None] * o, axis=0) / jnp.sum(w, axis=0)[..., None]
    return out.astype(v.dtype)


def conformer_conv(a, b, w_dw, b_dw, ln_g, ln_b, w_pw, b_pw):
    u = a * jax.nn.sigmoid(b)
    C = u.shape[-1]
    u = lax.conv_general_dilated(u, w_dw[:, None, :].astype(u.dtype), window_strides=(1,), padding=((CONV_WIDTH - 1, 0),), dimension_numbers=('NWC', 'WIO', 'NWC'), feature_group_count=C) + b_dw
    uf = u.astype(jnp.float32)
    mu = jnp.mean(uf, axis=-1, keepdims=True)
    var = jnp.mean(jnp.square(uf - mu), axis=-1, keepdims=True)
    un = (uf - mu) * lax.rsqrt(var + EPS) * ln_g + ln_b
    return jax.nn.silu(un).astype(u.dtype) @ w_pw + b_pw


def gla_mixer(q, k, v, gate_lr, w_g, b_g, g_norm):
    Bsz, S = q.shape[:2]
    f32 = jnp.float32
    log_a = jax.nn.log_sigmoid((gate_lr @ w_g + b_g).astype(f32)) / GLA_TAU
    log_a = log_a.reshape(Bsz, S, GLA_HEADS, GLA_DK)
    nC = S // GLA_CHUNK

    def chunks(t):
        return t.astype(f32).reshape(Bsz, nC, GLA_CHUNK, GLA_HEADS, -1).transpose(0, 3, 1, 2, 4)

    qc = chunks(q) * GLA_DK ** -0.5
    kc = chunks(k)
    vc = chunks(v)
    bc = jnp.cumsum(chunks(log_a), axis=3)
    causal = jnp.tril(jnp.ones((GLA_CHUNK, GLA_CHUNK), dtype=bool))
    diff = bc[..., :, None, :] - bc[..., None, :, :]
    decay = jnp.exp(jnp.where(causal[:, :, None], diff, -jnp.inf))
    attn = jnp.einsum('bhcid,bhcjd,bhcijd->bhcij', qc, kc, decay)
    o_intra = jnp.einsum('bhcij,bhcjv->bhciv', attn, vc)
    b_last = bc[..., -1, :]
    kv = jnp.einsum('bhcjd,bhcjv->bhcdv', kc * jnp.exp(b_last[..., None, :] - bc), vc)

    def step(state, inp):
        dec, kv_c = inp
        return dec[..., None] * state + kv_c, state

    init = jnp.zeros((Bsz, GLA_HEADS, GLA_DK, GLA_DV), f32)
    _, prev = lax.scan(step, init, (jnp.moveaxis(jnp.exp(b_last), 2, 0), jnp.moveaxis(kv, 2, 0)))
    prev = jnp.moveaxis(prev, 0, 2)
    o_inter = jnp.einsum('bhcid,bhcdv->bhciv', qc * jnp.exp(bc), prev)
    o = (o_intra + o_inter).transpose(0, 2, 3, 1, 4).reshape(Bsz, S, GLA_HEADS, GLA_DV)
    o = rms_norm(o, g_norm)
    return o.reshape(Bsz, S, GLA_HEADS * GLA_DV).astype(v.dtype)


def hybrid_layer(x, norm_g, w_in, q_gain_a, k_gain_a, q_gain_b, k_gain_b, conv_w, conv_b, conv_ln_g, conv_ln_b, conv_pw_w, conv_pw_b, gla_gate_w, gla_gate_b, gla_norm_g, w_out):
    Bsz, S, _ = x.shape
    h = rms_norm(x, norm_g)
    z = h @ w_in
    splits = np.cumsum(IN_SIZES)[:-1].tolist()
    (qa, ka, va, ga, qb, kb, vb, gb, ca, cg, gc, qd, kd, vd, lr, gd) = jnp.split(z, splits, axis=-1)
    slopes_a, slopes_b = alibi_slopes()

    def heads(t, n):
        return t.reshape(Bsz, S, n, -1)

    ya = moba_attention(rms_norm(heads(qa, N_HEADS_A), q_gain_a), rms_norm(heads(ka, N_HEADS_A), k_gain_a), heads(va, N_HEADS_A), slopes_a)
    ya = ya.reshape(Bsz, S, BRANCH_WIDTH) * jax.nn.silu(ga)
    yb = dilated_attention(rms_norm(heads(qb, N_HEADS_B), q_gain_b), rms_norm(heads(kb, N_HEADS_B), k_gain_b), heads(vb, N_HEADS_B), slopes_b)
    yb = yb.reshape(Bsz, S, BRANCH_WIDTH) * jax.nn.silu(gb)
    yc = conformer_conv(ca, cg, conv_w, conv_b, conv_ln_g, conv_ln_b, conv_pw_w, conv_pw_b) * jax.nn.silu(gc)
    yd = gla_mixer(heads(qd, GLA_HEADS), heads(kd, GLA_HEADS), heads(vd, GLA_HEADS), lr, gla_gate_w, gla_gate_b, gla_norm_g) * jax.nn.silu(gd)
    y = jnp.concatenate([ya, yb, yc, yd], axis=-1) @ w_out
    return x + y


def setup_inputs(seed: int = 0) -> dict:
    key = jax.random.key(seed)
    ks = jax.random.split(key, 17)
    f32 = jnp.float32

    def nrm(k, shape, scale):
        return jax.random.normal(k, shape, f32) * scale

    BW = BRANCH_WIDTH
    return {
        'x': nrm(ks[0], (BATCH, SEQ, D_MODEL), 1.0),
        'norm_g': 1.0 + nrm(ks[1], (DEPTH, D_MODEL), 0.02),
        'w_in': nrm(ks[2], (DEPTH, D_MODEL, IN_COLS), D_MODEL ** -0.5),
        'q_gain_a': 1.0 + nrm(ks[3], (DEPTH, HEAD_DIM), 0.02),
        'k_gain_a': 1.0 + nrm(ks[4], (DEPTH, HEAD_DIM), 0.02),
        'q_gain_b': 1.0 + nrm(ks[5], (DEPTH, HEAD_DIM), 0.02),
        'k_gain_b': 1.0 + nrm(ks[6], (DEPTH, HEAD_DIM), 0.02),
        'conv_w': nrm(ks[7], (DEPTH, CONV_WIDTH, BW), CONV_WIDTH ** -0.5),
        'conv_b': nrm(ks[8], (DEPTH, BW), 0.02),
        'conv_ln_g': 1.0 + nrm(ks[9], (DEPTH, BW), 0.02),
        'conv_ln_b': nrm(ks[10], (DEPTH, BW), 0.02),
        'conv_pw_w': nrm(ks[11], (DEPTH, BW, BW), BW ** -0.5),
        'conv_pw_b': nrm(ks[12], (DEPTH, BW), 0.02),
        'gla_gate_w': nrm(ks[13], (DEPTH, GLA_RANK, GLA_HEADS * GLA_DK), GLA_RANK ** -0.5),
        'gla_gate_b': nrm(ks[14], (DEPTH, GLA_HEADS * GLA_DK), 0.02),
        'gla_norm_g': 1.0 + nrm(ks[15], (DEPTH, GLA_DV), 0.02),
        'w_out': nrm(ks[16], (DEPTH, MIX_WIDTH, D_MODEL), MIX_WIDTH ** -0.5),
    }


def reference(x, norm_g, w_in, q_gain_a, k_gain_a, q_gain_b, k_gain_b, conv_w, conv_b, conv_ln_g, conv_ln_b, conv_pw_w, conv_pw_b, gla_gate_w, gla_gate_b, gla_norm_g, w_out):
    for l in range(DEPTH):
        x = hybrid_layer(x, norm_g[l], w_in[l], q_gain_a[l], k_gain_a[l], q_gain_b[l], k_gain_b[l], conv_w[l], conv_b[l], conv_ln_g[l], conv_ln_b[l], conv_pw_w[l], conv_pw_b[l], gla_gate_w[l], gla_gate_b[l], gla_norm_g[l], w_out[l])
    return x
```

```python
import numpy as np
import jax
import jax.numpy as jnp
from jax import lax
from jax.experimental import pallas as pl
from jax.experimental.pallas import tpu as pltpu

F32 = jnp.float32
BF16 = jnp.bfloat16

D_MODEL = 1024
BRANCH = 256
HEAD_DIM = 64
MOBA_BLOCK = 256
MOBA_TOPK = 3
DIL_PATTERNS = ((128, 1), (512, 4), (2048, 16))
CONV_WIDTH = 31
GLA_HEADS = 4
GLA_DK = 32
GLA_DV = 64
GLA_RANK = 16
GLA_TAU = 16.0
EPS = 1e-6
NEG = -1e30

LANES = 128
ROW_TILE = 512
ATT_TILE = 256
DIL_REACH = max(w for w, _ in DIL_PATTERNS) // ATT_TILE
GLA_CHUNK = 64
GLA_LEVELS = 6
CONV_HALO = 32
LR_PAD = 128
VMEM_LIMIT = 56 * 1024 * 1024

ZA_COLS = 4 * BRANCH
ZC_COLS = 3 * BRANCH
ZD_COLS = 2 * GLA_HEADS * GLA_DK + 2 * BRANCH + LR_PAD
W_COLS = 2 * ZA_COLS + ZC_COLS + ZD_COLS

_NT = (((1,), (1,)), ((), ()))
_TN = (((0,), (0,)), ((), ()))


def _params(n_grid):
    return pltpu.CompilerParams(dimension_semantics=("arbitrary",) * n_grid,
                                vmem_limit_bytes=VMEM_LIMIT)


def _silu(x):
    return x * jax.nn.sigmoid(x)


def _group_mean_sq(z, bd):
    z2 = z * z
    hi = z2.astype(BF16)
    lo = (z2 - hi.astype(F32)).astype(BF16)
    return (jnp.dot(hi, bd, preferred_element_type=F32)
            + jnp.dot(lo, bd, preferred_element_type=F32))


def _inproj_kernel(x_ref, ng_ref, w_ref, gains_ref, bd_ref,
                   qa_ref, ka_ref, va_ref, ga_ref, kma_ref,
                   qb_ref, kb_ref, vb_ref, gb_ref, zc_ref, zd_ref):
    x = x_ref[...]
    ms = jnp.mean(x * x, axis=-1, keepdims=True)
    h = (x * lax.rsqrt(ms + EPS) * ng_ref[...]).astype(BF16)
    bd = bd_ref[...]

    def proj(c0, width):
        return jnp.dot(h, w_ref[:, c0:c0 + width], preferred_element_type=F32)

    def head_norm(z, row):
        return z * lax.rsqrt(_group_mean_sq(z, bd) + EPS) * gains_ref[row:row + 1, :]

    qa_ref[...] = head_norm(proj(0, BRANCH), 0)
    ka = head_norm(proj(BRANCH, BRANCH), 1)
    ka_ref[...] = ka.astype(BF16)
    for blk in range(ROW_TILE // MOBA_BLOCK):
        kma_ref[0, blk:blk + 1, :] = jnp.mean(
            ka[blk * MOBA_BLOCK:(blk + 1) * MOBA_BLOCK], axis=0, keepdims=True)
    va_ref[...] = proj(2 * BRANCH, BRANCH).astype(BF16)
    ga_ref[...] = proj(3 * BRANCH, BRANCH)

    c = ZA_COLS
    qb_ref[...] = (head_norm(proj(c, BRANCH), 2) * HEAD_DIM ** -0.5).astype(BF16)
    kb_ref[...] = head_norm(proj(c + BRANCH, BRANCH), 3).astype(BF16)
    vb_ref[...] = proj(c + 2 * BRANCH, BRANCH).astype(BF16)
    gb_ref[...] = proj(c + 3 * BRANCH, BRANCH)

    c = 2 * ZA_COLS
    zc_ref[...] = proj(c, ZC_COLS)
    zd_ref[...] = proj(c + ZC_COLS, ZD_COLS)


def _inproj(x2, ng, w, gains, bd):
    n = x2.shape[0]
    nt = n // ROW_TILE
    row = lambda i: (i, 0)
    const = lambda i: (0, 0)

    def out(cols, dtype):
        return (jax.ShapeDtypeStruct((n, cols), dtype), pl.BlockSpec((ROW_TILE, cols), row))

    outs = [out(BRANCH, F32), out(BRANCH, BF16), out(BRANCH, BF16), out(BRANCH, F32),
            (jax.ShapeDtypeStruct((nt, ROW_TILE // MOBA_BLOCK, BRANCH), F32),
             pl.BlockSpec((1, ROW_TILE // MOBA_BLOCK, BRANCH), lambda i: (i, 0, 0))),
            out(BRANCH, BF16), out(BRANCH, BF16), out(BRANCH, BF16), out(BRANCH, F32),
            out(ZC_COLS, F32), out(ZD_COLS, F32)]
    return pl.pallas_call(
        _inproj_kernel,
        grid=(nt,),
        in_specs=[pl.BlockSpec((ROW_TILE, D_MODEL), row),
                  pl.BlockSpec((1, D_MODEL), const),
                  pl.BlockSpec((D_MODEL, W_COLS), const),
                  pl.BlockSpec((4, BRANCH), const),
                  pl.BlockSpec((BRANCH, BRANCH), const)],
        out_specs=[o[1] for o in outs],
        out_shape=[o[0] for o in outs],
        compiler_params=_params(1),
        name="inproj",
    )(x2, ng, w, gains, bd)


def _online_softmax_step(s, m, l):
    m_new = jnp.maximum(m, jnp.max(s, axis=-1, keepdims=True))
    alpha = jnp.exp(m - m_new)
    p = jnp.exp(s - m_new)
    return p, m_new, alpha * l + jnp.sum(p, axis=-1, keepdims=True), alpha


def _attend_block(qpair, kb, vb, bias_fn, lo_half, carry):
    m0, l0, m1, l1, acc = carry
    s0 = bias_fn(0, lax.dot_general(qpair[0], kb, _NT, preferred_element_type=F32))
    s1 = bias_fn(1, lax.dot_general(qpair[1], kb, _NT, preferred_element_type=F32))
    p0, m0, l0, a0 = _online_softmax_step(s0, m0, l0)
    p1, m1, l1, a1 = _online_softmax_step(s1, m1, l1)
    pv0 = jnp.dot(p0.astype(BF16), vb, preferred_element_type=F32)
    pv1 = jnp.dot(p1.astype(BF16), vb, preferred_element_type=F32)
    acc = jnp.where(lo_half, a0 * acc + pv0, a1 * acc + pv1)
    return m0, l0, m1, l1, acc


def _attn_init():
    col = lambda v: jnp.full((ATT_TILE, 1), v, F32)
    return col(NEG), col(0.0), col(NEG), col(0.0), jnp.zeros((ATT_TILE, LANES), F32)


def _attn_finish(carry, gate, lo_half):
    _, l0, _, l1, acc = carry
    return acc / jnp.where(lo_half, l0, l1) * _silu(gate)


def _key_rows(n):
    return pl.ds(pl.multiple_of(n * ATT_TILE, ATT_TILE), ATT_TILE)


def _top_k_mask(gate, blk_f):
    sel = jnp.zeros(gate.shape, F32)
    for _ in range(MOBA_TOPK):
        top = jnp.max(gate, axis=-1, keepdims=True)
        first = jnp.min(jnp.where(gate == top, blk_f, 1e9), axis=-1, keepdims=True)
        pick = blk_f == first
        sel = jnp.where(pick, 1.0, sel)
        gate = jnp.where(pick, -jnp.inf, gate)
    return sel


def _moba_kernel(q_ref, k_ref, v_ref, km_ref, g_ref, o_ref):
    hp = pl.program_id(1)
    i = pl.program_id(2)
    n_blk = km_ref.shape[0]
    lo_half = lax.broadcasted_iota(jnp.int32, (ATT_TILE, LANES), 1) < HEAD_DIM
    q = q_ref[...]
    q_heads = (jnp.where(lo_half, q, 0.0), jnp.where(lo_half, 0.0, q))

    km = km_ref[...]
    blk = lax.broadcasted_iota(jnp.int32, (ATT_TILE, n_blk), 1)
    blk_f = blk.astype(F32)
    past = blk < i
    sel = []
    for qh in q_heads:
        gate = lax.dot_general(qh, km, _NT, precision=lax.Precision.HIGHEST,
                               preferred_element_type=F32)
        sel.append(jnp.where(past, _top_k_mask(jnp.where(past, gate, -jnp.inf), blk_f), 0.0))

    qpair = tuple((qh * HEAD_DIM ** -0.5).astype(BF16) for qh in q_heads)
    slopes = (jnp.where(hp == 0, 2.0 ** -1, 2.0 ** -5), jnp.where(hp == 0, 2.0 ** -3, 2.0 ** -7))
    rel = (lax.broadcasted_iota(jnp.int32, (ATT_TILE, ATT_TILE), 0)
           - lax.broadcasted_iota(jnp.int32, (ATT_TILE, ATT_TILE), 1))
    rel_f = rel.astype(F32)

    def own_bias(h, s):
        return jnp.where(rel >= 0, s - slopes[h] * rel_f, NEG)

    carry = _attend_block(qpair, k_ref[_key_rows(i), :], v_ref[_key_rows(i), :],
                          own_bias, lo_half, _attn_init())

    def past_block(n, carry):
        dist = rel_f + ((i - n) * ATT_TILE).astype(F32)

        def bias(h, s):
            chosen = jnp.sum(jnp.where(blk == n, sel[h], 0.0), axis=-1, keepdims=True) > 0.5
            return jnp.where(chosen, s - slopes[h] * dist, NEG)

        return _attend_block(qpair, k_ref[_key_rows(n), :], v_ref[_key_rows(n), :],
                             bias, lo_half, carry)

    carry = lax.fori_loop(0, i, past_block, carry)
    o_ref[...] = _attn_finish(carry, g_ref[...], lo_half)


def _moba(q, k, v, kmean, g, batch, seq):
    nq = seq // ATT_TILE
    tile = pl.BlockSpec((ATT_TILE, LANES), lambda b, hp, i: (b * nq + i, hp))
    whole = pl.BlockSpec((seq, LANES), lambda b, hp, i: (b, hp))
    return pl.pallas_call(
        _moba_kernel,
        grid=(batch, BRANCH // LANES, nq),
        in_specs=[tile, whole, whole,
                  pl.BlockSpec((seq // MOBA_BLOCK, LANES), lambda b, hp, i: (b, hp)),
                  tile],
        out_specs=tile,
        out_shape=jax.ShapeDtypeStruct(q.shape, F32),
        compiler_params=_params(3),
        name="moba",
    )(q, k, v, kmean, g)


def _dilated_bias_table():
    r = np.arange(ATT_TILE)
    delta = (np.arange(DIL_REACH + 1)[:, None, None] * ATT_TILE + r[None, :, None] - r[None, None, :])
    mult = np.zeros(delta.shape, np.float64)
    for window, dil in DIL_PATTERNS:
        mult += (delta >= 0) & (delta <= window) & (delta % dil == 0)
    logm = np.where(mult > 0, np.log(np.maximum(mult, 1.0)), NEG)
    n_heads = BRANCH // HEAD_DIM
    slopes = 2.0 ** (-2.0 * (np.arange(n_heads) + 1))
    table = np.where(mult[None] > 0, logm[None] - slopes[:, None, None, None] * delta[None], NEG)
    return jnp.asarray(table, F32)


def _dilated_kernel(q_ref, k_ref, v_ref, t_ref, g_ref, o_ref):
    i = pl.program_id(2)
    lo_half = lax.broadcasted_iota(jnp.int32, (ATT_TILE, LANES), 1) < HEAD_DIM
    q = q_ref[...]
    zero = jnp.zeros_like(q)
    qpair = (jnp.where(lo_half, q, zero), jnp.where(lo_half, zero, q))

    def block(o, carry):
        n = i - o
        return _attend_block(qpair, k_ref[_key_rows(n), :], v_ref[_key_rows(n), :],
                             lambda h, s: s + t_ref[h, o], lo_half, carry)

    carry = lax.fori_loop(0, jnp.minimum(i, DIL_REACH) + 1, block, _attn_init())
    o_ref[...] = _attn_finish(carry, g_ref[...], lo_half)


def _dilated(q, k, v, table, g, batch, seq):
    nq = seq // ATT_TILE
    tile = pl.BlockSpec((ATT_TILE, LANES), lambda b, hp, i: (b * nq + i, hp))
    whole = pl.BlockSpec((seq, LANES), lambda b, hp, i: (b, hp))
    return pl.pallas_call(
        _dilated_kernel,
        grid=(batch, BRANCH // LANES, nq),
        in_specs=[tile, whole, whole,
                  pl.BlockSpec((LANES // HEAD_DIM, DIL_REACH + 1, ATT_TILE, ATT_TILE),
                               lambda b, hp, i: (hp, 0, 0, 0)),
                  tile],
        out_specs=tile,
        out_shape=jax.ShapeDtypeStruct(q.shape, F32),
        compiler_params=_params(3),
        name="dilated",
    )(q, k, v, table, g)


def _conv_kernel(z_ref, halo_ref, w_ref, b_ref, lng_ref, lnb_ref, pw_ref, pwb_ref, o_ref, u_buf):
    j = pl.program_id(1)

    def glu(z):
        return z[:, 0:BRANCH] * jax.nn.sigmoid(z[:, BRANCH:2 * BRANCH])

    z = z_ref[...]
    u_buf[0:CONV_HALO, :] = jnp.where(j > 0, glu(halo_ref[...]), 0.0)
    u_buf[CONV_HALO:, :] = glu(z)
    acc = jnp.zeros((ROW_TILE, BRANCH), F32) + b_ref[...]
    first = CONV_HALO - (CONV_WIDTH - 1)
    for tap in range(CONV_WIDTH):
        acc = acc + w_ref[tap:tap + 1, :] * u_buf[first + tap:first + tap + ROW_TILE, :]
    mu = jnp.mean(acc, axis=-1, keepdims=True)
    cen = acc - mu
    var = jnp.mean(cen * cen, axis=-1, keepdims=True)
    un = cen * lax.rsqrt(var + EPS) * lng_ref[...] + lnb_ref[...]
    y = jnp.dot(_silu(un).astype(BF16), pw_ref[...], preferred_element_type=F32) + pwb_ref[...]
    o_ref[...] = y * _silu(z[:, 2 * BRANCH:3 * BRANCH])


def _conv(zc, w, b, lng, lnb, pw, pwb, batch, seq):
    nt = seq // ROW_TILE
    per = ROW_TILE // CONV_HALO
    const = lambda bi, j: (0, 0)
    return pl.pallas_call(
        _conv_kernel,
        grid=(batch, nt),
        in_specs=[pl.BlockSpec((ROW_TILE, ZC_COLS), lambda bi, j: (bi * nt + j, 0)),
                  pl.BlockSpec((CONV_HALO, ZC_COLS),
                               lambda bi, j: (jnp.maximum((bi * nt + j) * per - 1, 0), 0)),
                  pl.BlockSpec((CONV_WIDTH, BRANCH), const),
                  pl.BlockSpec((1, BRANCH), const),
                  pl.BlockSpec((1, BRANCH), const),
                  pl.BlockSpec((1, BRANCH), const),
                  pl.BlockSpec((BRANCH, BRANCH), const),
                  pl.BlockSpec((1, BRANCH), const)],
        out_specs=pl.BlockSpec((ROW_TILE, BRANCH), lambda bi, j: (bi * nt + j, 0)),
        out_shape=jax.ShapeDtypeStruct((batch * seq, BRANCH), F32),
        scratch_shapes=[pltpu.VMEM((CONV_HALO + ROW_TILE, BRANCH), F32)],
        compiler_params=_params(2),
        name="conv",
    )(zc, zc, w, b, lng, lnb, pw, pwb)


def _gla_sum_matrices():
    c = GLA_CHUNK
    i = np.arange(c)[:, None]
    t = np.arange(c)[None, :]
    mats = [t <= i]
    for l in range(GLA_LEVELS):
        h = (c // 2) >> l
        mid = (i // (2 * h)) * (2 * h) + h
        mats.append(((i & h) != 0) & (t >= mid) & (t <= i))
    for l in range(GLA_LEVELS):
        h = (c // 2) >> l
        mid = (i // (2 * h)) * (2 * h) + h
        mats.append(((i & h) == 0) & (t > i) & (t < mid))
    return jnp.asarray(np.concatenate(mats, axis=0), BF16)


def _gla_kernel(z_ref, sums_ref, wg_ref, bg_ref, gn_ref, bd_ref, o_ref, state_ref):
    c = GLA_CHUNK
    nh = GLA_HEADS
    kw = nh * GLA_DK
    vw = nh * GLA_DV

    @pl.when(pl.program_id(1) == 0)
    def _():
        state_ref[...] = jnp.zeros_like(state_ref)

    row = lax.broadcasted_iota(jnp.int32, (c, 1), 0)
    qi = lax.broadcasted_iota(jnp.int32, (c, nh * c), 0)
    kj = lax.broadcasted_iota(jnp.int32, (c, nh * c), 1) % c
    level_mask = [(qi >> (GLA_LEVELS - l)) == (kj >> (GLA_LEVELS - l)) for l in range(GLA_LEVELS)]
    diag_mask = qi == kj
    k_head = (lax.broadcasted_iota(jnp.int32, (nh * c, kw), 0) // c
              == lax.broadcasted_iota(jnp.int32, (nh * c, kw), 1) // GLA_DK)
    v_head = (lax.broadcasted_iota(jnp.int32, (nh * c, vw), 0) // c
              == lax.broadcasted_iota(jnp.int32, (nh * c, vw), 1) // GLA_DV)
    s_head = (lax.broadcasted_iota(jnp.int32, (vw, kw), 0) // GLA_DV
              == lax.broadcasted_iota(jnp.int32, (vw, kw), 1) // GLA_DK)

    def per_head_keys(kt):
        return jnp.where(k_head, jnp.concatenate([kt] * nh, axis=0), 0.0).astype(BF16)

    def chunk(ci, _):
        rows = pl.ds(pl.multiple_of(ci * c, c), c)
        q = z_ref[rows, 0:kw] * GLA_DK ** -0.5
        k = z_ref[rows, kw:2 * kw]
        v = z_ref[rows, 2 * kw:2 * kw + vw]
        gd = z_ref[rows, 2 * kw + vw:2 * kw + 2 * vw]
        lr = z_ref[rows, 2 * kw + 2 * vw:2 * kw + 2 * vw + LR_PAD]

        g = jnp.dot(lr.astype(BF16), wg_ref[...], preferred_element_type=F32) + bg_ref[...]
        la = (jnp.minimum(g, 0.0) - jnp.log(1.0 + jnp.exp(-jnp.abs(g)))) / GLA_TAU
        a1 = la.astype(BF16)
        r1 = la - a1.astype(F32)
        a2 = r1.astype(BF16)
        a3 = (r1 - a2.astype(F32)).astype(BF16)
        parts = jnp.dot(sums_ref[...], jnp.concatenate([a1, a2, a3], axis=1),
                        preferred_element_type=F32)
        sums = parts[:, 0:kw] + parts[:, kw:2 * kw] + parts[:, 2 * kw:3 * kw]
        bc = sums[0:c]

        attn = jnp.where(diag_mask,
                         lax.dot_general(q.astype(BF16), per_head_keys(k), _NT,
                                         preferred_element_type=F32), 0.0)
        for l in range(GLA_LEVELS):
            later = (row & ((c // 2) >> l)) != 0
            since_mid = sums[(1 + l) * c:(2 + l) * c]
            until_mid = sums[(1 + GLA_LEVELS + l) * c:(2 + GLA_LEVELS + l) * c]
            qt = jnp.where(later, q * jnp.exp(since_mid), 0.0).astype(BF16)
            kt = jnp.where(later, 0.0, k * jnp.exp(until_mid))
            a = lax.dot_general(qt, per_head_keys(kt), _NT, preferred_element_type=F32)
            attn = attn + jnp.where(level_mask[l], a, 0.0)

        vb = v.astype(BF16)
        v_stack = jnp.where(v_head, jnp.concatenate([vb] * nh, axis=0), jnp.zeros((), BF16))
        o = jnp.dot(attn.astype(BF16), v_stack, preferred_element_type=F32)

        state = state_ref[...]
        o = o + lax.dot_general((q * jnp.exp(bc)).astype(BF16), state.astype(BF16), _NT,
                                preferred_element_type=F32)
        b_last = bc[c - 1:c, :]
        k_dec = (k * jnp.exp(b_last - bc)).astype(BF16)
        upd = lax.dot_general(vb, k_dec, _TN, preferred_element_type=F32)
        state_ref[...] = state * jnp.exp(b_last) + jnp.where(s_head, upd, 0.0)

        on = o * lax.rsqrt(_group_mean_sq(o, bd_ref[...]) + EPS) * gn_ref[...]
        o_ref[rows, :] = on * _silu(gd)
        return 0

    lax.fori_loop(0, ROW_TILE // c, chunk, 0)


def _gla(zd, sums, wg, bg, gn, bd, batch, seq):
    nt = seq // ROW_TILE
    const = lambda bi, j: (0, 0)
    return pl.pallas_call(
        _gla_kernel,
        grid=(batch, nt),
        in_specs=[pl.BlockSpec((ROW_TILE, ZD_COLS), lambda bi, j: (bi * nt + j, 0)),
                  pl.BlockSpec(sums.shape, const),
                  pl.BlockSpec((LR_PAD, GLA_HEADS * GLA_DK), const),
                  pl.BlockSpec((1, GLA_HEADS * GLA_DK), const),
                  pl.BlockSpec((1, BRANCH), const),
                  pl.BlockSpec((BRANCH, BRANCH), const)],
        out_specs=pl.BlockSpec((ROW_TILE, BRANCH), lambda bi, j: (bi * nt + j, 0)),
        out_shape=jax.ShapeDtypeStruct((batch * seq, BRANCH), F32),
        scratch_shapes=[pltpu.VMEM((GLA_HEADS * GLA_DV, GLA_HEADS * GLA_DK), F32)],
        compiler_params=_params(2),
        name="gla",
    )(zd, sums, wg, bg, gn, bd)


def _outproj_kernel(x_ref, ya_ref, yb_ref, yc_ref, yd_ref, w_ref, o_ref):
    acc = x_ref[...]
    for g, y_ref in enumerate((ya_ref, yb_ref, yc_ref, yd_ref)):
        acc = acc + jnp.dot(y_ref[...].astype(BF16), w_ref[g * BRANCH:(g + 1) * BRANCH, :],
                            preferred_element_type=F32)
    o_ref[...] = acc


def _outproj(x2, ya, yb, yc, yd, w):
    n = x2.shape[0]
    row = lambda i: (i, 0)
    branch = pl.BlockSpec((ROW_TILE, BRANCH), row)
    return pl.pallas_call(
        _outproj_kernel,
        grid=(n // ROW_TILE,),
        in_specs=[pl.BlockSpec((ROW_TILE, D_MODEL), row), branch, branch, branch, branch,
                  pl.BlockSpec((D_MODEL, D_MODEL), lambda i: (0, 0))],
        out_specs=pl.BlockSpec((ROW_TILE, D_MODEL), row),
        out_shape=jax.ShapeDtypeStruct((n, D_MODEL), F32),
        compiler_params=_params(1),
        name="outproj",
    )(x2, ya, yb, yc, yd, w)


def _pack_w_in(w_in):
    gla0 = 2 * ZA_COLS + ZC_COLS
    qk = 2 * GLA_HEADS * GLA_DK
    lr0 = gla0 + qk + BRANCH
    pad = jnp.zeros((w_in.shape[0], LR_PAD - GLA_RANK), w_in.dtype)
    return jnp.concatenate([w_in[:, :lr0], w_in[:, lr0 + GLA_RANK:], w_in[:, lr0:lr0 + GLA_RANK], pad],
                           axis=1).astype(BF16)


def _layer(x2, batch, seq, consts, norm_g, w_in, q_gain_a, k_gain_a, q_gain_b, k_gain_b, conv_w, conv_b,
           conv_ln_g, conv_ln_b, conv_pw_w, conv_pw_b, gla_gate_w, gla_gate_b, gla_norm_g, w_out):
    bd, dil_table, gla_sums = consts
    heads = BRANCH // HEAD_DIM
    gains = jnp.stack([jnp.tile(g, heads) for g in (q_gain_a, k_gain_a, q_gain_b, k_gain_b)])
    (qa, ka, va, ga, kma, qb, kb, vb, gb, zc, zd) = _inproj(
        x2, norm_g[None, :], _pack_w_in(w_in), gains, bd)
    ya = _moba(qa, ka, va, kma.reshape(-1, BRANCH), ga, batch, seq)
    yb = _dilated(qb, kb, vb, dil_table, gb, batch, seq)
    yc = _conv(zc, conv_w, conv_b[None, :], conv_ln_g[None, :], conv_ln_b[None, :],
               conv_pw_w.astype(BF16), conv_pw_b[None, :], batch, seq)
    wg = jnp.concatenate([gla_gate_w, jnp.zeros((LR_PAD - GLA_RANK, gla_gate_w.shape[1]), F32)],
                         axis=0).astype(BF16)
    yd = _gla(zd, gla_sums, wg, gla_gate_b[None, :], jnp.tile(gla_norm_g, GLA_HEADS)[None, :], bd,
              batch, seq)
    return _outproj(x2, ya, yb, yc, yd, w_out.astype(BF16))


def kernel(x, norm_g, w_in, q_gain_a, k_gain_a, q_gain_b, k_gain_b, conv_w, conv_b, conv_ln_g, conv_ln_b,
           conv_pw_w, conv_pw_b, gla_gate_w, gla_gate_b, gla_norm_g, w_out):
    batch, seq, d = x.shape
    assert d == D_MODEL and seq % ROW_TILE == 0 and seq % ATT_TILE == 0
    group = np.arange(BRANCH) // HEAD_DIM
    bd = jnp.asarray((group[:, None] == group[None, :]) / HEAD_DIM, BF16)
    consts = (bd, _dilated_bias_table(), _gla_sum_matrices())
    x2 = x.reshape(batch * seq, d)
    params = (norm_g, w_in, q_gain_a, k_gain_a, q_gain_b, k_gain_b, conv_w, conv_b, conv_ln_g,
              conv_ln_b, conv_pw_w, conv_pw_b, gla_gate_w, gla_gate_b, gla_norm_g, w_out)
    for layer in range(norm_g.shape[0]):
        x2 = _layer(x2, batch, seq, consts, *(p[layer] for p in params))
    return x2.reshape(batch, seq, d)
```

```python
import numpy as np
import jax
import jax.numpy as jnp
from jax import lax
from jax.experimental import pallas as pl
from jax.experimental.pallas import tpu as pltpu

F32 = jnp.float32
BF16 = jnp.bfloat16

D_MODEL = 1024
BRANCH = 256
HEAD_DIM = 64
MOBA_BLOCK = 256
MOBA_TOPK = 3
DIL_PATTERNS = ((128, 1), (512, 4), (2048, 16))
CONV_WIDTH = 31
GLA_HEADS = 4
GLA_DK = 32
GLA_DV = 64
GLA_RANK = 16
GLA_TAU = 16.0
EPS = 1e-6
NEG = -1e30
LOG2E = 1.4426950408889634

LANES = 128
ROW_TILE = 512
ATT_TILE = 256
MOBA_GROUP = 2
DIL_REACH = max(w for w, _ in DIL_PATTERNS) // ATT_TILE
DIL_GROUP = 3
ALIBI_PIECES = 4
SEL_LANE0 = 16
MASK_BIAS = 2.0 ** 100
GLA_CHUNK = 64
GLA_LEVELS = 6
CONV_HALO = 32
LR_PAD = 128
VMEM_LIMIT = 56 * 1024 * 1024

ZA_COLS = 4 * BRANCH
ZC_COLS = 3 * BRANCH
ZD_COLS = 2 * GLA_HEADS * GLA_DK + 2 * BRANCH + LR_PAD
W_COLS = 2 * ZA_COLS + ZC_COLS + ZD_COLS

_NT = (((1,), (1,)), ((), ()))
_TN = (((0,), (0,)), ((), ()))


def _params(n_grid):
    return pltpu.CompilerParams(dimension_semantics=("arbitrary",) * n_grid,
                                vmem_limit_bytes=VMEM_LIMIT)


def _silu(x):
    return x * jax.nn.sigmoid(x)


def _group_mean_sq(z, bd):
    z2 = z * z
    hi = z2.astype(BF16)
    lo = (z2 - hi.astype(F32)).astype(BF16)
    return (jnp.dot(hi, bd, preferred_element_type=F32)
            + jnp.dot(lo, bd, preferred_element_type=F32))


def _inproj_kernel(x_ref, ng_ref, w_ref, gains_ref, bd_ref,
                   qa_ref, ka_ref, va_ref, ga_ref, kma_ref,
                   qb_ref, kb_ref, vb_ref, gb_ref, zc_ref, zd_ref):
    x = x_ref[...]
    ms = jnp.mean(x * x, axis=-1, keepdims=True)
    h = (x * lax.rsqrt(ms + EPS) * ng_ref[...]).astype(BF16)
    bd = bd_ref[...]

    def proj(c0, width):
        return jnp.dot(h, w_ref[:, c0:c0 + width], preferred_element_type=F32)

    def head_norm(z, row):
        return z * lax.rsqrt(_group_mean_sq(z, bd) + EPS) * gains_ref[row:row + 1, :]

    qa_ref[...] = head_norm(proj(0, BRANCH), 0)
    ka = head_norm(proj(BRANCH, BRANCH), 1)
    ka_ref[...] = ka.astype(BF16)
    for blk in range(ROW_TILE // MOBA_BLOCK):
        kma_ref[0, blk:blk + 1, :] = jnp.mean(
            ka[blk * MOBA_BLOCK:(blk + 1) * MOBA_BLOCK], axis=0, keepdims=True)
    va_ref[...] = proj(2 * BRANCH, BRANCH).astype(BF16)
    ga_ref[...] = proj(3 * BRANCH, BRANCH)

    c = ZA_COLS
    qb_ref[...] = (head_norm(proj(c, BRANCH), 2) * (HEAD_DIM ** -0.5 * LOG2E)).astype(BF16)
    kb_ref[...] = head_norm(proj(c + BRANCH, BRANCH), 3).astype(BF16)
    vb_ref[...] = proj(c + 2 * BRANCH, BRANCH).astype(BF16)
    gb_ref[...] = proj(c + 3 * BRANCH, BRANCH)

    c = 2 * ZA_COLS
    zc_ref[...] = proj(c, ZC_COLS)
    zd_ref[...] = proj(c + ZC_COLS, ZD_COLS)


def _inproj(x2, ng, w, gains, bd):
    n = x2.shape[0]
    nt = n // ROW_TILE
    row = lambda i: (i, 0)
    const = lambda i: (0, 0)

    def out(cols, dtype):
        return (jax.ShapeDtypeStruct((n, cols), dtype), pl.BlockSpec((ROW_TILE, cols), row))

    outs = [out(BRANCH, F32), out(BRANCH, BF16), out(BRANCH, BF16), out(BRANCH, F32),
            (jax.ShapeDtypeStruct((nt, ROW_TILE // MOBA_BLOCK, BRANCH), F32),
             pl.BlockSpec((1, ROW_TILE // MOBA_BLOCK, BRANCH), lambda i: (i, 0, 0))),
            out(BRANCH, BF16), out(BRANCH, BF16), out(BRANCH, BF16), out(BRANCH, F32),
            out(ZC_COLS, F32), out(ZD_COLS, F32)]
    return pl.pallas_call(
        _inproj_kernel,
        grid=(nt,),
        in_specs=[pl.BlockSpec((ROW_TILE, D_MODEL), row),
                  pl.BlockSpec((1, D_MODEL), const),
                  pl.BlockSpec((D_MODEL, W_COLS), const),
                  pl.BlockSpec((4, BRANCH), const),
                  pl.BlockSpec((BRANCH, BRANCH), const)],
        out_specs=[o[1] for o in outs],
        out_shape=[o[0] for o in outs],
        compiler_params=_params(1),
        name="inproj",
    )(x2, ng, w, gains, bd)


def _key_position_lanes(seq):
    pos = np.arange(seq)
    c, n = pos % ATT_TILE, pos // ATT_TILE
    kx = np.zeros((seq, LANES), np.float32)
    p = ALIBI_PIECES
    kx[:, 0:p] = (c // 16)[:, None]
    kx[:, p:2 * p] = (c % 16)[:, None]
    kx[:, 2 * p:3 * p] = n[:, None]
    kx[pos, SEL_LANE0 + n] = 1.0
    return jnp.asarray(kx, BF16)


def _query_alibi_lanes(slopes):
    pieces, rest = [], LOG2E
    for _ in range(ALIBI_PIECES):
        piece = float(np.asarray(rest, dtype=BF16).astype(np.float64))
        pieces.append(piece)
        rest -= piece
    p = ALIBI_PIECES
    qx = np.zeros((len(slopes), 1, LANES), np.float32)
    for h, slope in enumerate(slopes):
        for g, weight in enumerate((16.0, 1.0, float(ATT_TILE))):
            qx[h, 0, g * p:(g + 1) * p] = [weight * slope * piece for piece in pieces]
    return jnp.asarray(qx, BF16)


def _attend(qfull, kfull, vb, mask_fn, lo_half, carry):
    m, l, acc = carry
    p, m, l, alpha = _softmax_step(_scores(qfull, kfull), m, l, mask_fn)
    return m, l, _accumulate(p, alpha, vb, acc, lo_half)


def _scores(qfull, kfull):
    return tuple(lax.dot_general(qh, kfull, _NT, preferred_element_type=F32) for qh in qfull)


def _softmax_step(s, m, l, mask_fn):
    p_out, m_out, l_out, alpha = [], [], [], []
    for h in range(2):
        sh = s[h] if mask_fn is None else mask_fn(s[h])
        m_new = jnp.maximum(m[h], jnp.max(sh, axis=-1, keepdims=True))
        p = jnp.exp2(sh - m_new)
        a = jnp.exp2(m[h] - m_new)
        p_out.append(p.astype(BF16))
        m_out.append(m_new)
        l_out.append(a * l[h] + jnp.sum(p, axis=-1, keepdims=True))
        alpha.append(a)
    return tuple(p_out), tuple(m_out), tuple(l_out), tuple(alpha)


def _accumulate(p, alpha, vb, acc, lo_half):
    zero = jnp.zeros_like(vb)
    v_lo = lax.broadcasted_iota(jnp.int32, vb.shape, 1) < HEAD_DIM
    pv = (jnp.dot(p[0], jnp.where(v_lo, vb, zero), preferred_element_type=F32)
          + jnp.dot(p[1], jnp.where(v_lo, zero, vb), preferred_element_type=F32))
    return jnp.where(lo_half, alpha[0], alpha[1]) * acc + pv


def _attn_init():
    col = lambda v: jnp.full((ATT_TILE, 1), v, F32)
    return (col(NEG), col(NEG)), (col(0.0), col(0.0)), jnp.zeros((ATT_TILE, LANES), F32)


def _attn_finish(carry, gate, lo_half):
    _, l, acc = carry
    return acc / jnp.where(lo_half, l[0], l[1]) * _silu(gate)


def _slab_rows(first_block, n_blocks):
    return pl.ds(pl.multiple_of(first_block * ATT_TILE, ATT_TILE), n_blocks * ATT_TILE)


def _key_slab(k_ref, kx_ref, v_ref, first_block, n_blocks):
    rows = _slab_rows(first_block, n_blocks)
    return jnp.concatenate([k_ref[rows, :], kx_ref[rows, :]], axis=1), v_ref[rows, :]


def _top_k_rows(gate, row_f):
    sel = jnp.zeros(gate.shape, F32)
    for _ in range(MOBA_TOPK):
        top = jnp.max(gate, axis=0, keepdims=True)
        first = jnp.min(jnp.where(gate == top, row_f, 1e9), axis=0, keepdims=True)
        pick = row_f == first
        sel = jnp.where(pick, 1.0, sel)
        gate = jnp.where(pick, -jnp.inf, gate)
    return sel > 0.5


def _moba_kernel(q_ref, k_ref, v_ref, km_ref, g_ref, kx_ref, qx_ref, o_ref,
                 qf_buf, s_buf, p_buf, m_buf, l_buf, a_buf, acc_buf):
    i = pl.program_id(2)
    n_blk = km_ref.shape[0]
    lo_half = lax.broadcasted_iota(jnp.int32, (ATT_TILE, LANES), 1) < HEAD_DIM
    q = q_ref[...]
    q_heads = (jnp.where(lo_half, q, 0.0), jnp.where(lo_half, 0.0, q))

    km = km_ref[...]
    blk = lax.broadcasted_iota(jnp.int32, (n_blk, ATT_TILE), 0)
    blk_f = blk.astype(F32)
    past = blk < i
    for h, qh in enumerate(q_heads):
        gate = lax.dot_general(km, qh, _NT, precision=lax.Precision.HIGHEST,
                               preferred_element_type=F32)
        keep = (past & _top_k_rows(jnp.where(past, gate, -jnp.inf), blk_f)) | (blk == i)
        bias = jnp.concatenate([jnp.zeros((SEL_LANE0, ATT_TILE), F32),
                                jnp.where(keep, 0.0, -MASK_BIAS),
                                jnp.zeros((LANES - SEL_LANE0 - n_blk, ATT_TILE), F32)], axis=0)
        extra = bias.T + qx_ref[h].astype(F32)
        qf_buf[h] = jnp.concatenate(
            [(qh * (HEAD_DIM ** -0.5 * LOG2E)).astype(BF16), extra.astype(BF16)], axis=1)

    width = MOBA_GROUP * ATT_TILE
    ahead = (lax.broadcasted_iota(jnp.int32, (ATT_TILE, width), 1)
             - lax.broadcasted_iota(jnp.int32, (ATT_TILE, width), 0))

    def issue_scores(j):
        rows = _slab_rows(j * MOBA_GROUP, MOBA_GROUP)
        kfull = jnp.concatenate([k_ref[rows, :], kx_ref[rows, :]], axis=1)
        for h in range(2):
            s_buf[h] = lax.dot_general(qf_buf[h], kfull, _NT, preferred_element_type=F32)

    def softmax(mask_fn):
        for h in range(2):
            s = s_buf[h] if mask_fn is None else mask_fn(s_buf[h])
            m_old = m_buf[h]
            m_new = jnp.maximum(m_old, jnp.max(s, axis=-1, keepdims=True))
            p = jnp.exp2(s - m_new)
            a = jnp.exp2(m_old - m_new)
            m_buf[h] = m_new
            l_buf[h] = a * l_buf[h] + jnp.sum(p, axis=-1, keepdims=True)
            a_buf[h] = a
            p_buf[h] = p.astype(BF16)

    def fold_values(j):
        vb = v_ref[_slab_rows(j * MOBA_GROUP, MOBA_GROUP), :]
        zero = jnp.zeros_like(vb)
        v_lo = lax.broadcasted_iota(jnp.int32, vb.shape, 1) < HEAD_DIM
        pv = (jnp.dot(p_buf[0], jnp.where(v_lo, vb, zero), preferred_element_type=F32)
              + jnp.dot(p_buf[1], jnp.where(v_lo, zero, vb), preferred_element_type=F32))
        acc_buf[...] = jnp.where(lo_half, a_buf[0], a_buf[1]) * acc_buf[...] + pv

    m_buf[...] = jnp.full(m_buf.shape, NEG, F32)
    l_buf[...] = jnp.zeros(l_buf.shape, F32)
    a_buf[...] = jnp.ones(a_buf.shape, F32)
    p_buf[...] = jnp.zeros(p_buf.shape, BF16)
    acc_buf[...] = jnp.zeros(acc_buf.shape, F32)
    issue_scores(0)

    last = i // MOBA_GROUP

    def step(j, _):
        fold_values(jnp.maximum(j - 1, 0))
        softmax(None)
        issue_scores(j + 1)
        return 0

    lax.fori_loop(0, last, step, 0)
    fold_values(jnp.maximum(last - 1, 0))
    own_start = (i - last * MOBA_GROUP) * ATT_TILE
    softmax(lambda s: jnp.where(ahead <= own_start, s, NEG))
    fold_values(last)
    o_ref[...] = (acc_buf[...] / jnp.where(lo_half, l_buf[0], l_buf[1])) * _silu(g_ref[...])


def _moba(q, k, v, kmean, g, kx, qx, batch, seq):
    nq = seq // ATT_TILE
    assert nq % MOBA_GROUP == 0 and SEL_LANE0 + nq <= LANES
    tile = pl.BlockSpec((ATT_TILE, LANES), lambda b, hp, i: (b * nq + i, hp))
    whole = pl.BlockSpec((seq, LANES), lambda b, hp, i: (b, hp))
    return pl.pallas_call(
        _moba_kernel,
        grid=(batch, BRANCH // LANES, nq),
        in_specs=[tile, whole, whole,
                  pl.BlockSpec((seq // MOBA_BLOCK, LANES), lambda b, hp, i: (b, hp)),
                  tile,
                  pl.BlockSpec((seq, LANES), lambda b, hp, i: (0, 0)),
                  pl.BlockSpec((LANES // HEAD_DIM, 1, LANES), lambda b, hp, i: (hp, 0, 0))],
        out_specs=tile,
        out_shape=jax.ShapeDtypeStruct(q.shape, F32),
        scratch_shapes=[pltpu.VMEM((2, ATT_TILE, 2 * LANES), BF16),
                        pltpu.VMEM((2, ATT_TILE, MOBA_GROUP * ATT_TILE), F32),
                        pltpu.VMEM((2, ATT_TILE, MOBA_GROUP * ATT_TILE), BF16),
                        pltpu.VMEM((2, ATT_TILE, 1), F32),
                        pltpu.VMEM((2, ATT_TILE, 1), F32),
                        pltpu.VMEM((2, ATT_TILE, 1), F32),
                        pltpu.VMEM((ATT_TILE, LANES), F32)],
        compiler_params=_params(3),
        name="moba",
    )(q, k, v, kmean, g, kx, qx)


def _dilated_multiplicity_tables():
    r = np.arange(ATT_TILE)
    delta = (np.arange(DIL_REACH + 1)[:, None, None] * ATT_TILE + r[None, :, None] - r[None, None, :])
    mult = np.zeros(delta.shape, np.float64)
    for window, dil in DIL_PATTERNS:
        mult += (delta >= 0) & (delta <= window) & (delta % dil == 0)
    per_offset = np.where(mult > 0, np.log2(np.maximum(mult, 1.0)), NEG)
    groups = [np.concatenate([per_offset[DIL_REACH - DIL_GROUP * g - b] for b in range(DIL_GROUP)], axis=1)
              for g in range((DIL_REACH + 1) // DIL_GROUP)]
    return jnp.asarray(per_offset, F32), jnp.asarray(np.stack(groups), F32)


def _dilated_kernel(q_ref, k_ref, v_ref, t1_ref, tg_ref, g_ref, kx_ref, qx_ref, o_ref):
    i = pl.program_id(2)
    lo_half = lax.broadcasted_iota(jnp.int32, (ATT_TILE, LANES), 1) < HEAD_DIM
    q = q_ref[...]
    zero = jnp.zeros_like(q)
    qfull = tuple(
        jnp.concatenate([qh, jnp.broadcast_to(qx_ref[h], (ATT_TILE, LANES))], axis=1)
        for h, qh in enumerate((jnp.where(lo_half, q, zero), jnp.where(lo_half, zero, q))))

    @pl.when(i >= DIL_REACH)
    def _():
        carry = _attn_init()
        for g in range((DIL_REACH + 1) // DIL_GROUP):
            kfull, vb = _key_slab(k_ref, kx_ref, v_ref, i - DIL_REACH + g * DIL_GROUP, DIL_GROUP)
            carry = _attend(qfull, kfull, vb, lambda s: s + tg_ref[g], lo_half, carry)
        o_ref[...] = _attn_finish(carry, g_ref[...], lo_half)

    @pl.when(i < DIL_REACH)
    def _():
        def block(o, carry):
            kfull, vb = _key_slab(k_ref, kx_ref, v_ref, i - o, 1)
            return _attend(qfull, kfull, vb, lambda s: s + t1_ref[o], lo_half, carry)

        carry = lax.fori_loop(0, i + 1, block, _attn_init())
        o_ref[...] = _attn_finish(carry, g_ref[...], lo_half)


def _dilated(q, k, v, tables, g, kx, qx, batch, seq):
    nq = seq // ATT_TILE
    per_offset, per_group = tables
    tile = pl.BlockSpec((ATT_TILE, LANES), lambda b, hp, i: (b * nq + i, hp))
    whole = pl.BlockSpec((seq, LANES), lambda b, hp, i: (b, hp))
    return pl.pallas_call(
        _dilated_kernel,
        grid=(batch, BRANCH // LANES, nq),
        in_specs=[tile, whole, whole,
                  pl.BlockSpec(per_offset.shape, lambda b, hp, i: (0, 0, 0)),
                  pl.BlockSpec(per_group.shape, lambda b, hp, i: (0, 0, 0)),
                  tile,
                  pl.BlockSpec((seq, LANES), lambda b, hp, i: (0, 0)),
                  pl.BlockSpec((LANES // HEAD_DIM, 1, LANES), lambda b, hp, i: (hp, 0, 0))],
        out_specs=tile,
        out_shape=jax.ShapeDtypeStruct(q.shape, F32),
        compiler_params=_params(3),
        name="dilated",
    )(q, k, v, per_offset, per_group, g, kx, qx)


def _conv_kernel(z_ref, halo_ref, w_ref, b_ref, lng_ref, lnb_ref, pw_ref, pwb_ref, o_ref, u_buf):
    j = pl.program_id(1)

    def glu(z):
        return z[:, 0:BRANCH] * jax.nn.sigmoid(z[:, BRANCH:2 * BRANCH])

    z = z_ref[...]
    u_buf[0:CONV_HALO, :] = jnp.where(j > 0, glu(halo_ref[...]), 0.0)
    u_buf[CONV_HALO:, :] = glu(z)
    acc = jnp.zeros((ROW_TILE, BRANCH), F32) + b_ref[...]
    first = CONV_HALO - (CONV_WIDTH - 1)
    for tap in range(CONV_WIDTH):
        acc = acc + w_ref[tap:tap + 1, :] * u_buf[first + tap:first + tap + ROW_TILE, :]
    mu = jnp.mean(acc, axis=-1, keepdims=True)
    cen = acc - mu
    var = jnp.mean(cen * cen, axis=-1, keepdims=True)
    un = cen * lax.rsqrt(var + EPS) * lng_ref[...] + lnb_ref[...]
    y = jnp.dot(_silu(un).astype(BF16), pw_ref[...], preferred_element_type=F32) + pwb_ref[...]
    o_ref[...] = y * _silu(z[:, 2 * BRANCH:3 * BRANCH])


def _conv(zc, w, b, lng, lnb, pw, pwb, batch, seq):
    nt = seq // ROW_TILE
    per = ROW_TILE // CONV_HALO
    const = lambda bi, j: (0, 0)
    return pl.pallas_call(
        _conv_kernel,
        grid=(batch, nt),
        in_specs=[pl.BlockSpec((ROW_TILE, ZC_COLS), lambda bi, j: (bi * nt + j, 0)),
                  pl.BlockSpec((CONV_HALO, ZC_COLS),
                               lambda bi, j: (jnp.maximum((bi * nt + j) * per - 1, 0), 0)),
                  pl.BlockSpec((CONV_WIDTH, BRANCH), const),
                  pl.BlockSpec((1, BRANCH), const),
                  pl.BlockSpec((1, BRANCH), const),
                  pl.BlockSpec((1, BRANCH), const),
                  pl.BlockSpec((BRANCH, BRANCH), const),
                  pl.BlockSpec((1, BRANCH), const)],
        out_specs=pl.BlockSpec((ROW_TILE, BRANCH), lambda bi, j: (bi * nt + j, 0)),
        out_shape=jax.ShapeDtypeStruct((batch * seq, BRANCH), F32),
        scratch_shapes=[pltpu.VMEM((CONV_HALO + ROW_TILE, BRANCH), F32)],
        compiler_params=_params(2),
        name="conv",
    )(zc, zc, w, b, lng, lnb, pw, pwb)


def _gla_sum_matrices():
    c = GLA_CHUNK
    i = np.arange(c)[:, None]
    t = np.arange(c)[None, :]
    mats = [t <= i]
    for l in range(GLA_LEVELS):
        h = (c // 2) >> l
        mid = (i // (2 * h)) * (2 * h) + h
        mats.append(((i & h) != 0) & (t >= mid) & (t <= i))
    for l in range(GLA_LEVELS):
        h = (c // 2) >> l
        mid = (i // (2 * h)) * (2 * h) + h
        mats.append(((i & h) == 0) & (t > i) & (t < mid))
    return jnp.asarray(np.concatenate(mats, axis=0), BF16)


def _gla_kernel(z_ref, sums_ref, wg_ref, bg_ref, gn_ref, bd_ref, o_ref, state_ref):
    c = GLA_CHUNK
    nh = GLA_HEADS
    kw = nh * GLA_DK
    vw = nh * GLA_DV

    @pl.when(pl.program_id(1) == 0)
    def _():
        state_ref[...] = jnp.zeros_like(state_ref)

    row = lax.broadcasted_iota(jnp.int32, (c, 1), 0)
    qi = lax.broadcasted_iota(jnp.int32, (c, nh * c), 0)
    kj = lax.broadcasted_iota(jnp.int32, (c, nh * c), 1) % c
    level_mask = [(qi >> (GLA_LEVELS - l)) == (kj >> (GLA_LEVELS - l)) for l in range(GLA_LEVELS)]
    diag_mask = qi == kj
    k_head = (lax.broadcasted_iota(jnp.int32, (nh * c, kw), 0) // c
              == lax.broadcasted_iota(jnp.int32, (nh * c, kw), 1) // GLA_DK)
    v_head = (lax.broadcasted_iota(jnp.int32, (nh * c, vw), 0) // c
              == lax.broadcasted_iota(jnp.int32, (nh * c, vw), 1) // GLA_DV)
    s_head = (lax.broadcasted_iota(jnp.int32, (vw, kw), 0) // GLA_DV
              == lax.broadcasted_iota(jnp.int32, (vw, kw), 1) // GLA_DK)

    def per_head_keys(kt):
        return jnp.where(k_head, jnp.concatenate([kt] * nh, axis=0), 0.0).astype(BF16)

    def chunk(ci, _):
        rows = pl.ds(pl.multiple_of(ci * c, c), c)
        q = z_ref[rows, 0:kw] * GLA_DK ** -0.5
        k = z_ref[rows, kw:2 * kw]
        v = z_ref[rows, 2 * kw:2 * kw + vw]
        gd = z_ref[rows, 2 * kw + vw:2 * kw + 2 * vw]
        lr = z_ref[rows, 2 * kw + 2 * vw:2 * kw + 2 * vw + LR_PAD]

        g = jnp.dot(lr.astype(BF16), wg_ref[...], preferred_element_type=F32) + bg_ref[...]
        la = (jnp.minimum(g, 0.0) - jnp.log(1.0 + jnp.exp(-jnp.abs(g)))) / GLA_TAU
        a1 = la.astype(BF16)
        r1 = la - a1.astype(F32)
        a2 = r1.astype(BF16)
        a3 = (r1 - a2.astype(F32)).astype(BF16)
        parts = jnp.dot(sums_ref[...], jnp.concatenate([a1, a2, a3], axis=1),
                        preferred_element_type=F32)
        sums = parts[:, 0:kw] + parts[:, kw:2 * kw] + parts[:, 2 * kw:3 * kw]
        bc = sums[0:c]

        attn = jnp.where(diag_mask,
                         lax.dot_general(q.astype(BF16), per_head_keys(k), _NT,
                                         preferred_element_type=F32), 0.0)
        for l in range(GLA_LEVELS):
            later = (row & ((c // 2) >> l)) != 0
            since_mid = sums[(1 + l) * c:(2 + l) * c]
            until_mid = sums[(1 + GLA_LEVELS + l) * c:(2 + GLA_LEVELS + l) * c]
            qt = jnp.where(later, q * jnp.exp(since_mid), 0.0).astype(BF16)
            kt = jnp.where(later, 0.0, k * jnp.exp(until_mid))
            a = lax.dot_general(qt, per_head_keys(kt), _NT, preferred_element_type=F32)
            attn = attn + jnp.where(level_mask[l], a, 0.0)

        vb = v.astype(BF16)
        v_stack = jnp.where(v_head, jnp.concatenate([vb] * nh, axis=0), jnp.zeros((), BF16))
        o = jnp.dot(attn.astype(BF16), v_stack, preferred_element_type=F32)

        state = state_ref[...]
        o = o + lax.dot_general((q * jnp.exp(bc)).astype(BF16), state.astype(BF16), _NT,
                                preferred_element_type=F32)
        b_last = bc[c - 1:c, :]
        k_dec = (k * jnp.exp(b_last - bc)).astype(BF16)
        upd = lax.dot_general(vb, k_dec, _TN, preferred_element_type=F32)
        state_ref[...] = state * jnp.exp(b_last) + jnp.where(s_head, upd, 0.0)

        on = o * lax.rsqrt(_group_mean_sq(o, bd_ref[...]) + EPS) * gn_ref[...]
        o_ref[rows, :] = on * _silu(gd)
        return 0

    lax.fori_loop(0, ROW_TILE // c, chunk, 0)


def _gla(zd, sums, wg, bg, gn, bd, batch, seq):
    nt = seq // ROW_TILE
    const = lambda bi, j: (0, 0)
    return pl.pallas_call(
        _gla_kernel,
        grid=(batch, nt),
        in_specs=[pl.BlockSpec((ROW_TILE, ZD_COLS), lambda bi, j: (bi * nt + j, 0)),
                  pl.BlockSpec(sums.shape, const),
                  pl.BlockSpec((LR_PAD, GLA_HEADS * GLA_DK), const),
                  pl.BlockSpec((1, GLA_HEADS * GLA_DK), const),
                  pl.BlockSpec((1, BRANCH), const),
                  pl.BlockSpec((BRANCH, BRANCH), const)],
        out_specs=pl.BlockSpec((ROW_TILE, BRANCH), lambda bi, j: (bi * nt + j, 0)),
        out_shape=jax.ShapeDtypeStruct((batch * seq, BRANCH), F32),
        scratch_shapes=[pltpu.VMEM((GLA_HEADS * GLA_DV, GLA_HEADS * GLA_DK), F32)],
        compiler_params=_params(2),
        name="gla",
    )(zd, sums, wg, bg, gn, bd)


def _outproj_kernel(x_ref, ya_ref, yb_ref, yc_ref, yd_ref, w_ref, o_ref):
    acc = x_ref[...]
    for g, y_ref in enumerate((ya_ref, yb_ref, yc_ref, yd_ref)):
        acc = acc + jnp.dot(y_ref[...].astype(BF16), w_ref[g * BRANCH:(g + 1) * BRANCH, :],
                            preferred_element_type=F32)
    o_ref[...] = acc


def _outproj(x2, ya, yb, yc, yd, w):
    n = x2.shape[0]
    row = lambda i: (i, 0)
    branch = pl.BlockSpec((ROW_TILE, BRANCH), row)
    return pl.pallas_call(
        _outproj_kernel,
        grid=(n // ROW_TILE,),
        in_specs=[pl.BlockSpec((ROW_TILE, D_MODEL), row), branch, branch, branch, branch,
                  pl.BlockSpec((D_MODEL, D_MODEL), lambda i: (0, 0))],
        out_specs=pl.BlockSpec((ROW_TILE, D_MODEL), row),
        out_shape=jax.ShapeDtypeStruct((n, D_MODEL), F32),
        compiler_params=_params(1),
        name="outproj",
    )(x2, ya, yb, yc, yd, w)


def _pack_w_in(w_in):
    gla0 = 2 * ZA_COLS + ZC_COLS
    qk = 2 * GLA_HEADS * GLA_DK
    lr0 = gla0 + qk + BRANCH
    pad = jnp.zeros((w_in.shape[0], LR_PAD - GLA_RANK), w_in.dtype)
    return jnp.concatenate([w_in[:, :lr0], w_in[:, lr0 + GLA_RANK:], w_in[:, lr0:lr0 + GLA_RANK], pad],
                           axis=1).astype(BF16)


def _layer(x2, batch, seq, consts, norm_g, w_in, q_gain_a, k_gain_a, q_gain_b, k_gain_b, conv_w, conv_b,
           conv_ln_g, conv_ln_b, conv_pw_w, conv_pw_b, gla_gate_w, gla_gate_b, gla_norm_g, w_out):
    bd, kx, qx_moba, qx_dil, dil_tables, gla_sums = consts
    heads = BRANCH // HEAD_DIM
    gains = jnp.stack([jnp.tile(g, heads) for g in (q_gain_a, k_gain_a, q_gain_b, k_gain_b)])
    (qa, ka, va, ga, kma, qb, kb, vb, gb, zc, zd) = _inproj(
        x2, norm_g[None, :], _pack_w_in(w_in), gains, bd)
    ya = _moba(qa, ka, va, kma.reshape(-1, BRANCH), ga, kx, qx_moba, batch, seq)
    yb = _dilated(qb, kb, vb, dil_tables, gb, kx, qx_dil, batch, seq)
    yc = _conv(zc, conv_w, conv_b[None, :], conv_ln_g[None, :], conv_ln_b[None, :],
               conv_pw_w.astype(BF16), conv_pw_b[None, :], batch, seq)
    wg = jnp.concatenate([gla_gate_w, jnp.zeros((LR_PAD - GLA_RANK, gla_gate_w.shape[1]), F32)],
                         axis=0).astype(BF16)
    yd = _gla(zd, gla_sums, wg, gla_gate_b[None, :], jnp.tile(gla_norm_g, GLA_HEADS)[None, :], bd,
              batch, seq)
    return _outproj(x2, ya, yb, yc, yd, w_out.astype(BF16))


def kernel(x, norm_g, w_in, q_gain_a, k_gain_a, q_gain_b, k_gain_b, conv_w, conv_b, conv_ln_g, conv_ln_b,
           conv_pw_w, conv_pw_b, gla_gate_w, gla_gate_b, gla_norm_g, w_out):
    batch, seq, d = x.shape
    assert d == D_MODEL and seq % ROW_TILE == 0 and seq % ATT_TILE == 0
    group = np.arange(BRANCH) // HEAD_DIM
    bd = jnp.asarray((group[:, None] == group[None, :]) / HEAD_DIM, BF16)
    heads = np.arange(BRANCH // HEAD_DIM)
    consts = (bd, _key_position_lanes(seq),
              _query_alibi_lanes(2.0 ** -(1.0 + 2 * heads)),
              _query_alibi_lanes(2.0 ** -(2.0 + 2 * heads)),
              _dilated_multiplicity_tables(), _gla_sum_matrices())
    x2 = x.reshape(batch * seq, d)
    params = (norm_g, w_in, q_gain_a, k_gain_a, q_gain_b, k_gain_b, conv_w, conv_b, conv_ln_g,
              conv_ln_b, conv_pw_w, conv_pw_b, gla_gate_w, gla_gate_b, gla_norm_g, w_out)
    for layer in range(norm_g.shape[0]):
        x2 = _layer(x2, batch, seq, consts, *(p[layer] for p in params))
    return x2.reshape(batch, seq, d)
```

```python
import numpy as np
import jax
import jax.numpy as jnp
from jax import lax
from jax.experimental import pallas as pl
from jax.experimental.pallas import tpu as pltpu

F32 = jnp.float32
BF16 = jnp.bfloat16

D_MODEL = 1024
BRANCH = 256
HEAD_DIM = 64
MOBA_BLOCK = 256
MOBA_TOPK = 3
DIL_PATTERNS = ((128, 1), (512, 4), (2048, 16))
CONV_WIDTH = 31
GLA_HEADS = 4
GLA_DK = 32
GLA_DV = 64
GLA_RANK = 16
GLA_TAU = 16.0
EPS = 1e-6
NEG = -1e30
LOG2E = 1.4426950408889634

LANES = 128
ROW_TILE = 512
ATT_TILE = 512
BLOCKS_PER_TILE = ATT_TILE // MOBA_BLOCK
DIL_GROUPS_BACK = max(w for w, _ in DIL_PATTERNS) // ATT_TILE
ALIBI_PIECES = 4
SEL_LANE0 = 16
MASK_BIAS = 2.0 ** 100
M_INIT = -1e29
GLA_CHUNK = 64
GLA_LEVELS = 6
CONV_HALO = 32
LR_PAD = 128
VMEM_LIMIT = 56 * 1024 * 1024

WT_ROWS = 4 * BRANCH
ZC_COLS = 3 * BRANCH
ZD_COLS = 2 * GLA_HEADS * GLA_DK + 2 * BRANCH + LR_PAD
WN_COLS = 4 * BRANCH + ZC_COLS + ZD_COLS

_NT = (((1,), (1,)), ((), ()))
_TN = (((0,), (0,)), ((), ()))


def _params(n_grid):
    return pltpu.CompilerParams(dimension_semantics=("arbitrary",) * n_grid,
                                vmem_limit_bytes=VMEM_LIMIT)


def _silu(x):
    return x * jax.nn.sigmoid(x)


def _group_mean_sq(z, bd):
    z2 = z * z
    hi = z2.astype(BF16)
    lo = (z2 - hi.astype(F32)).astype(BF16)
    return (jnp.dot(hi, bd, preferred_element_type=F32)
            + jnp.dot(lo, bd, preferred_element_type=F32))


def _inproj_kernel(x_ref, ng_ref, wn_ref, wt_ref, kgain_ref, qgain_ref, bd_ref, kx_ref,
                   qat_ref, ka_ref, vat_ref, ga_ref, kma_ref,
                   qbt_ref, kb_ref, vbt_ref, gb_ref, zc_ref, zd_ref):
    x = x_ref[...]
    ms = jnp.mean(x * x, axis=-1, keepdims=True)
    h = (x * lax.rsqrt(ms + EPS) * ng_ref[...]).astype(BF16)
    bd = bd_ref[...]
    kx = kx_ref[...]

    def proj(c0, width):
        return jnp.dot(h, wn_ref[:, c0:c0 + width], preferred_element_type=F32)

    def proj_t(r0):
        return lax.dot_general(wt_ref[r0:r0 + BRANCH, :], h, _NT, preferred_element_type=F32)

    def head_norm(z, row):
        return z * lax.rsqrt(_group_mean_sq(z, bd) + EPS) * kgain_ref[row:row + 1, :]

    def head_norm_t(zt, idx):
        parts = []
        for g in range(BRANCH // HEAD_DIM):
            part = zt[g * HEAD_DIM:(g + 1) * HEAD_DIM]
            parts.append(part * lax.rsqrt(jnp.mean(part * part, axis=0, keepdims=True) + EPS))
        return jnp.concatenate(parts, axis=0) * qgain_ref[idx]

    def store_keys(ref, kn):
        for hp in range(BRANCH // LANES):
            ref[:, 2 * hp * LANES:(2 * hp + 1) * LANES] = kn[:, hp * LANES:(hp + 1) * LANES].astype(BF16)
            ref[:, (2 * hp + 1) * LANES:(2 * hp + 2) * LANES] = kx

    qat_ref[0] = head_norm_t(proj_t(0), 0)
    ka = head_norm(proj(0, BRANCH), 0)
    store_keys(ka_ref, ka)
    for blk in range(ROW_TILE // MOBA_BLOCK):
        kma_ref[0, blk:blk + 1, :] = jnp.mean(
            ka[blk * MOBA_BLOCK:(blk + 1) * MOBA_BLOCK], axis=0, keepdims=True)
    vat_ref[0] = proj_t(BRANCH).astype(BF16)
    ga_ref[...] = proj(BRANCH, BRANCH)

    qbt_ref[0] = (head_norm_t(proj_t(2 * BRANCH), 1) * (HEAD_DIM ** -0.5 * LOG2E)).astype(BF16)
    store_keys(kb_ref, head_norm(proj(2 * BRANCH, BRANCH), 1))
    vbt_ref[0] = proj_t(3 * BRANCH).astype(BF16)
    gb_ref[...] = proj(3 * BRANCH, BRANCH)

    zc_ref[...] = proj(4 * BRANCH, ZC_COLS)
    zd_ref[...] = proj(4 * BRANCH + ZC_COLS, ZD_COLS)


def _inproj(x2, ng, wn, wt, kgains, qgains, bd, kx, seq):
    n = x2.shape[0]
    nt = n // ROW_TILE
    per_seq = seq // ROW_TILE
    row = lambda i: (i, 0)
    const = lambda i: (0, 0)

    def nat(cols, dtype):
        return (jax.ShapeDtypeStruct((n, cols), dtype), pl.BlockSpec((ROW_TILE, cols), row))

    def tr(dtype):
        return (jax.ShapeDtypeStruct((nt, BRANCH, ROW_TILE), dtype),
                pl.BlockSpec((1, BRANCH, ROW_TILE), lambda i: (i, 0, 0)))

    kmean = (jax.ShapeDtypeStruct((nt, ROW_TILE // MOBA_BLOCK, BRANCH), F32),
             pl.BlockSpec((1, ROW_TILE // MOBA_BLOCK, BRANCH), lambda i: (i, 0, 0)))
    outs = [tr(F32), nat(2 * BRANCH, BF16), tr(BF16), nat(BRANCH, F32), kmean,
            tr(BF16), nat(2 * BRANCH, BF16), tr(BF16), nat(BRANCH, F32),
            nat(ZC_COLS, F32), nat(ZD_COLS, F32)]
    return pl.pallas_call(
        _inproj_kernel,
        grid=(nt,),
        in_specs=[pl.BlockSpec((ROW_TILE, D_MODEL), row),
                  pl.BlockSpec((1, D_MODEL), const),
                  pl.BlockSpec((D_MODEL, WN_COLS), const),
                  pl.BlockSpec((WT_ROWS, D_MODEL), const),
                  pl.BlockSpec((2, BRANCH), const),
                  pl.BlockSpec((2, BRANCH, 1), lambda i: (0, 0, 0)),
                  pl.BlockSpec((BRANCH, BRANCH), const),
                  pl.BlockSpec((ROW_TILE, LANES), lambda i: (i % per_seq, 0))],
        out_specs=[o[1] for o in outs],
        out_shape=[o[0] for o in outs],
        compiler_params=_params(1),
        name="inproj",
    )(x2, ng, wn, wt, kgains, qgains, bd, kx)


def _key_position_lanes(seq):
    pos = np.arange(seq)
    c, n = pos % MOBA_BLOCK, pos // MOBA_BLOCK
    kx = np.zeros((seq, LANES), np.float32)
    p = ALIBI_PIECES
    kx[:, 0:p] = (c // 16)[:, None]
    kx[:, p:2 * p] = (c % 16)[:, None]
    kx[:, 2 * p:3 * p] = n[:, None]
    kx[pos, SEL_LANE0 + n] = 1.0
    return jnp.asarray(kx, BF16)


def _query_alibi_rows(slopes):
    pieces, rest = [], LOG2E
    for _ in range(ALIBI_PIECES):
        piece = float(np.asarray(rest, dtype=BF16).astype(np.float64))
        pieces.append(piece)
        rest -= piece
    p = ALIBI_PIECES
    qx = np.zeros((len(slopes), LANES, ATT_TILE), np.float32)
    for h, slope in enumerate(slopes):
        for g, weight in enumerate((16.0, 1.0, float(MOBA_BLOCK))):
            qx[h, g * p:(g + 1) * p, :] = np.asarray([weight * slope * piece for piece in pieces])[:, None]
    return jnp.asarray(qx, F32)


def _flash_sweep(n_steps, group_of, loop_mask, last_mask, k_ref, vt_ref, g_ref, o_ref,
                 qft_buf, s_buf, p_buf, m_buf, l_buf, a_buf, acc_buf):
    def issue_scores(t):
        rows = pl.ds(pl.multiple_of(group_of(t) * ATT_TILE, ATT_TILE), ATT_TILE)
        keys = k_ref[rows, :]
        for h in range(2):
            s_buf[h] = jnp.dot(keys, qft_buf[h], preferred_element_type=F32)

    def softmax(mask):
        for h in range(2):
            s = s_buf[h] if mask is None else mask(s_buf[h])
            m_old = m_buf[h]
            m_new = jnp.maximum(m_old, jnp.max(s, axis=0, keepdims=True))
            p = jnp.exp2(s - m_new)
            a = jnp.exp2(m_old - m_new)
            m_buf[h] = m_new
            l_buf[h] = a * l_buf[h] + jnp.sum(p, axis=0, keepdims=True)
            a_buf[h] = a
            p_buf[h] = p.astype(BF16)

    def fold_values(t):
        vt = vt_ref[group_of(t)]
        for h in range(2):
            rows = slice(h * HEAD_DIM, (h + 1) * HEAD_DIM)
            acc_buf[rows, :] = (a_buf[h] * acc_buf[rows, :]
                                + jnp.dot(vt[rows, :], p_buf[h], preferred_element_type=F32))

    m_buf[...] = jnp.full(m_buf.shape, M_INIT, F32)
    l_buf[...] = jnp.zeros(l_buf.shape, F32)
    a_buf[...] = jnp.ones(a_buf.shape, F32)
    p_buf[...] = jnp.zeros(p_buf.shape, BF16)
    acc_buf[...] = jnp.zeros(acc_buf.shape, F32)
    issue_scores(0)

    def step(t, _):
        fold_values(jnp.maximum(t - 1, 0))
        softmax(None if loop_mask is None else (lambda s: loop_mask(t, s)))
        issue_scores(t + 1)
        return 0

    lax.fori_loop(0, n_steps - 1, step, 0)
    fold_values(jnp.maximum(n_steps - 2, 0))
    softmax(lambda s: last_mask(n_steps - 1, s))
    fold_values(n_steps - 1)
    out_t = jnp.concatenate([acc_buf[h * HEAD_DIM:(h + 1) * HEAD_DIM, :] / l_buf[h] for h in range(2)],
                            axis=0)
    o_ref[...] = out_t.T * _silu(g_ref[...])


def _head_rows(shape):
    first = lax.broadcasted_iota(jnp.int32, shape, 0) < HEAD_DIM
    return first, jnp.logical_not(first)


def _attn_scratch():
    return [pltpu.VMEM((2, 2 * LANES, ATT_TILE), BF16),
            pltpu.VMEM((2, ATT_TILE, ATT_TILE), F32),
            pltpu.VMEM((2, ATT_TILE, ATT_TILE), BF16),
            pltpu.VMEM((2, 1, ATT_TILE), F32),
            pltpu.VMEM((2, 1, ATT_TILE), F32),
            pltpu.VMEM((2, 1, ATT_TILE), F32),
            pltpu.VMEM((LANES, ATT_TILE), F32)]


def _attn_specs(seq):
    nq = seq // ATT_TILE
    heads = LANES // HEAD_DIM
    q_tile = pl.BlockSpec((1, LANES, ATT_TILE), lambda b, hp, i: (b * nq + i, hp, 0))
    keys = pl.BlockSpec((seq, 2 * LANES), lambda b, hp, i: (b, hp))
    values = pl.BlockSpec((nq, LANES, ATT_TILE), lambda b, hp, i: (b, hp, 0))
    gate = pl.BlockSpec((ATT_TILE, LANES), lambda b, hp, i: (b * nq + i, hp))
    qx = pl.BlockSpec((heads, LANES, ATT_TILE), lambda b, hp, i: (hp, 0, 0))
    return nq, q_tile, keys, values, gate, qx


def _top_k_rows(gate, row_f):
    sel = jnp.zeros(gate.shape, F32)
    for _ in range(MOBA_TOPK):
        top = jnp.max(gate, axis=0, keepdims=True)
        first = jnp.min(jnp.where(gate == top, row_f, 1e9), axis=0, keepdims=True)
        pick = row_f == first
        sel = jnp.where(pick, 1.0, sel)
        gate = jnp.where(pick, -jnp.inf, gate)
    return sel > 0.5


def _moba_kernel(qt_ref, k_ref, vt_ref, km_ref, g_ref, qx_ref, o_ref, qft_buf, *bufs):
    i = pl.program_id(2)
    n_blk = km_ref.shape[0]
    qt = qt_ref[0]
    km = km_ref[...]

    blk = lax.broadcasted_iota(jnp.int32, (n_blk, ATT_TILE), 0)
    own = (i * BLOCKS_PER_TILE
           + lax.broadcasted_iota(jnp.int32, (n_blk, ATT_TILE), 1) // MOBA_BLOCK)
    blk_f = blk.astype(F32)
    past = blk < own
    for h, rows in enumerate(_head_rows(qt.shape)):
        qh = jnp.where(rows, qt, 0.0)
        gate = jnp.dot(km, qh, precision=lax.Precision.HIGHEST, preferred_element_type=F32)
        keep = (past & _top_k_rows(jnp.where(past, gate, -jnp.inf), blk_f)) | (blk == own)
        bias = jnp.concatenate([jnp.zeros((SEL_LANE0, ATT_TILE), F32),
                                jnp.where(keep, 0.0, -MASK_BIAS),
                                jnp.zeros((LANES - SEL_LANE0 - n_blk, ATT_TILE), F32)], axis=0)
        qft_buf[h] = jnp.concatenate([(qh * (HEAD_DIM ** -0.5 * LOG2E)).astype(BF16),
                                      (bias + qx_ref[h]).astype(BF16)], axis=0)

    causal = (lax.broadcasted_iota(jnp.int32, (ATT_TILE, ATT_TILE), 0)
              <= lax.broadcasted_iota(jnp.int32, (ATT_TILE, ATT_TILE), 1))
    _flash_sweep(i + 1, lambda t: t, None, lambda t, s: jnp.where(causal, s, NEG),
                 k_ref, vt_ref, g_ref, o_ref, qft_buf, *bufs)


def _moba(qt, k, vt, kmean, g, qx, batch, seq):
    nq, q_tile, keys, values, gate, qx_spec = _attn_specs(seq)
    n_blk = seq // MOBA_BLOCK
    assert SEL_LANE0 + n_blk <= LANES
    return pl.pallas_call(
        _moba_kernel,
        grid=(batch, BRANCH // LANES, nq),
        in_specs=[q_tile, keys, values,
                  pl.BlockSpec((n_blk, LANES), lambda b, hp, i: (b, hp)),
                  gate, qx_spec],
        out_specs=gate,
        out_shape=jax.ShapeDtypeStruct(g.shape, F32),
        scratch_shapes=_attn_scratch(),
        compiler_params=_params(3),
        name="moba",
    )(qt, k, vt, kmean, g, qx)


def _dilated_multiplicity_table():
    idx = np.arange(ATT_TILE)
    delta = (np.arange(DIL_GROUPS_BACK + 1)[:, None, None] * ATT_TILE
             + idx[None, None, :] - idx[None, :, None])
    mult = np.zeros(delta.shape, np.float64)
    for window, dil in DIL_PATTERNS:
        mult += (delta >= 0) & (delta <= window) & (delta % dil == 0)
    return jnp.asarray(np.where(mult > 0, np.log2(np.maximum(mult, 1.0)), NEG), F32)


def _dilated_kernel(qt_ref, k_ref, vt_ref, t_ref, g_ref, qx_ref, o_ref, qft_buf, *bufs):
    i = pl.program_id(2)
    qt = qt_ref[0]
    zero = jnp.zeros_like(qt)
    for h, rows in enumerate(_head_rows(qt.shape)):
        qft_buf[h] = jnp.concatenate([jnp.where(rows, qt, zero), qx_ref[h].astype(BF16)], axis=0)

    add_table = lambda t, s: s + t_ref[t]
    _flash_sweep(jnp.minimum(i, DIL_GROUPS_BACK) + 1, lambda t: i - t, add_table, add_table,
                 k_ref, vt_ref, g_ref, o_ref, qft_buf, *bufs)


def _dilated(qt, k, vt, table, g, qx, batch, seq):
    nq, q_tile, keys, values, gate, qx_spec = _attn_specs(seq)
    return pl.pallas_call(
        _dilated_kernel,
        grid=(batch, BRANCH // LANES, nq),
        in_specs=[q_tile, keys, values,
                  pl.BlockSpec(table.shape, lambda b, hp, i: (0, 0, 0)),
                  gate, qx_spec],
        out_specs=gate,
        out_shape=jax.ShapeDtypeStruct(g.shape, F32),
        scratch_shapes=_attn_scratch(),
        compiler_params=_params(3),
        name="dilated",
    )(qt, k, vt, table, g, qx)


def _conv_kernel(z_ref, halo_ref, w_ref, b_ref, lng_ref, lnb_ref, pw_ref, pwb_ref, o_ref, u_buf):
    j = pl.program_id(1)

    def glu(z):
        return z[:, 0:BRANCH] * jax.nn.sigmoid(z[:, BRANCH:2 * BRANCH])

    z = z_ref[...]
    u_buf[0:CONV_HALO, :] = jnp.where(j > 0, glu(halo_ref[...]), 0.0)
    u_buf[CONV_HALO:, :] = glu(z)
    acc = jnp.zeros((ROW_TILE, BRANCH), F32) + b_ref[...]
    first = CONV_HALO - (CONV_WIDTH - 1)
    for tap in range(CONV_WIDTH):
        acc = acc + w_ref[tap:tap + 1, :] * u_buf[first + tap:first + tap + ROW_TILE, :]
    mu = jnp.mean(acc, axis=-1, keepdims=True)
    cen = acc - mu
    var = jnp.mean(cen * cen, axis=-1, keepdims=True)
    un = cen * lax.rsqrt(var + EPS) * lng_ref[...] + lnb_ref[...]
    y = jnp.dot(_silu(un).astype(BF16), pw_ref[...], preferred_element_type=F32) + pwb_ref[...]
    o_ref[...] = y * _silu(z[:, 2 * BRANCH:3 * BRANCH])


def _conv(zc, w, b, lng, lnb, pw, pwb, batch, seq):
    nt = seq // ROW_TILE
    per = ROW_TILE // CONV_HALO
    const = lambda bi, j: (0, 0)
    return pl.pallas_call(
        _conv_kernel,
        grid=(batch, nt),
        in_specs=[pl.BlockSpec((ROW_TILE, ZC_COLS), lambda bi, j: (bi * nt + j, 0)),
                  pl.BlockSpec((CONV_HALO, ZC_COLS),
                               lambda bi, j: (jnp.maximum((bi * nt + j) * per - 1, 0), 0)),
                  pl.BlockSpec((CONV_WIDTH, BRANCH), const),
                  pl.BlockSpec((1, BRANCH), const),
                  pl.BlockSpec((1, BRANCH), const),
                  pl.BlockSpec((1, BRANCH), const),
                  pl.BlockSpec((BRANCH, BRANCH), const),
                  pl.BlockSpec((1, BRANCH), const)],
        out_specs=pl.BlockSpec((ROW_TILE, BRANCH), lambda bi, j: (bi * nt + j, 0)),
        out_shape=jax.ShapeDtypeStruct((batch * seq, BRANCH), F32),
        scratch_shapes=[pltpu.VMEM((CONV_HALO + ROW_TILE, BRANCH), F32)],
        compiler_params=_params(2),
        name="conv",
    )(zc, zc, w, b, lng, lnb, pw, pwb)


def _gla_sum_matrices():
    c = GLA_CHUNK
    i = np.arange(c)[:, None]
    t = np.arange(c)[None, :]
    mats = [t <= i]
    for l in range(GLA_LEVELS):
        h = (c // 2) >> l
        mid = (i // (2 * h)) * (2 * h) + h
        mats.append(((i & h) != 0) & (t >= mid) & (t <= i))
    for l in range(GLA_LEVELS):
        h = (c // 2) >> l
        mid = (i // (2 * h)) * (2 * h) + h
        mats.append(((i & h) == 0) & (t > i) & (t < mid))
    return jnp.asarray(np.concatenate(mats, axis=0), BF16)


def _gla_kernel(z_ref, sums_ref, wg_ref, bg_ref, gn_ref, bd_ref, o_ref, state_ref):
    c = GLA_CHUNK
    nh = GLA_HEADS
    kw = nh * GLA_DK
    vw = nh * GLA_DV

    @pl.when(pl.program_id(1) == 0)
    def _():
        state_ref[...] = jnp.zeros_like(state_ref)

    row = lax.broadcasted_iota(jnp.int32, (c, 1), 0)
    qi = lax.broadcasted_iota(jnp.int32, (c, nh * c), 0)
    kj = lax.broadcasted_iota(jnp.int32, (c, nh * c), 1) % c
    level_mask = [(qi >> (GLA_LEVELS - l)) == (kj >> (GLA_LEVELS - l)) for l in range(GLA_LEVELS)]
    diag_mask = qi == kj
    k_head = (lax.broadcasted_iota(jnp.int32, (nh * c, kw), 0) // c
              == lax.broadcasted_iota(jnp.int32, (nh * c, kw), 1) // GLA_DK)
    v_head = (lax.broadcasted_iota(jnp.int32, (nh * c, vw), 0) // c
              == lax.broadcasted_iota(jnp.int32, (nh * c, vw), 1) // GLA_DV)
    s_head = (lax.broadcasted_iota(jnp.int32, (vw, kw), 0) // GLA_DV
              == lax.broadcasted_iota(jnp.int32, (vw, kw), 1) // GLA_DK)

    def per_head_keys(kt):
        return jnp.where(k_head, jnp.concatenate([kt] * nh, axis=0), 0.0).astype(BF16)

    def chunk(ci, _):
        rows = pl.ds(pl.multiple_of(ci * c, c), c)
        q = z_ref[rows, 0:kw] * GLA_DK ** -0.5
        k = z_ref[rows, kw:2 * kw]
        v = z_ref[rows, 2 * kw:2 * kw + vw]
        gd = z_ref[rows, 2 * kw + vw:2 * kw + 2 * vw]
        lr = z_ref[rows, 2 * kw + 2 * vw:2 * kw + 2 * vw + LR_PAD]

        g = jnp.dot(lr.astype(BF16), wg_ref[...], preferred_element_type=F32) + bg_ref[...]
        la = (jnp.minimum(g, 0.0) - jnp.log(1.0 + jnp.exp(-jnp.abs(g)))) / GLA_TAU
        a1 = la.astype(BF16)
        r1 = la - a1.astype(F32)
        a2 = r1.astype(BF16)
        a3 = (r1 - a2.astype(F32)).astype(BF16)
        parts = jnp.dot(sums_ref[...], jnp.concatenate([a1, a2, a3], axis=1),
                        preferred_element_type=F32)
        sums = parts[:, 0:kw] + parts[:, kw:2 * kw] + parts[:, 2 * kw:3 * kw]
        bc = sums[0:c]

        attn = jnp.where(diag_mask,
                         lax.dot_general(q.astype(BF16), per_head_keys(k), _NT,
                                         preferred_element_type=F32), 0.0)
        for l in range(GLA_LEVELS):
            later = (row & ((c // 2) >> l)) != 0
            since_mid = sums[(1 + l) * c:(2 + l) * c]
            until_mid = sums[(1 + GLA_LEVELS + l) * c:(2 + GLA_LEVELS + l) * c]
            qt = jnp.where(later, q * jnp.exp(since_mid), 0.0).astype(BF16)
            kt = jnp.where(later, 0.0, k * jnp.exp(until_mid))
            a = lax.dot_general(qt, per_head_keys(kt), _NT, preferred_element_type=F32)
            attn = attn + jnp.where(level_mask[l], a, 0.0)

        vb = v.astype(BF16)
        v_stack = jnp.where(v_head, jnp.concatenate([vb] * nh, axis=0), jnp.zeros((), BF16))
        o = jnp.dot(attn.astype(BF16), v_stack, preferred_element_type=F32)

        state = state_ref[...]
        o = o + lax.dot_general((q * jnp.exp(bc)).astype(BF16), state.astype(BF16), _NT,
                                preferred_element_type=F32)
        b_last = bc[c - 1:c, :]
        k_dec = (k * jnp.exp(b_last - bc)).astype(BF16)
        upd = lax.dot_general(vb, k_dec, _TN, preferred_element_type=F32)
        state_ref[...] = state * jnp.exp(b_last) + jnp.where(s_head, upd, 0.0)

        on = o * lax.rsqrt(_group_mean_sq(o, bd_ref[...]) + EPS) * gn_ref[...]
        o_ref[rows, :] = on * _silu(gd)
        return 0

    lax.fori_loop(0, ROW_TILE // c, chunk, 0)


def _gla(zd, sums, wg, bg, gn, bd, batch, seq):
    nt = seq // ROW_TILE
    const = lambda bi, j: (0, 0)
    return pl.pallas_call(
        _gla_kernel,
        grid=(batch, nt),
        in_specs=[pl.BlockSpec((ROW_TILE, ZD_COLS), lambda bi, j: (bi * nt + j, 0)),
                  pl.BlockSpec(sums.shape, const),
                  pl.BlockSpec((LR_PAD, GLA_HEADS * GLA_DK), const),
                  pl.BlockSpec((1, GLA_HEADS * GLA_DK), const),
                  pl.BlockSpec((1, BRANCH), const),
                  pl.BlockSpec((BRANCH, BRANCH), const)],
        out_specs=pl.BlockSpec((ROW_TILE, BRANCH), lambda bi, j: (bi * nt + j, 0)),
        out_shape=jax.ShapeDtypeStruct((batch * seq, BRANCH), F32),
        scratch_shapes=[pltpu.VMEM((GLA_HEADS * GLA_DV, GLA_HEADS * GLA_DK), F32)],
        compiler_params=_params(2),
        name="gla",
    )(zd, sums, wg, bg, gn, bd)


def _outproj_kernel(x_ref, ya_ref, yb_ref, yc_ref, yd_ref, w_ref, o_ref):
    acc = x_ref[...]
    for g, y_ref in enumerate((ya_ref, yb_ref, yc_ref, yd_ref)):
        acc = acc + jnp.dot(y_ref[...].astype(BF16), w_ref[g * BRANCH:(g + 1) * BRANCH, :],
                            preferred_element_type=F32)
    o_ref[...] = acc


def _outproj(x2, ya, yb, yc, yd, w):
    n = x2.shape[0]
    row = lambda i: (i, 0)
    branch = pl.BlockSpec((ROW_TILE, BRANCH), row)
    return pl.pallas_call(
        _outproj_kernel,
        grid=(n // ROW_TILE,),
        in_specs=[pl.BlockSpec((ROW_TILE, D_MODEL), row), branch, branch, branch, branch,
                  pl.BlockSpec((D_MODEL, D_MODEL), lambda i: (0, 0))],
        out_specs=pl.BlockSpec((ROW_TILE, D_MODEL), row),
        out_shape=jax.ShapeDtypeStruct((n, D_MODEL), F32),
        compiler_params=_params(1),
        name="outproj",
    )(x2, ya, yb, yc, yd, w)


def _pack_w_in(w_in):
    col = lambda j: w_in[:, j * BRANCH:(j + 1) * BRANCH]
    gla0 = 11 * BRANCH
    qkv = 2 * GLA_HEADS * GLA_DK + BRANCH
    pad = jnp.zeros((w_in.shape[0], LR_PAD - GLA_RANK), w_in.dtype)
    wn = jnp.concatenate([col(1), col(3), col(5), col(7), w_in[:, 8 * BRANCH:gla0 + qkv],
                          w_in[:, gla0 + qkv + GLA_RANK:], w_in[:, gla0 + qkv:gla0 + qkv + GLA_RANK], pad],
                         axis=1)
    wt = jnp.concatenate([col(0), col(2), col(4), col(6)], axis=1).T
    return wn.astype(BF16), wt.astype(BF16)


def _layer(x2, batch, seq, consts, norm_g, w_in, q_gain_a, k_gain_a, q_gain_b, k_gain_b, conv_w, conv_b,
           conv_ln_g, conv_ln_b, conv_pw_w, conv_pw_b, gla_gate_w, gla_gate_b, gla_norm_g, w_out):
    bd, kx, qx_moba, qx_dil, dil_table, gla_sums = consts
    heads = BRANCH // HEAD_DIM
    kgains = jnp.stack([jnp.tile(k_gain_a, heads), jnp.tile(k_gain_b, heads)])
    qgains = jnp.stack([jnp.tile(q_gain_a, heads), jnp.tile(q_gain_b, heads)])[:, :, None]
    wn, wt = _pack_w_in(w_in)
    (qat, ka, vat, ga, kma, qbt, kb, vbt, gb, zc, zd) = _inproj(
        x2, norm_g[None, :], wn, wt, kgains, qgains, bd, kx, seq)
    ya = _moba(qat, ka, vat, kma.reshape(-1, BRANCH), ga, qx_moba, batch, seq)
    yb = _dilated(qbt, kb, vbt, dil_table, gb, qx_dil, batch, seq)
    yc = _conv(zc, conv_w, conv_b[None, :], conv_ln_g[None, :], conv_ln_b[None, :],
               conv_pw_w.astype(BF16), conv_pw_b[None, :], batch, seq)
    wg = jnp.concatenate([gla_gate_w, jnp.zeros((LR_PAD - GLA_RANK, gla_gate_w.shape[1]), F32)],
                         axis=0).astype(BF16)
    yd = _gla(zd, gla_sums, wg, gla_gate_b[None, :], jnp.tile(gla_norm_g, GLA_HEADS)[None, :], bd,
              batch, seq)
    return _outproj(x2, ya, yb, yc, yd, w_out.astype(BF16))


def kernel(x, norm_g, w_in, q_gain_a, k_gain_a, q_gain_b, k_gain_b, conv_w, conv_b, conv_ln_g, conv_ln_b,
           conv_pw_w, conv_pw_b, gla_gate_w, gla_gate_b, gla_norm_g, w_out):
    batch, seq, d = x.shape
    assert d == D_MODEL and seq % ROW_TILE == 0 and ROW_TILE == ATT_TILE
    group = np.arange(BRANCH) // HEAD_DIM
    bd = jnp.asarray((group[:, None] == group[None, :]) / HEAD_DIM, BF16)
    heads = np.arange(BRANCH // HEAD_DIM)
    consts = (bd, _key_position_lanes(seq),
              _query_alibi_rows(2.0 ** -(1.0 + 2 * heads)),
              _query_alibi_rows(2.0 ** -(2.0 + 2 * heads)),
              _dilated_multiplicity_table(), _gla_sum_matrices())
    x2 = x.reshape(batch * seq, d)
    params = (norm_g, w_in, q_gain_a, k_gain_a, q_gain_b, k_gain_b, conv_w, conv_b, conv_ln_g,
              conv_ln_b, conv_pw_w, conv_pw_b, gla_gate_w, gla_gate_b, gla_norm_g, w_out)
    for layer in range(norm_g.shape[0]):
        x2 = _layer(x2, batch, seq, consts, *(p[layer] for p in params))
    return x2.reshape(batch, seq, d)
```

```python
import numpy as np
import jax
import jax.numpy as jnp
from jax import lax
from jax.experimental import pallas as pl
from jax.experimental.pallas import tpu as pltpu

F32 = jnp.float32
BF16 = jnp.bfloat16

D_MODEL = 1024
BRANCH = 256
HEAD_DIM = 64
MOBA_BLOCK = 256
MOBA_TOPK = 3
DIL_PATTERNS = ((128, 1), (512, 4), (2048, 16))
CONV_WIDTH = 31
GLA_HEADS = 4
GLA_DK = 32
GLA_DV = 64
GLA_RANK = 16
GLA_TAU = 16.0
EPS = 1e-6
NEG = -1e30
LOG2E = 1.4426950408889634

LANES = 128
SUBLANES = 8
ROW_TILE = 512
ATT_TILE = 512
BLOCKS_PER_TILE = ATT_TILE // MOBA_BLOCK
DIL_GROUPS_BACK = max(w for w, _ in DIL_PATTERNS) // ATT_TILE
ALIBI_PIECES = 4
SEL_LANE0 = 16
MASK_BIAS = 2.0 ** 100
M_INIT = -1e29
GLA_CHUNK = 128
GLA_LEVELS = 7
CONV_HALO = 32
LR_PAD = 128
VMEM_LIMIT = 56 * 1024 * 1024

WT_ROWS = 4 * BRANCH
ZC_COLS = 3 * BRANCH
ZD_COLS = 2 * GLA_HEADS * GLA_DK + 2 * BRANCH + LR_PAD
WN_COLS = 4 * BRANCH + ZC_COLS + ZD_COLS

_NT = (((1,), (1,)), ((), ()))
_TN = (((0,), (0,)), ((), ()))


def _params(n_grid):
    return pltpu.CompilerParams(dimension_semantics=("arbitrary",) * n_grid,
                                vmem_limit_bytes=VMEM_LIMIT)


def _silu(x):
    return x * jax.nn.sigmoid(x)


def _group_mean_sq(z, bd):
    z2 = z * z
    hi = z2.astype(BF16)
    lo = (z2 - hi.astype(F32)).astype(BF16)
    return (jnp.dot(hi, bd, preferred_element_type=F32)
            + jnp.dot(lo, bd, preferred_element_type=F32))


def _inproj_kernel(x_ref, ng_ref, wn_ref, wt_ref, kgain_ref, qgain_ref, bd_ref, kx_ref,
                   qat_ref, ka_ref, vat_ref, ga_ref, kma_ref,
                   qbt_ref, kb_ref, vbt_ref, gb_ref, zc_ref, zd_ref):
    x = x_ref[...]
    ms = jnp.mean(x * x, axis=-1, keepdims=True)
    h = (x * lax.rsqrt(ms + EPS) * ng_ref[...]).astype(BF16)
    bd = bd_ref[...]
    kx = kx_ref[...]

    def proj(c0, width):
        return jnp.dot(h, wn_ref[:, c0:c0 + width], preferred_element_type=F32)

    def proj_t(r0):
        return lax.dot_general(wt_ref[r0:r0 + BRANCH, :], h, _NT, preferred_element_type=F32)

    def head_norm(z, row):
        return z * lax.rsqrt(_group_mean_sq(z, bd) + EPS) * kgain_ref[row:row + 1, :]

    def head_norm_t(zt, idx):
        parts = []
        for g in range(BRANCH // HEAD_DIM):
            part = zt[g * HEAD_DIM:(g + 1) * HEAD_DIM]
            parts.append(part * lax.rsqrt(jnp.mean(part * part, axis=0, keepdims=True) + EPS))
        return jnp.concatenate(parts, axis=0) * qgain_ref[idx]

    def store_keys(ref, kn):
        for hp in range(BRANCH // LANES):
            ref[:, 2 * hp * LANES:(2 * hp + 1) * LANES] = kn[:, hp * LANES:(hp + 1) * LANES].astype(BF16)
            ref[:, (2 * hp + 1) * LANES:(2 * hp + 2) * LANES] = kx

    qat_ref[0] = head_norm_t(proj_t(0), 0)
    ka = head_norm(proj(0, BRANCH), 0)
    store_keys(ka_ref, ka)
    for blk in range(ROW_TILE // MOBA_BLOCK):
        kma_ref[0, blk:blk + 1, :] = jnp.mean(
            ka[blk * MOBA_BLOCK:(blk + 1) * MOBA_BLOCK], axis=0, keepdims=True)
    vat_ref[0] = proj_t(BRANCH).astype(BF16)
    ga_ref[...] = proj(BRANCH, BRANCH)

    qbt_ref[0] = (head_norm_t(proj_t(2 * BRANCH), 1) * (HEAD_DIM ** -0.5 * LOG2E)).astype(BF16)
    store_keys(kb_ref, head_norm(proj(2 * BRANCH, BRANCH), 1))
    vbt_ref[0] = proj_t(3 * BRANCH).astype(BF16)
    gb_ref[...] = proj(3 * BRANCH, BRANCH)

    zc_ref[...] = proj(4 * BRANCH, ZC_COLS)
    zd_ref[...] = proj(4 * BRANCH + ZC_COLS, ZD_COLS)


def _inproj(x2, ng, wn, wt, kgains, qgains, bd, kx, seq):
    n = x2.shape[0]
    nt = n // ROW_TILE
    per_seq = seq // ROW_TILE
    row = lambda i: (i, 0)
    const = lambda i: (0, 0)

    def nat(cols, dtype):
        return (jax.ShapeDtypeStruct((n, cols), dtype), pl.BlockSpec((ROW_TILE, cols), row))

    def tr(dtype):
        return (jax.ShapeDtypeStruct((nt, BRANCH, ROW_TILE), dtype),
                pl.BlockSpec((1, BRANCH, ROW_TILE), lambda i: (i, 0, 0)))

    kmean = (jax.ShapeDtypeStruct((nt, ROW_TILE // MOBA_BLOCK, BRANCH), F32),
             pl.BlockSpec((1, ROW_TILE // MOBA_BLOCK, BRANCH), lambda i: (i, 0, 0)))
    outs = [tr(F32), nat(2 * BRANCH, BF16), tr(BF16), nat(BRANCH, F32), kmean,
            tr(BF16), nat(2 * BRANCH, BF16), tr(BF16), nat(BRANCH, F32),
            nat(ZC_COLS, F32), nat(ZD_COLS, F32)]
    return pl.pallas_call(
        _inproj_kernel,
        grid=(nt,),
        in_specs=[pl.BlockSpec((ROW_TILE, D_MODEL), row),
                  pl.BlockSpec((1, D_MODEL), const),
                  pl.BlockSpec((D_MODEL, WN_COLS), const),
                  pl.BlockSpec((WT_ROWS, D_MODEL), const),
                  pl.BlockSpec((2, BRANCH), const),
                  pl.BlockSpec((2, BRANCH, 1), lambda i: (0, 0, 0)),
                  pl.BlockSpec((BRANCH, BRANCH), const),
                  pl.BlockSpec((ROW_TILE, LANES), lambda i: (i % per_seq, 0))],
        out_specs=[o[1] for o in outs],
        out_shape=[o[0] for o in outs],
        compiler_params=_params(1),
        name="inproj",
    )(x2, ng, wn, wt, kgains, qgains, bd, kx)


def _key_position_lanes(seq):
    pos = np.arange(seq)
    c, n = pos % MOBA_BLOCK, pos // MOBA_BLOCK
    kx = np.zeros((seq, LANES), np.float32)
    p = ALIBI_PIECES
    kx[:, 0:p] = (c // 16)[:, None]
    kx[:, p:2 * p] = (c % 16)[:, None]
    kx[:, 2 * p:3 * p] = n[:, None]
    kx[pos, SEL_LANE0 + n] = 1.0
    return jnp.asarray(kx, BF16)


def _query_alibi_rows(slopes):
    pieces, rest = [], LOG2E
    for _ in range(ALIBI_PIECES):
        piece = float(np.asarray(rest, dtype=BF16).astype(np.float64))
        pieces.append(piece)
        rest -= piece
    p = ALIBI_PIECES
    qx = np.zeros((len(slopes), LANES, ATT_TILE), np.float32)
    for h, slope in enumerate(slopes):
        for g, weight in enumerate((16.0, 1.0, float(MOBA_BLOCK))):
            qx[h, g * p:(g + 1) * p, :] = np.asarray([weight * slope * piece for piece in pieces])[:, None]
    return jnp.asarray(qx, F32)


def _flash_sweep(n_steps, group_of, loop_mask, last_mask, k_ref, vt_ref, g_ref, o_ref,
                 qft_buf, s_bufs, p_bufs, a_bufs, m_buf, l_buf, acc_buf):
    def issue_scores(t, x):
        rows = pl.ds(pl.multiple_of(group_of(t) * ATT_TILE, ATT_TILE), ATT_TILE)
        keys = k_ref[rows, :]
        for h in range(2):
            s_bufs[x][h] = jnp.dot(keys, qft_buf[h], preferred_element_type=F32)

    def softmax(x, mask):
        for h in range(2):
            s = s_bufs[x][h] if mask is None else mask(s_bufs[x][h])
            m_old = m_buf[h]
            m_new = jnp.maximum(m_old, jnp.max(s, axis=0, keepdims=True))
            p = jnp.exp2(s - m_new)
            a = jnp.exp2(m_old - m_new)
            m_buf[h] = m_new
            l_buf[h] = a * l_buf[h] + jnp.sum(p, axis=0, keepdims=True)
            a_bufs[x][h] = a
            p_bufs[x][h] = p.astype(BF16)

    def fold_values(t, x):
        vt = vt_ref[group_of(t)]
        for h in range(2):
            rows = slice(h * HEAD_DIM, (h + 1) * HEAD_DIM)
            acc_buf[rows, :] = (a_bufs[x][h] * acc_buf[rows, :]
                                + jnp.dot(vt[rows, :], p_bufs[x][h], preferred_element_type=F32))

    def regular_step(t, x):
        issue_scores(t + 1, 1 - x)
        fold_values(jnp.maximum(t - 1, 0), 1 - x)
        softmax(x, None if loop_mask is None else (lambda s: loop_mask(t, s)))

    m_buf[...] = jnp.full(m_buf.shape, M_INIT, F32)
    l_buf[...] = jnp.zeros(l_buf.shape, F32)
    acc_buf[...] = jnp.zeros(acc_buf.shape, F32)
    for x in range(2):
        a_bufs[x][...] = jnp.ones(a_bufs[x].shape, F32)
        p_bufs[x][...] = jnp.zeros(p_bufs[x].shape, BF16)

    n_regular = n_steps - 1
    odd = n_regular % 2

    @pl.when(odd == 1)
    def _():
        issue_scores(0, 1)
        regular_step(0, 1)

    @pl.when(odd == 0)
    def _():
        issue_scores(0, 0)

    def step_pair(u, _):
        t = odd + 2 * u
        regular_step(t, 0)
        regular_step(t + 1, 1)
        return 0

    lax.fori_loop(0, n_regular // 2, step_pair, 0)
    fold_values(jnp.maximum(n_steps - 2, 0), 1)
    softmax(0, lambda s: last_mask(n_steps - 1, s))
    fold_values(n_steps - 1, 0)
    out_t = jnp.concatenate([acc_buf[h * HEAD_DIM:(h + 1) * HEAD_DIM, :] / l_buf[h] for h in range(2)],
                            axis=0)
    o_ref[...] = out_t.T * _silu(g_ref[...])


def _head_rows(shape):
    first = lax.broadcasted_iota(jnp.int32, shape, 0) < HEAD_DIM
    return first, jnp.logical_not(first)


def _attn_scratch():
    stat = pltpu.VMEM((2, 1, ATT_TILE), F32)
    return ([pltpu.VMEM((2, 2 * LANES, ATT_TILE), BF16)]
            + [pltpu.VMEM((2, ATT_TILE, ATT_TILE), F32)] * 2
            + [pltpu.VMEM((2, ATT_TILE, ATT_TILE), BF16)] * 2
            + [stat, stat]
            + [stat, stat]
            + [pltpu.VMEM((LANES, ATT_TILE), F32)])


def _attn_specs(seq):
    nq = seq // ATT_TILE
    heads = LANES // HEAD_DIM
    q_tile = pl.BlockSpec((1, LANES, ATT_TILE), lambda b, hp, i: (b * nq + i, hp, 0))
    keys = pl.BlockSpec((seq, 2 * LANES), lambda b, hp, i: (b, hp))
    values = pl.BlockSpec((nq, LANES, ATT_TILE), lambda b, hp, i: (b, hp, 0))
    gate = pl.BlockSpec((ATT_TILE, LANES), lambda b, hp, i: (b * nq + i, hp))
    qx = pl.BlockSpec((heads, LANES, ATT_TILE), lambda b, hp, i: (hp, 0, 0))
    return nq, q_tile, keys, values, gate, qx


def _top_k_rows(gate, row_f):
    sel = jnp.zeros(gate.shape, F32)
    for _ in range(MOBA_TOPK):
        top = jnp.max(gate, axis=0, keepdims=True)
        first = jnp.min(jnp.where(gate == top, row_f, 1e9), axis=0, keepdims=True)
        pick = row_f == first
        sel = jnp.where(pick, 1.0, sel)
        gate = jnp.where(pick, -jnp.inf, gate)
    return sel > 0.5


def _moba_kernel(qt_ref, k_ref, vt_ref, km_ref, g_ref, qx_ref, o_ref, qft_buf, *bufs):
    i = pl.program_id(2)
    n_blk = km_ref.shape[0]
    qt = qt_ref[0]
    km = km_ref[...]

    blk = lax.broadcasted_iota(jnp.int32, (n_blk, ATT_TILE), 0)
    own = (i * BLOCKS_PER_TILE
           + lax.broadcasted_iota(jnp.int32, (n_blk, ATT_TILE), 1) // MOBA_BLOCK)
    blk_f = blk.astype(F32)
    past = blk < own
    for h, rows in enumerate(_head_rows(qt.shape)):
        qh = jnp.where(rows, qt, 0.0)
        gate = jnp.dot(km, qh, precision=lax.Precision.HIGHEST, preferred_element_type=F32)
        keep = (past & _top_k_rows(jnp.where(past, gate, -jnp.inf), blk_f)) | (blk == own)
        bias = jnp.concatenate([jnp.zeros((SEL_LANE0, ATT_TILE), F32),
                                jnp.where(keep, 0.0, -MASK_BIAS),
                                jnp.zeros((LANES - SEL_LANE0 - n_blk, ATT_TILE), F32)], axis=0)
        qft_buf[h] = jnp.concatenate([(qh * (HEAD_DIM ** -0.5 * LOG2E)).astype(BF16),
                                      (bias + qx_ref[h]).astype(BF16)], axis=0)

    causal = (lax.broadcasted_iota(jnp.int32, (ATT_TILE, ATT_TILE), 0)
              <= lax.broadcasted_iota(jnp.int32, (ATT_TILE, ATT_TILE), 1))
    _flash_sweep(i + 1, lambda t: t, None, lambda t, s: jnp.where(causal, s, NEG),
                 k_ref, vt_ref, g_ref, o_ref, qft_buf, bufs[0:2], bufs[2:4], bufs[4:6], *bufs[6:])


def _moba(qt, k, vt, kmean, g, qx, batch, seq):
    nq, q_tile, keys, values, gate, qx_spec = _attn_specs(seq)
    n_blk = seq // MOBA_BLOCK
    assert SEL_LANE0 + n_blk <= LANES
    return pl.pallas_call(
        _moba_kernel,
        grid=(batch, BRANCH // LANES, nq),
        in_specs=[q_tile, keys, values,
                  pl.BlockSpec((n_blk, LANES), lambda b, hp, i: (b, hp)),
                  gate, qx_spec],
        out_specs=gate,
        out_shape=jax.ShapeDtypeStruct(g.shape, F32),
        scratch_shapes=_attn_scratch(),
        compiler_params=_params(3),
        name="moba",
    )(qt, k, vt, kmean, g, qx)


def _dilated_multiplicity_table():
    idx = np.arange(ATT_TILE)
    delta = (np.arange(DIL_GROUPS_BACK + 1)[:, None, None] * ATT_TILE
             + idx[None, None, :] - idx[None, :, None])
    mult = np.zeros(delta.shape, np.float64)
    for window, dil in DIL_PATTERNS:
        mult += (delta >= 0) & (delta <= window) & (delta % dil == 0)
    return jnp.asarray(np.where(mult > 0, np.log2(np.maximum(mult, 1.0)), NEG), F32)


def _dilated_kernel(qt_ref, k_ref, vt_ref, t_ref, g_ref, qx_ref, o_ref, qft_buf, *bufs):
    i = pl.program_id(2)
    qt = qt_ref[0]
    zero = jnp.zeros_like(qt)
    for h, rows in enumerate(_head_rows(qt.shape)):
        qft_buf[h] = jnp.concatenate([jnp.where(rows, qt, zero), qx_ref[h].astype(BF16)], axis=0)

    add_table = lambda t, s: s + t_ref[t]
    _flash_sweep(jnp.minimum(i, DIL_GROUPS_BACK) + 1, lambda t: i - t, add_table, add_table,
                 k_ref, vt_ref, g_ref, o_ref, qft_buf, bufs[0:2], bufs[2:4], bufs[4:6], *bufs[6:])


def _dilated(qt, k, vt, table, g, qx, batch, seq):
    nq, q_tile, keys, values, gate, qx_spec = _attn_specs(seq)
    return pl.pallas_call(
        _dilated_kernel,
        grid=(batch, BRANCH // LANES, nq),
        in_specs=[q_tile, keys, values,
                  pl.BlockSpec(table.shape, lambda b, hp, i: (0, 0, 0)),
                  gate, qx_spec],
        out_specs=gate,
        out_shape=jax.ShapeDtypeStruct(g.shape, F32),
        scratch_shapes=_attn_scratch(),
        compiler_params=_params(3),
        name="dilated",
    )(qt, k, vt, table, g, qx)


def _conv_kernel(z_ref, halo_ref, w_ref, b_ref, lng_ref, lnb_ref, pw_ref, pwb_ref, o_ref, u_buf):
    j = pl.program_id(1)

    def glu(z):
        return z[:, 0:BRANCH] * jax.nn.sigmoid(z[:, BRANCH:2 * BRANCH])

    z = z_ref[...]
    u_buf[0, 0:CONV_HALO, :] = jnp.where(j > 0, glu(halo_ref[...]), 0.0)
    u_buf[0, CONV_HALO:, :] = glu(z)
    shifted = CONV_HALO + ROW_TILE - SUBLANES
    for phase in range(1, SUBLANES):
        u_buf[phase, 0:shifted, :] = u_buf[0, phase:phase + shifted, :]
    acc = jnp.zeros((ROW_TILE, BRANCH), F32) + b_ref[...]
    first = CONV_HALO - (CONV_WIDTH - 1)
    for tap in range(CONV_WIDTH):
        phase, start = (first + tap) % SUBLANES, (first + tap) // SUBLANES * SUBLANES
        acc = acc + w_ref[tap:tap + 1, :] * u_buf[phase, start:start + ROW_TILE, :]
    mu = jnp.mean(acc, axis=-1, keepdims=True)
    cen = acc - mu
    var = jnp.mean(cen * cen, axis=-1, keepdims=True)
    un = cen * lax.rsqrt(var + EPS) * lng_ref[...] + lnb_ref[...]
    y = jnp.dot(_silu(un).astype(BF16), pw_ref[...], preferred_element_type=F32) + pwb_ref[...]
    o_ref[...] = y * _silu(z[:, 2 * BRANCH:3 * BRANCH])


def _conv(zc, w, b, lng, lnb, pw, pwb, batch, seq):
    nt = seq // ROW_TILE
    per = ROW_TILE // CONV_HALO
    const = lambda bi, j: (0, 0)
    return pl.pallas_call(
        _conv_kernel,
        grid=(batch, nt),
        in_specs=[pl.BlockSpec((ROW_TILE, ZC_COLS), lambda bi, j: (bi * nt + j, 0)),
                  pl.BlockSpec((CONV_HALO, ZC_COLS),
                               lambda bi, j: (jnp.maximum((bi * nt + j) * per - 1, 0), 0)),
                  pl.BlockSpec((CONV_WIDTH, BRANCH), const),
                  pl.BlockSpec((1, BRANCH), const),
                  pl.BlockSpec((1, BRANCH), const),
                  pl.BlockSpec((1, BRANCH), const),
                  pl.BlockSpec((BRANCH, BRANCH), const),
                  pl.BlockSpec((1, BRANCH), const)],
        out_specs=pl.BlockSpec((ROW_TILE, BRANCH), lambda bi, j: (bi * nt + j, 0)),
        out_shape=jax.ShapeDtypeStruct((batch * seq, BRANCH), F32),
        scratch_shapes=[pltpu.VMEM((SUBLANES, CONV_HALO + ROW_TILE, BRANCH), F32)],
        compiler_params=_params(2),
        name="conv",
    )(zc, zc, w, b, lng, lnb, pw, pwb)


def _gla_sum_matrices():
    c = GLA_CHUNK
    i = np.arange(c)[:, None]
    t = np.arange(c)[None, :]
    mats = [t <= i]
    for l in range(GLA_LEVELS):
        h = (c // 2) >> l
        mid = (i // (2 * h)) * (2 * h) + h
        later = (i & h) != 0
        mats.append((later & (t >= mid) & (t <= i)) | (~later & (t > i) & (t < mid)))
    return jnp.asarray(np.concatenate(mats, axis=0), BF16)


def _gla_kernel(z_ref, sums_ref, wg_ref, bg_ref, gn_ref, bd_ref, o_ref, state_ref):
    c = GLA_CHUNK
    nh = GLA_HEADS
    kw = nh * GLA_DK
    vw = nh * GLA_DV

    @pl.when(pl.program_id(0) == 0)
    def _():
        state_ref[...] = jnp.zeros_like(state_ref)

    row = lax.broadcasted_iota(jnp.int32, (c, kw), 0)
    qi = lax.broadcasted_iota(jnp.int32, (c, nh * c), 0)
    kj = lax.broadcasted_iota(jnp.int32, (c, nh * c), 1) % c
    level_mask = [(qi >> (GLA_LEVELS - l)) == (kj >> (GLA_LEVELS - l)) for l in range(GLA_LEVELS)]
    diag_mask = qi == kj
    k_head = (lax.broadcasted_iota(jnp.int32, (nh * c, kw), 0) // c
              == lax.broadcasted_iota(jnp.int32, (nh * c, kw), 1) // GLA_DK)
    v_head = (lax.broadcasted_iota(jnp.int32, (nh * c, vw), 0) // c
              == lax.broadcasted_iota(jnp.int32, (nh * c, vw), 1) // GLA_DV)
    s_head = (lax.broadcasted_iota(jnp.int32, (vw, kw), 0) // GLA_DV
              == lax.broadcasted_iota(jnp.int32, (vw, kw), 1) // GLA_DK)

    def per_head_keys(kt):
        return jnp.where(k_head, jnp.concatenate([kt] * nh, axis=0), 0.0).astype(BF16)

    def chunk_of(b, rows):
        q = z_ref[b, rows, 0:kw] * GLA_DK ** -0.5
        k = z_ref[b, rows, kw:2 * kw]
        v = z_ref[b, rows, 2 * kw:2 * kw + vw]
        gd = z_ref[b, rows, 2 * kw + vw:2 * kw + 2 * vw]
        lr = z_ref[b, rows, 2 * kw + 2 * vw:2 * kw + 2 * vw + LR_PAD]

        g = jnp.dot(lr.astype(BF16), wg_ref[...], preferred_element_type=F32) + bg_ref[...]
        la = (jnp.minimum(g, 0.0) - jnp.log(1.0 + jnp.exp(-jnp.abs(g)))) / GLA_TAU
        a1 = la.astype(BF16)
        a2 = (la - a1.astype(F32)).astype(BF16)
        parts = jnp.dot(sums_ref[...], jnp.concatenate([a1, a2], axis=1),
                        preferred_element_type=F32)
        sums = parts[:, 0:kw] + parts[:, kw:2 * kw]
        bc = sums[0:c]

        attn = jnp.where(diag_mask,
                         lax.dot_general(q.astype(BF16), per_head_keys(k), _NT,
                                         preferred_element_type=F32), 0.0)
        for l in range(GLA_LEVELS):
            later = (row & ((c // 2) >> l)) != 0
            scaled = jnp.where(later, q, k) * jnp.exp(sums[(1 + l) * c:(2 + l) * c])
            qt = jnp.where(later, scaled, 0.0).astype(BF16)
            a = lax.dot_general(qt, per_head_keys(jnp.where(later, 0.0, scaled)), _NT,
                                preferred_element_type=F32)
            attn = attn + jnp.where(level_mask[l], a, 0.0)

        vb = v.astype(BF16)
        v_stack = jnp.where(v_head, jnp.concatenate([vb] * nh, axis=0), jnp.zeros((), BF16))
        o = jnp.dot(attn.astype(BF16), v_stack, preferred_element_type=F32)

        state = state_ref[b]
        o = o + lax.dot_general((q * jnp.exp(bc)).astype(BF16), state.astype(BF16), _NT,
                                preferred_element_type=F32)
        b_last = bc[c - 1:c, :]
        k_dec = (k * jnp.exp(b_last - bc)).astype(BF16)
        upd = lax.dot_general(vb, k_dec, _TN, preferred_element_type=F32)
        state_ref[b] = state * jnp.exp(b_last) + jnp.where(s_head, upd, 0.0)

        on = o * lax.rsqrt(_group_mean_sq(o, bd_ref[...]) + EPS) * gn_ref[...]
        o_ref[b, rows, :] = on * _silu(gd)

    def chunk(ci, _):
        rows = pl.ds(pl.multiple_of(ci * c, c), c)
        for b in range(z_ref.shape[0]):
            chunk_of(b, rows)
        return 0

    lax.fori_loop(0, ROW_TILE // c, chunk, 0)


def _gla(zd, sums, wg, bg, gn, bd, batch, seq):
    const = lambda j: (0, 0)
    out = pl.pallas_call(
        _gla_kernel,
        grid=(seq // ROW_TILE,),
        in_specs=[pl.BlockSpec((batch, ROW_TILE, ZD_COLS), lambda j: (0, j, 0)),
                  pl.BlockSpec(sums.shape, const),
                  pl.BlockSpec((LR_PAD, GLA_HEADS * GLA_DK), const),
                  pl.BlockSpec((1, GLA_HEADS * GLA_DK), const),
                  pl.BlockSpec((1, BRANCH), const),
                  pl.BlockSpec((BRANCH, BRANCH), const)],
        out_specs=pl.BlockSpec((batch, ROW_TILE, BRANCH), lambda j: (0, j, 0)),
        out_shape=jax.ShapeDtypeStruct((batch, seq, BRANCH), F32),
        scratch_shapes=[pltpu.VMEM((batch, GLA_HEADS * GLA_DV, GLA_HEADS * GLA_DK), F32)],
        compiler_params=_params(1),
        name="gla",
    )(zd.reshape(batch, seq, ZD_COLS), sums, wg, bg, gn, bd)
    return out.reshape(batch * seq, BRANCH)


def _outproj_kernel(x_ref, ya_ref, yb_ref, yc_ref, yd_ref, w_ref, o_ref):
    acc = x_ref[...]
    for g, y_ref in enumerate((ya_ref, yb_ref, yc_ref, yd_ref)):
        acc = acc + jnp.dot(y_ref[...].astype(BF16), w_ref[g * BRANCH:(g + 1) * BRANCH, :],
                            preferred_element_type=F32)
    o_ref[...] = acc


def _outproj(x2, ya, yb, yc, yd, w):
    n = x2.shape[0]
    row = lambda i: (i, 0)
    branch = pl.BlockSpec((ROW_TILE, BRANCH), row)
    return pl.pallas_call(
        _outproj_kernel,
        grid=(n // ROW_TILE,),
        in_specs=[pl.BlockSpec((ROW_TILE, D_MODEL), row), branch, branch, branch, branch,
                  pl.BlockSpec((D_MODEL, D_MODEL), lambda i: (0, 0))],
        out_specs=pl.BlockSpec((ROW_TILE, D_MODEL), row),
        out_shape=jax.ShapeDtypeStruct((n, D_MODEL), F32),
        compiler_params=_params(1),
        name="outproj",
    )(x2, ya, yb, yc, yd, w)


def _pack_w_in(w_in):
    col = lambda j: w_in[:, j * BRANCH:(j + 1) * BRANCH]
    gla0 = 11 * BRANCH
    qkv = 2 * GLA_HEADS * GLA_DK + BRANCH
    pad = jnp.zeros((w_in.shape[0], LR_PAD - GLA_RANK), w_in.dtype)
    wn = jnp.concatenate([col(1), col(3), col(5), col(7), w_in[:, 8 * BRANCH:gla0 + qkv],
                          w_in[:, gla0 + qkv + GLA_RANK:], w_in[:, gla0 + qkv:gla0 + qkv + GLA_RANK], pad],
                         axis=1)
    wt = jnp.concatenate([col(0), col(2), col(4), col(6)], axis=1).T
    return wn.astype(BF16), wt.astype(BF16)


def _layer(x2, batch, seq, consts, norm_g, w_in, q_gain_a, k_gain_a, q_gain_b, k_gain_b, conv_w, conv_b,
           conv_ln_g, conv_ln_b, conv_pw_w, conv_pw_b, gla_gate_w, gla_gate_b, gla_norm_g, w_out):
    bd, kx, qx_moba, qx_dil, dil_table, gla_sums = consts
    heads = BRANCH // HEAD_DIM
    kgains = jnp.stack([jnp.tile(k_gain_a, heads), jnp.tile(k_gain_b, heads)])
    qgains = jnp.stack([jnp.tile(q_gain_a, heads), jnp.tile(q_gain_b, heads)])[:, :, None]
    wn, wt = _pack_w_in(w_in)
    (qat, ka, vat, ga, kma, qbt, kb, vbt, gb, zc, zd) = _inproj(
        x2, norm_g[None, :], wn, wt, kgains, qgains, bd, kx, seq)
    ya = _moba(qat, ka, vat, kma.reshape(-1, BRANCH), ga, qx_moba, batch, seq)
    yb = _dilated(qbt, kb, vbt, dil_table, gb, qx_dil, batch, seq)
    yc = _conv(zc, conv_w, conv_b[None, :], conv_ln_g[None, :], conv_ln_b[None, :],
               conv_pw_w.astype(BF16), conv_pw_b[None, :], batch, seq)
    wg = jnp.concatenate([gla_gate_w, jnp.zeros((LR_PAD - GLA_RANK, gla_gate_w.shape[1]), F32)],
                         axis=0).astype(BF16)
    yd = _gla(zd, gla_sums, wg, gla_gate_b[None, :], jnp.tile(gla_norm_g, GLA_HEADS)[None, :], bd,
              batch, seq)
    return _outproj(x2, ya, yb, yc, yd, w_out.astype(BF16))


def kernel(x, norm_g, w_in, q_gain_a, k_gain_a, q_gain_b, k_gain_b, conv_w, conv_b, conv_ln_g, conv_ln_b,
           conv_pw_w, conv_pw_b, gla_gate_w, gla_gate_b, gla_norm_g, w_out):
    batch, seq, d = x.shape
    assert d == D_MODEL and seq % ROW_TILE == 0 and ROW_TILE == ATT_TILE
    group = np.arange(BRANCH) // HEAD_DIM
    bd = jnp.asarray((group[:, None] == group[None, :]) / HEAD_DIM, BF16)
    heads = np.arange(BRANCH // HEAD_DIM)
    consts = (bd, _key_position_lanes(seq),
              _query_alibi_rows(2.0 ** -(1.0 + 2 * heads)),
              _query_alibi_rows(2.0 ** -(2.0 + 2 * heads)),
              _dilated_multiplicity_table(), _gla_sum_matrices())
    x2 = x.reshape(batch * seq, d)
    params = (norm_g, w_in, q_gain_a, k_gain_a, q_gain_b, k_gain_b, conv_w, conv_b, conv_ln_g,
              conv_ln_b, conv_pw_w, conv_pw_b, gla_gate_w, gla_gate_b, gla_norm_g, w_out)
    for layer in range(norm_g.shape[0]):
        x2 = _layer(x2, batch, seq, consts, *(p[layer] for p in params))
    return x2.reshape(batch, seq, d)
```

```python
import numpy as np
import jax
import jax.numpy as jnp
from jax import lax
from jax.experimental import pallas as pl
from jax.experimental.pallas import tpu as pltpu

F32 = jnp.float32
BF16 = jnp.bfloat16

D_MODEL = 1024
BRANCH = 256
HEAD_DIM = 64
MOBA_BLOCK = 256
MOBA_TOPK = 3
DIL_PATTERNS = ((128, 1), (512, 4), (2048, 16))
CONV_WIDTH = 31
GLA_HEADS = 4
GLA_DK = 32
GLA_DV = 64
GLA_RANK = 16
GLA_TAU = 16.0
EPS = 1e-6
NEG = -1e30
LOG2E = 1.4426950408889634

LANES = 128
SUBLANES = 8
ROW_TILE = 512
ATT_TILE = 512
BLOCKS_PER_TILE = ATT_TILE // MOBA_BLOCK
DIL_GROUPS_BACK = max(w for w, _ in DIL_PATTERNS) // ATT_TILE
ALIBI_PIECES = 4
SEL_LANE0 = 16
MASK_BIAS = 2.0 ** 100
M_INIT = -1e29
GLA_CHUNK = 128
GLA_LEVELS = 7
CONV_HALO = 32
LR_PAD = 128
VMEM_LIMIT = 56 * 1024 * 1024

WT_ROWS = 4 * BRANCH
ZC_COLS = 3 * BRANCH
ZD_COLS = 2 * GLA_HEADS * GLA_DK + 2 * BRANCH + LR_PAD
WN_COLS = 4 * BRANCH + ZC_COLS + ZD_COLS

_NT = (((1,), (1,)), ((), ()))
_TN = (((0,), (0,)), ((), ()))


def _params(n_grid):
    return pltpu.CompilerParams(dimension_semantics=("arbitrary",) * n_grid,
                                vmem_limit_bytes=VMEM_LIMIT)


def _silu(x):
    return x * jax.nn.sigmoid(x)


def _group_mean_sq(z, bd):
    z2 = z * z
    hi = z2.astype(BF16)
    lo = (z2 - hi.astype(F32)).astype(BF16)
    return (jnp.dot(hi, bd, preferred_element_type=F32)
            + jnp.dot(lo, bd, preferred_element_type=F32))


def _inproj_kernel(x_ref, ng_ref, wn_ref, wt_ref, kgain_ref, qgain_ref, bd_ref, kx_ref,
                   qat_ref, ka_ref, vat_ref, ga_ref, kma_ref,
                   qbt_ref, kb_ref, vbt_ref, gb_ref, zc_ref, zd_ref):
    x = x_ref[...]
    ms = jnp.mean(x * x, axis=-1, keepdims=True)
    h = (x * lax.rsqrt(ms + EPS) * ng_ref[...]).astype(BF16)
    bd = bd_ref[...]
    kx = kx_ref[...]

    def proj(c0, width):
        return jnp.dot(h, wn_ref[:, c0:c0 + width], preferred_element_type=F32)

    def proj_t(r0):
        return lax.dot_general(wt_ref[r0:r0 + BRANCH, :], h, _NT, preferred_element_type=F32)

    def head_norm(z, row):
        return z * lax.rsqrt(_group_mean_sq(z, bd) + EPS) * kgain_ref[row:row + 1, :]

    def head_norm_t(zt, idx):
        parts = []
        for g in range(BRANCH // HEAD_DIM):
            part = zt[g * HEAD_DIM:(g + 1) * HEAD_DIM]
            parts.append(part * lax.rsqrt(jnp.mean(part * part, axis=0, keepdims=True) + EPS))
        return jnp.concatenate(parts, axis=0) * qgain_ref[idx]

    def store_keys(ref, kn):
        for hp in range(BRANCH // LANES):
            ref[:, 2 * hp * LANES:(2 * hp + 1) * LANES] = kn[:, hp * LANES:(hp + 1) * LANES].astype(BF16)
            ref[:, (2 * hp + 1) * LANES:(2 * hp + 2) * LANES] = kx

    qat_ref[0] = head_norm_t(proj_t(0), 0)
    ka = head_norm(proj(0, BRANCH), 0)
    store_keys(ka_ref, ka)
    for blk in range(ROW_TILE // MOBA_BLOCK):
        kma_ref[0, blk:blk + 1, :] = jnp.mean(
            ka[blk * MOBA_BLOCK:(blk + 1) * MOBA_BLOCK], axis=0, keepdims=True)
    vat_ref[0] = proj_t(BRANCH).astype(BF16)
    ga_ref[...] = proj(BRANCH, BRANCH)

    qbt_ref[0] = (head_norm_t(proj_t(2 * BRANCH), 1) * (HEAD_DIM ** -0.5 * LOG2E)).astype(BF16)
    store_keys(kb_ref, head_norm(proj(2 * BRANCH, BRANCH), 1))
    vbt_ref[0] = proj_t(3 * BRANCH).astype(BF16)
    gb_ref[...] = proj(3 * BRANCH, BRANCH)

    zc_ref[...] = proj(4 * BRANCH, ZC_COLS)
    zd_ref[...] = proj(4 * BRANCH + ZC_COLS, ZD_COLS)


def _inproj(x2, ng, wn, wt, kgains, qgains, bd, kx, seq):
    n = x2.shape[0]
    nt = n // ROW_TILE
    per_seq = seq // ROW_TILE
    row = lambda i: (i, 0)
    const = lambda i: (0, 0)

    def nat(cols, dtype):
        return (jax.ShapeDtypeStruct((n, cols), dtype), pl.BlockSpec((ROW_TILE, cols), row))

    def tr(dtype):
        return (jax.ShapeDtypeStruct((nt, BRANCH, ROW_TILE), dtype),
                pl.BlockSpec((1, BRANCH, ROW_TILE), lambda i: (i, 0, 0)))

    kmean = (jax.ShapeDtypeStruct((nt, ROW_TILE // MOBA_BLOCK, BRANCH), F32),
             pl.BlockSpec((1, ROW_TILE // MOBA_BLOCK, BRANCH), lambda i: (i, 0, 0)))
    outs = [tr(F32), nat(2 * BRANCH, BF16), tr(BF16), nat(BRANCH, F32), kmean,
            tr(BF16), nat(2 * BRANCH, BF16), tr(BF16), nat(BRANCH, F32),
            nat(ZC_COLS, F32), nat(ZD_COLS, F32)]
    return pl.pallas_call(
        _inproj_kernel,
        grid=(nt,),
        in_specs=[pl.BlockSpec((ROW_TILE, D_MODEL), row),
                  pl.BlockSpec((1, D_MODEL), const),
                  pl.BlockSpec((D_MODEL, WN_COLS), const),
                  pl.BlockSpec((WT_ROWS, D_MODEL), const),
                  pl.BlockSpec((2, BRANCH), const),
                  pl.BlockSpec((2, BRANCH, 1), lambda i: (0, 0, 0)),
                  pl.BlockSpec((BRANCH, BRANCH), const),
                  pl.BlockSpec((ROW_TILE, LANES), lambda i: (i % per_seq, 0))],
        out_specs=[o[1] for o in outs],
        out_shape=[o[0] for o in outs],
        compiler_params=_params(1),
        name="inproj",
    )(x2, ng, wn, wt, kgains, qgains, bd, kx)


def _key_position_lanes(seq):
    pos = np.arange(seq)
    c, n = pos % MOBA_BLOCK, pos // MOBA_BLOCK
    kx = np.zeros((seq, LANES), np.float32)
    p = ALIBI_PIECES
    kx[:, 0:p] = (c // 16)[:, None]
    kx[:, p:2 * p] = (c % 16)[:, None]
    kx[:, 2 * p:3 * p] = n[:, None]
    kx[pos, SEL_LANE0 + n] = 1.0
    return jnp.asarray(kx, BF16)


def _query_alibi_rows(slopes):
    pieces, rest = [], LOG2E
    for _ in range(ALIBI_PIECES):
        piece = float(np.asarray(rest, dtype=BF16).astype(np.float64))
        pieces.append(piece)
        rest -= piece
    p = ALIBI_PIECES
    qx = np.zeros((len(slopes), LANES, ATT_TILE), np.float32)
    for h, slope in enumerate(slopes):
        for g, weight in enumerate((16.0, 1.0, float(MOBA_BLOCK))):
            qx[h, g * p:(g + 1) * p, :] = np.asarray([weight * slope * piece for piece in pieces])[:, None]
    return jnp.asarray(qx, F32)


def _flash_sweep(n_steps, group_of, score_bias, last_mask, k_ref, vt_ref, g_ref, o_ref,
                 qft_buf, s_bufs, p_bufs, a_bufs, smax_bufs, m_buf, acc_buf):
    def issue_scores(t, x):
        rows = pl.ds(pl.multiple_of(group_of(t) * ATT_TILE, ATT_TILE), ATT_TILE)
        keys = k_ref[rows, :]
        for h in range(2):
            s = jnp.dot(keys, qft_buf[h], preferred_element_type=F32)
            if score_bias is not None:
                s = s + score_bias(t)
            s_bufs[x][h] = s
            smax_bufs[x][h] = jnp.max(s, axis=0, keepdims=True)

    def softmax(x, mask):
        for h in range(2):
            s = s_bufs[x][h]
            group_max = smax_bufs[x][h]
            if mask is not None:
                s = mask(s)
                group_max = jnp.max(s, axis=0, keepdims=True)
            m_old = m_buf[h]
            m_new = jnp.maximum(m_old, group_max)
            m_buf[h] = m_new
            a_bufs[x][h] = jnp.exp2(m_old - m_new)
            p_bufs[x][h] = jnp.exp2(s - m_new).astype(BF16)

    ones = jnp.ones((SUBLANES, ATT_TILE), BF16)

    def fold_values(t, x):
        vt = vt_ref[group_of(t)]
        for h in range(2):
            lhs = jnp.concatenate([vt[h * HEAD_DIM:(h + 1) * HEAD_DIM, :], ones], axis=0)
            acc_buf[h] = (a_bufs[x][h] * acc_buf[h]
                          + jnp.dot(lhs, p_bufs[x][h], preferred_element_type=F32))

    def regular_step(t, x):
        issue_scores(t + 1, 1 - x)
        fold_values(jnp.maximum(t - 1, 0), 1 - x)
        softmax(x, None)

    m_buf[...] = jnp.full(m_buf.shape, M_INIT, F32)
    acc_buf[...] = jnp.zeros(acc_buf.shape, F32)
    for x in range(2):
        a_bufs[x][...] = jnp.ones(a_bufs[x].shape, F32)
        p_bufs[x][...] = jnp.zeros(p_bufs[x].shape, BF16)

    n_regular = n_steps - 1
    odd = n_regular % 2

    @pl.when(odd == 1)
    def _():
        issue_scores(0, 1)
        regular_step(0, 1)

    @pl.when(odd == 0)
    def _():
        issue_scores(0, 0)

    def step_pair(u, _):
        t = odd + 2 * u
        regular_step(t, 0)
        regular_step(t + 1, 1)
        return 0

    lax.fori_loop(0, n_regular // 2, step_pair, 0)
    fold_values(jnp.maximum(n_steps - 2, 0), 1)
    softmax(0, last_mask)
    fold_values(n_steps - 1, 0)
    out_t = jnp.concatenate(
        [acc_buf[h, 0:HEAD_DIM, :] / acc_buf[h, HEAD_DIM:HEAD_DIM + 1, :] for h in range(2)], axis=0)
    o_ref[...] = out_t.T * _silu(g_ref[...])


def _head_rows(shape):
    first = lax.broadcasted_iota(jnp.int32, shape, 0) < HEAD_DIM
    return first, jnp.logical_not(first)


def _attn_scratch():
    stat = pltpu.VMEM((2, 1, ATT_TILE), F32)
    return ([pltpu.VMEM((2, 2 * LANES, ATT_TILE), BF16)]
            + [pltpu.VMEM((2, ATT_TILE, ATT_TILE), F32)] * 2
            + [pltpu.VMEM((2, ATT_TILE, ATT_TILE), BF16)] * 2
            + [stat, stat]
            + [stat, stat]
            + [stat]
            + [pltpu.VMEM((2, HEAD_DIM + SUBLANES, ATT_TILE), F32)])


def _attn_specs(seq):
    nq = seq // ATT_TILE
    heads = LANES // HEAD_DIM
    q_tile = pl.BlockSpec((1, LANES, ATT_TILE), lambda b, hp, i: (b * nq + i, hp, 0))
    keys = pl.BlockSpec((seq, 2 * LANES), lambda b, hp, i: (b, hp))
    values = pl.BlockSpec((nq, LANES, ATT_TILE), lambda b, hp, i: (b, hp, 0))
    gate = pl.BlockSpec((ATT_TILE, LANES), lambda b, hp, i: (b * nq + i, hp))
    qx = pl.BlockSpec((heads, LANES, ATT_TILE), lambda b, hp, i: (hp, 0, 0))
    return nq, q_tile, keys, values, gate, qx


def _top_k_rows(gate, row_f):
    sel = jnp.zeros(gate.shape, F32)
    for _ in range(MOBA_TOPK):
        top = jnp.max(gate, axis=0, keepdims=True)
        first = jnp.min(jnp.where(gate == top, row_f, 1e9), axis=0, keepdims=True)
        pick = row_f == first
        sel = jnp.where(pick, 1.0, sel)
        gate = jnp.where(pick, -jnp.inf, gate)
    return sel > 0.5


def _moba_kernel(qt_ref, k_ref, vt_ref, km_ref, g_ref, qx_ref, o_ref, qft_buf, *bufs):
    i = pl.program_id(2)
    n_blk = km_ref.shape[0]
    qt = qt_ref[0]
    km = km_ref[...]

    blk = lax.broadcasted_iota(jnp.int32, (n_blk, ATT_TILE), 0)
    own = (i * BLOCKS_PER_TILE
           + lax.broadcasted_iota(jnp.int32, (n_blk, ATT_TILE), 1) // MOBA_BLOCK)
    blk_f = blk.astype(F32)
    past = blk < own
    for h, rows in enumerate(_head_rows(qt.shape)):
        qh = jnp.where(rows, qt, 0.0)
        gate = jnp.dot(km, qh, precision=lax.Precision.HIGHEST, preferred_element_type=F32)
        keep = (past & _top_k_rows(jnp.where(past, gate, -jnp.inf), blk_f)) | (blk == own)
        bias = jnp.concatenate([jnp.zeros((SEL_LANE0, ATT_TILE), F32),
                                jnp.where(keep, 0.0, -MASK_BIAS),
                                jnp.zeros((LANES - SEL_LANE0 - n_blk, ATT_TILE), F32)], axis=0)
        qft_buf[h] = jnp.concatenate([(qh * (HEAD_DIM ** -0.5 * LOG2E)).astype(BF16),
                                      (bias + qx_ref[h]).astype(BF16)], axis=0)

    causal = (lax.broadcasted_iota(jnp.int32, (ATT_TILE, ATT_TILE), 0)
              <= lax.broadcasted_iota(jnp.int32, (ATT_TILE, ATT_TILE), 1))
    _flash_sweep(i + 1, lambda t: t, None, lambda s: jnp.where(causal, s, NEG),
                 k_ref, vt_ref, g_ref, o_ref, qft_buf, bufs[0:2], bufs[2:4], bufs[4:6], bufs[6:8],
                 *bufs[8:])


def _moba(qt, k, vt, kmean, g, qx, batch, seq):
    nq, q_tile, keys, values, gate, qx_spec = _attn_specs(seq)
    n_blk = seq // MOBA_BLOCK
    assert SEL_LANE0 + n_blk <= LANES
    return pl.pallas_call(
        _moba_kernel,
        grid=(batch, BRANCH // LANES, nq),
        in_specs=[q_tile, keys, values,
                  pl.BlockSpec((n_blk, LANES), lambda b, hp, i: (b, hp)),
                  gate, qx_spec],
        out_specs=gate,
        out_shape=jax.ShapeDtypeStruct(g.shape, F32),
        scratch_shapes=_attn_scratch(),
        compiler_params=_params(3),
        name="moba",
    )(qt, k, vt, kmean, g, qx)


def _dilated_multiplicity_table():
    idx = np.arange(ATT_TILE)
    delta = (np.arange(DIL_GROUPS_BACK + 1)[:, None, None] * ATT_TILE
             + idx[None, None, :] - idx[None, :, None])
    mult = np.zeros(delta.shape, np.float64)
    for window, dil in DIL_PATTERNS:
        mult += (delta >= 0) & (delta <= window) & (delta % dil == 0)
    return jnp.asarray(np.where(mult > 0, np.log2(np.maximum(mult, 1.0)), NEG), F32)


def _dilated_kernel(qt_ref, k_ref, vt_ref, t_ref, g_ref, qx_ref, o_ref, qft_buf, *bufs):
    i = pl.program_id(2)
    qt = qt_ref[0]
    zero = jnp.zeros_like(qt)
    for h, rows in enumerate(_head_rows(qt.shape)):
        qft_buf[h] = jnp.concatenate([jnp.where(rows, qt, zero), qx_ref[h].astype(BF16)], axis=0)

    _flash_sweep(jnp.minimum(i, DIL_GROUPS_BACK) + 1, lambda t: i - t, lambda t: t_ref[t], None,
                 k_ref, vt_ref, g_ref, o_ref, qft_buf, bufs[0:2], bufs[2:4], bufs[4:6], bufs[6:8],
                 *bufs[8:])


def _dilated(qt, k, vt, table, g, qx, batch, seq):
    nq, q_tile, keys, values, gate, qx_spec = _attn_specs(seq)
    return pl.pallas_call(
        _dilated_kernel,
        grid=(batch, BRANCH // LANES, nq),
        in_specs=[q_tile, keys, values,
                  pl.BlockSpec(table.shape, lambda b, hp, i: (0, 0, 0)),
                  gate, qx_spec],
        out_specs=gate,
        out_shape=jax.ShapeDtypeStruct(g.shape, F32),
        scratch_shapes=_attn_scratch(),
        compiler_params=_params(3),
        name="dilated",
    )(qt, k, vt, table, g, qx)


def _conv_kernel(z_ref, halo_ref, w_ref, b_ref, lng_ref, lnb_ref, pw_ref, pwb_ref, o_ref, u_buf):
    j = pl.program_id(1)

    def glu(z):
        return z[:, 0:BRANCH] * jax.nn.sigmoid(z[:, BRANCH:2 * BRANCH])

    z = z_ref[...]
    u_buf[0, 0:CONV_HALO, :] = jnp.where(j > 0, glu(halo_ref[...]), 0.0)
    u_buf[0, CONV_HALO:, :] = glu(z)
    shifted = CONV_HALO + ROW_TILE - SUBLANES
    for phase in range(1, SUBLANES):
        u_buf[phase, 0:shifted, :] = u_buf[0, phase:phase + shifted, :]
    acc = jnp.zeros((ROW_TILE, BRANCH), F32) + b_ref[...]
    first = CONV_HALO - (CONV_WIDTH - 1)
    for tap in range(CONV_WIDTH):
        phase, start = (first + tap) % SUBLANES, (first + tap) // SUBLANES * SUBLANES
        acc = acc + w_ref[tap:tap + 1, :] * u_buf[phase, start:start + ROW_TILE, :]
    mu = jnp.mean(acc, axis=-1, keepdims=True)
    cen = acc - mu
    var = jnp.mean(cen * cen, axis=-1, keepdims=True)
    un = cen * lax.rsqrt(var + EPS) * lng_ref[...] + lnb_ref[...]
    y = jnp.dot(_silu(un).astype(BF16), pw_ref[...], preferred_element_type=F32) + pwb_ref[...]
    o_ref[...] = y * _silu(z[:, 2 * BRANCH:3 * BRANCH])


def _conv(zc, w, b, lng, lnb, pw, pwb, batch, seq):
    nt = seq // ROW_TILE
    per = ROW_TILE // CONV_HALO
    const = lambda bi, j: (0, 0)
    return pl.pallas_call(
        _conv_kernel,
        grid=(batch, nt),
        in_specs=[pl.BlockSpec((ROW_TILE, ZC_COLS), lambda bi, j: (bi * nt + j, 0)),
                  pl.BlockSpec((CONV_HALO, ZC_COLS),
                               lambda bi, j: (jnp.maximum((bi * nt + j) * per - 1, 0), 0)),
                  pl.BlockSpec((CONV_WIDTH, BRANCH), const),
                  pl.BlockSpec((1, BRANCH), const),
                  pl.BlockSpec((1, BRANCH), const),
                  pl.BlockSpec((1, BRANCH), const),
                  pl.BlockSpec((BRANCH, BRANCH), const),
                  pl.BlockSpec((1, BRANCH), const)],
        out_specs=pl.BlockSpec((ROW_TILE, BRANCH), lambda bi, j: (bi * nt + j, 0)),
        out_shape=jax.ShapeDtypeStruct((batch * seq, BRANCH), F32),
        scratch_shapes=[pltpu.VMEM((SUBLANES, CONV_HALO + ROW_TILE, BRANCH), F32)],
        compiler_params=_params(2),
        name="conv",
    )(zc, zc, w, b, lng, lnb, pw, pwb)


def _gla_sum_matrices():
    c = GLA_CHUNK
    i = np.arange(c)[:, None]
    t = np.arange(c)[None, :]
    mats = [t <= i]
    for l in range(GLA_LEVELS):
        h = (c // 2) >> l
        mid = (i // (2 * h)) * (2 * h) + h
        later = (i & h) != 0
        mats.append((later & (t >= mid) & (t <= i)) | (~later & (t > i) & (t < mid)))
    return jnp.asarray(np.concatenate(mats, axis=0), BF16)


def _gla_kernel(z_ref, sums_ref, wg_ref, bg_ref, gn_ref, bd_ref, o_ref, state_ref):
    c = GLA_CHUNK
    nh = GLA_HEADS
    kw = nh * GLA_DK
    vw = nh * GLA_DV

    @pl.when(pl.program_id(0) == 0)
    def _():
        state_ref[...] = jnp.zeros_like(state_ref)

    row = lax.broadcasted_iota(jnp.int32, (c, kw), 0)
    qi = lax.broadcasted_iota(jnp.int32, (c, nh * c), 0)
    kj = lax.broadcasted_iota(jnp.int32, (c, nh * c), 1) % c
    level_mask = [(qi >> (GLA_LEVELS - l)) == (kj >> (GLA_LEVELS - l)) for l in range(GLA_LEVELS)]
    diag_mask = qi == kj
    k_head = (lax.broadcasted_iota(jnp.int32, (nh * c, kw), 0) // c
              == lax.broadcasted_iota(jnp.int32, (nh * c, kw), 1) // GLA_DK)
    v_head = (lax.broadcasted_iota(jnp.int32, (nh * c, vw), 0) // c
              == lax.broadcasted_iota(jnp.int32, (nh * c, vw), 1) // GLA_DV)
    s_head = (lax.broadcasted_iota(jnp.int32, (vw, kw), 0) // GLA_DV
              == lax.broadcasted_iota(jnp.int32, (vw, kw), 1) // GLA_DK)

    def per_head_keys(kt):
        return jnp.where(k_head, jnp.concatenate([kt] * nh, axis=0), 0.0).astype(BF16)

    def chunk_of(b, rows):
        q = z_ref[b, rows, 0:kw] * GLA_DK ** -0.5
        k = z_ref[b, rows, kw:2 * kw]
        v = z_ref[b, rows, 2 * kw:2 * kw + vw]
        gd = z_ref[b, rows, 2 * kw + vw:2 * kw + 2 * vw]
        lr = z_ref[b, rows, 2 * kw + 2 * vw:2 * kw + 2 * vw + LR_PAD]

        g = jnp.dot(lr.astype(BF16), wg_ref[...], preferred_element_type=F32) + bg_ref[...]
        la = (jnp.minimum(g, 0.0) - jnp.log(1.0 + jnp.exp(-jnp.abs(g)))) / GLA_TAU
        a1 = la.astype(BF16)
        a2 = (la - a1.astype(F32)).astype(BF16)
        parts = jnp.dot(sums_ref[...], jnp.concatenate([a1, a2], axis=1),
                        preferred_element_type=F32)
        sums = parts[:, 0:kw] + parts[:, kw:2 * kw]
        bc = sums[0:c]

        attn = jnp.where(diag_mask,
                         lax.dot_general(q.astype(BF16), per_head_keys(k), _NT,
                                         preferred_element_type=F32), 0.0)
        for l in range(GLA_LEVELS):
            later = (row & ((c // 2) >> l)) != 0
            scaled = jnp.where(later, q, k) * jnp.exp(sums[(1 + l) * c:(2 + l) * c])
            qt = jnp.where(later, scaled, 0.0).astype(BF16)
            a = lax.dot_general(qt, per_head_keys(jnp.where(later, 0.0, scaled)), _NT,
                                preferred_element_type=F32)
            attn = attn + jnp.where(level_mask[l], a, 0.0)

        vb = v.astype(BF16)
        v_stack = jnp.where(v_head, jnp.concatenate([vb] * nh, axis=0), jnp.zeros((), BF16))
        o = jnp.dot(attn.astype(BF16), v_stack, preferred_element_type=F32)

        state = state_ref[b]
        o = o + lax.dot_general((q * jnp.exp(bc)).astype(BF16), state.astype(BF16), _NT,
                                preferred_element_type=F32)
        b_last = bc[c - 1:c, :]
        k_dec = (k * jnp.exp(b_last - bc)).astype(BF16)
        upd = lax.dot_general(vb, k_dec, _TN, preferred_element_type=F32)
        state_ref[b] = state * jnp.exp(b_last) + jnp.where(s_head, upd, 0.0)

        on = o * lax.rsqrt(_group_mean_sq(o, bd_ref[...]) + EPS) * gn_ref[...]
        o_ref[b, rows, :] = on * _silu(gd)

    def chunk(ci, _):
        rows = pl.ds(pl.multiple_of(ci * c, c), c)
        for b in range(z_ref.shape[0]):
            chunk_of(b, rows)
        return 0

    lax.fori_loop(0, ROW_TILE // c, chunk, 0)


def _gla(zd, sums, wg, bg, gn, bd, batch, seq):
    const = lambda j: (0, 0)
    out = pl.pallas_call(
        _gla_kernel,
        grid=(seq // ROW_TILE,),
        in_specs=[pl.BlockSpec((batch, ROW_TILE, ZD_COLS), lambda j: (0, j, 0)),
                  pl.BlockSpec(sums.shape, const),
                  pl.BlockSpec((LR_PAD, GLA_HEADS * GLA_DK), const),
                  pl.BlockSpec((1, GLA_HEADS * GLA_DK), const),
                  pl.BlockSpec((1, BRANCH), const),
                  pl.BlockSpec((BRANCH, BRANCH), const)],
        out_specs=pl.BlockSpec((batch, ROW_TILE, BRANCH), lambda j: (0, j, 0)),
        out_shape=jax.ShapeDtypeStruct((batch, seq, BRANCH), F32),
        scratch_shapes=[pltpu.VMEM((batch, GLA_HEADS * GLA_DV, GLA_HEADS * GLA_DK), F32)],
        compiler_params=_params(1),
        name="gla",
    )(zd.reshape(batch, seq, ZD_COLS), sums, wg, bg, gn, bd)
    return out.reshape(batch * seq, BRANCH)


def _outproj_kernel(x_ref, ya_ref, yb_ref, yc_ref, yd_ref, w_ref, o_ref):
    acc = x_ref[...]
    for g, y_ref in enumerate((ya_ref, yb_ref, yc_ref, yd_ref)):
        acc = acc + jnp.dot(y_ref[...].astype(BF16), w_ref[g * BRANCH:(g + 1) * BRANCH, :],
                            preferred_element_type=F32)
    o_ref[...] = acc


def _outproj(x2, ya, yb, yc, yd, w):
    n = x2.shape[0]
    row = lambda i: (i, 0)
    branch = pl.BlockSpec((ROW_TILE, BRANCH), row)
    return pl.pallas_call(
        _outproj_kernel,
        grid=(n // ROW_TILE,),
        in_specs=[pl.BlockSpec((ROW_TILE, D_MODEL), row), branch, branch, branch, branch,
                  pl.BlockSpec((D_MODEL, D_MODEL), lambda i: (0, 0))],
        out_specs=pl.BlockSpec((ROW_TILE, D_MODEL), row),
        out_shape=jax.ShapeDtypeStruct((n, D_MODEL), F32),
        compiler_params=_params(1),
        name="outproj",
    )(x2, ya, yb, yc, yd, w)


def _pack_w_in(w_in):
    col = lambda j: w_in[:, j * BRANCH:(j + 1) * BRANCH]
    gla0 = 11 * BRANCH
    qkv = 2 * GLA_HEADS * GLA_DK + BRANCH
    pad = jnp.zeros((w_in.shape[0], LR_PAD - GLA_RANK), w_in.dtype)
    wn = jnp.concatenate([col(1), col(3), col(5), col(7), w_in[:, 8 * BRANCH:gla0 + qkv],
                          w_in[:, gla0 + qkv + GLA_RANK:], w_in[:, gla0 + qkv:gla0 + qkv + GLA_RANK], pad],
                         axis=1)
    wt = jnp.concatenate([col(0), col(2), col(4), col(6)], axis=1).T
    return wn.astype(BF16), wt.astype(BF16)


def _layer(x2, batch, seq, consts, norm_g, w_in, q_gain_a, k_gain_a, q_gain_b, k_gain_b, conv_w, conv_b,
           conv_ln_g, conv_ln_b, conv_pw_w, conv_pw_b, gla_gate_w, gla_gate_b, gla_norm_g, w_out):
    bd, kx, qx_moba, qx_dil, dil_table, gla_sums = consts
    heads = BRANCH // HEAD_DIM
    kgains = jnp.stack([jnp.tile(k_gain_a, heads), jnp.tile(k_gain_b, heads)])
    qgains = jnp.stack([jnp.tile(q_gain_a, heads), jnp.tile(q_gain_b, heads)])[:, :, None]
    wn, wt = _pack_w_in(w_in)
    (qat, ka, vat, ga, kma, qbt, kb, vbt, gb, zc, zd) = _inproj(
        x2, norm_g[None, :], wn, wt, kgains, qgains, bd, kx, seq)
    ya = _moba(qat, ka, vat, kma.reshape(-1, BRANCH), ga, qx_moba, batch, seq)
    yb = _dilated(qbt, kb, vbt, dil_table, gb, qx_dil, batch, seq)
    yc = _conv(zc, conv_w, conv_b[None, :], conv_ln_g[None, :], conv_ln_b[None, :],
               conv_pw_w.astype(BF16), conv_pw_b[None, :], batch, seq)
    wg = jnp.concatenate([gla_gate_w, jnp.zeros((LR_PAD - GLA_RANK, gla_gate_w.shape[1]), F32)],
                         axis=0).astype(BF16)
    yd = _gla(zd, gla_sums, wg, gla_gate_b[None, :], jnp.tile(gla_norm_g, GLA_HEADS)[None, :], bd,
              batch, seq)
    return _outproj(x2, ya, yb, yc, yd, w_out.astype(BF16))


def kernel(x, norm_g, w_in, q_gain_a, k_gain_a, q_gain_b, k_gain_b, conv_w, conv_b, conv_ln_g, conv_ln_b,
           conv_pw_w, conv_pw_b, gla_gate_w, gla_gate_b, gla_norm_g, w_out):
    batch, seq, d = x.shape
    assert d == D_MODEL and seq % ROW_TILE == 0 and ROW_TILE == ATT_TILE
    group = np.arange(BRANCH) // HEAD_DIM
    bd = jnp.asarray((group[:, None] == group[None, :]) / HEAD_DIM, BF16)
    heads = np.arange(BRANCH // HEAD_DIM)
    consts = (bd, _key_position_lanes(seq),
              _query_alibi_rows(2.0 ** -(1.0 + 2 * heads)),
              _query_alibi_rows(2.0 ** -(2.0 + 2 * heads)),
              _dilated_multiplicity_table(), _gla_sum_matrices())
    x2 = x.reshape(batch * seq, d)
    params = (norm_g, w_in, q_gain_a, k_gain_a, q_gain_b, k_gain_b, conv_w, conv_b, conv_ln_g,
              conv_ln_b, conv_pw_w, conv_pw_b, gla_gate_w, gla_gate_b, gla_norm_g, w_out)
    for layer in range(norm_g.shape[0]):
        x2 = _layer(x2, batch, seq, consts, *(p[layer] for p in params))
    return x2.reshape(batch, seq, d)
```

```python
import numpy as np
import jax
import jax.numpy as jnp
from jax import lax
from jax.experimental import pallas as pl
from jax.experimental.pallas import tpu as pltpu

F32 = jnp.float32
BF16 = jnp.bfloat16

D_MODEL = 1024
BRANCH = 256
HEAD_DIM = 64
N_HEADS = BRANCH // HEAD_DIM
MOBA_BLOCK = 256
MOBA_TOPK = 3
DIL_PATTERNS = ((128, 1), (512, 4), (2048, 16))
CONV_WIDTH = 31
GLA_HEADS = 4
GLA_DK = 32
GLA_DV = 64
GLA_RANK = 16
GLA_TAU = 16.0
EPS = 1e-6
NEG = -1e30
LOG2E = 1.4426950408889634

LANES = 128
SUBLANES = 8
ROW_TILE = 512
ATT_TILE = 512
BLOCKS_PER_TILE = ATT_TILE // MOBA_BLOCK
DIL_GROUPS_BACK = max(w for w, _ in DIL_PATTERNS) // ATT_TILE
ALIBI_PIECES = 4
SEL_LANE0 = 16
MASK_BIAS = 2.0 ** 100
M_INIT = -1e29
GLA_CHUNK = 128
GLA_LEVELS = 7
CONV_HALO = 32
LR_PAD = 128
VMEM_LIMIT = 56 * 1024 * 1024

WT_ROWS = 4 * BRANCH
ZC_COLS = 3 * BRANCH
ZD_COLS = 2 * GLA_HEADS * GLA_DK + 2 * BRANCH + LR_PAD
WN_COLS = 4 * BRANCH + ZC_COLS + ZD_COLS

_NT = (((1,), (1,)), ((), ()))
_TN = (((0,), (0,)), ((), ()))


def _params(n_grid):
    return pltpu.CompilerParams(dimension_semantics=("arbitrary",) * n_grid,
                                vmem_limit_bytes=VMEM_LIMIT)


def _silu(x):
    return x * jax.nn.sigmoid(x)


def _group_mean_sq(z, bd):
    z2 = z * z
    hi = z2.astype(BF16)
    lo = (z2 - hi.astype(F32)).astype(BF16)
    return (jnp.dot(hi, bd, preferred_element_type=F32)
            + jnp.dot(lo, bd, preferred_element_type=F32))


def _inproj_kernel(x_ref, ng_ref, wn_ref, wt_ref, kgain_ref, qgain_ref, bd_ref, kx_ref,
                   qat_ref, ka_ref, vat_ref, ga_ref, kma_ref,
                   qbt_ref, kb_ref, vbt_ref, gb_ref, zc_ref, zd_ref):
    x = x_ref[...]
    ms = jnp.mean(x * x, axis=-1, keepdims=True)
    h = (x * lax.rsqrt(ms + EPS) * ng_ref[...]).astype(BF16)
    bd = bd_ref[...]
    kx = kx_ref[...]

    def proj(c0, width):
        return jnp.dot(h, wn_ref[:, c0:c0 + width], preferred_element_type=F32)

    def proj_t(r0):
        return lax.dot_general(wt_ref[r0:r0 + BRANCH, :], h, _NT, preferred_element_type=F32)

    def head_norm(z, row):
        return z * lax.rsqrt(_group_mean_sq(z, bd) + EPS) * kgain_ref[row:row + 1, :]

    def head_norm_t(zt, idx):
        parts = []
        for g in range(N_HEADS):
            part = zt[g * HEAD_DIM:(g + 1) * HEAD_DIM]
            parts.append(part * lax.rsqrt(jnp.mean(part * part, axis=0, keepdims=True) + EPS))
        return jnp.concatenate(parts, axis=0) * qgain_ref[idx]

    def store_keys(ref, kn):
        for hp in range(BRANCH // LANES):
            ref[:, 2 * hp * LANES:(2 * hp + 1) * LANES] = kn[:, hp * LANES:(hp + 1) * LANES].astype(BF16)
            ref[:, (2 * hp + 1) * LANES:(2 * hp + 2) * LANES] = kx

    qat_ref[0] = head_norm_t(proj_t(0), 0)
    ka = head_norm(proj(0, BRANCH), 0)
    store_keys(ka_ref, ka)
    for blk in range(ROW_TILE // MOBA_BLOCK):
        kma_ref[0, blk:blk + 1, :] = jnp.mean(
            ka[blk * MOBA_BLOCK:(blk + 1) * MOBA_BLOCK], axis=0, keepdims=True)
    vat_ref[0] = proj_t(BRANCH).astype(BF16)
    ga_ref[...] = proj(BRANCH, BRANCH)

    qbt_ref[0] = (head_norm_t(proj_t(2 * BRANCH), 1) * (HEAD_DIM ** -0.5 * LOG2E)).astype(BF16)
    store_keys(kb_ref, head_norm(proj(2 * BRANCH, BRANCH), 1))
    vbt_ref[0] = proj_t(3 * BRANCH).astype(BF16)
    gb_ref[...] = proj(3 * BRANCH, BRANCH)

    zc_ref[...] = proj(4 * BRANCH, ZC_COLS)
    zd_ref[...] = proj(4 * BRANCH + ZC_COLS, ZD_COLS)


def _inproj(x2, ng, wn, wt, kgains, qgains, bd, kx, seq):
    n = x2.shape[0]
    nt = n // ROW_TILE
    per_seq = seq // ROW_TILE
    row = lambda i: (i, 0)
    const = lambda i: (0, 0)

    def nat(cols, dtype):
        return (jax.ShapeDtypeStruct((n, cols), dtype), pl.BlockSpec((ROW_TILE, cols), row))

    def tr(dtype):
        return (jax.ShapeDtypeStruct((nt, BRANCH, ROW_TILE), dtype),
                pl.BlockSpec((1, BRANCH, ROW_TILE), lambda i: (i, 0, 0)))

    kmean = (jax.ShapeDtypeStruct((nt, ROW_TILE // MOBA_BLOCK, BRANCH), F32),
             pl.BlockSpec((1, ROW_TILE // MOBA_BLOCK, BRANCH), lambda i: (i, 0, 0)))
    outs = [tr(F32), nat(2 * BRANCH, BF16), tr(BF16), nat(BRANCH, F32), kmean,
            tr(BF16), nat(2 * BRANCH, BF16), tr(BF16), nat(BRANCH, F32),
            nat(ZC_COLS, F32), nat(ZD_COLS, F32)]
    return pl.pallas_call(
        _inproj_kernel,
        grid=(nt,),
        in_specs=[pl.BlockSpec((ROW_TILE, D_MODEL), row),
                  pl.BlockSpec((1, D_MODEL), const),
                  pl.BlockSpec((D_MODEL, WN_COLS), const),
                  pl.BlockSpec((WT_ROWS, D_MODEL), const),
                  pl.BlockSpec((2, BRANCH), const),
                  pl.BlockSpec((2, BRANCH, 1), lambda i: (0, 0, 0)),
                  pl.BlockSpec((BRANCH, BRANCH), const),
                  pl.BlockSpec((ROW_TILE, LANES), lambda i: (i % per_seq, 0))],
        out_specs=[o[1] for o in outs],
        out_shape=[o[0] for o in outs],
        compiler_params=_params(1),
        name="inproj",
    )(x2, ng, wn, wt, kgains, qgains, bd, kx)


def _key_position_lanes(seq):
    pos = np.arange(seq)
    c, n = pos % MOBA_BLOCK, pos // MOBA_BLOCK
    kx = np.zeros((seq, LANES), np.float32)
    p = ALIBI_PIECES
    kx[:, 0:p] = (c // 16)[:, None]
    kx[:, p:2 * p] = (c % 16)[:, None]
    kx[:, 2 * p:3 * p] = n[:, None]
    kx[pos, SEL_LANE0 + n] = 1.0
    return jnp.asarray(kx, BF16)


def _query_alibi_rows(slopes):
    pieces, rest = [], LOG2E
    for _ in range(ALIBI_PIECES):
        piece = float(np.asarray(rest, dtype=BF16).astype(np.float64))
        pieces.append(piece)
        rest -= piece
    p = ALIBI_PIECES
    qx = np.zeros((len(slopes), LANES, ATT_TILE), np.float32)
    for h, slope in enumerate(slopes):
        for g, weight in enumerate((16.0, 1.0, float(MOBA_BLOCK))):
            qx[h, g * p:(g + 1) * p, :] = np.asarray([weight * slope * piece for piece in pieces])[:, None]
    return jnp.asarray(qx, F32)


def _flash_sweep(n_steps, group_of, score_bias, last_mask, k_ref, vt_ref, g_ref, o_ref,
                 qft_buf, s_bufs, p_bufs, a_bufs, smax_bufs, m_buf, acc_buf):
    def issue_scores(t, x):
        rows = pl.ds(pl.multiple_of(group_of(t) * ATT_TILE, ATT_TILE), ATT_TILE)
        for h in range(N_HEADS):
            pair = h // 2
            keys = k_ref[rows, 2 * pair * LANES:2 * (pair + 1) * LANES]
            s = jnp.dot(keys, qft_buf[h], preferred_element_type=F32)
            if score_bias is not None:
                s = s + score_bias(t)
            s_bufs[x][h] = s
            if score_bias is None:
                smax_bufs[x][h] = jnp.max(s, axis=0, keepdims=True)

    def softmax(x, mask):
        for h in range(N_HEADS):
            s = s_bufs[x][h]
            if mask is not None:
                s = mask(s)
            if mask is None and score_bias is None:
                group_max = smax_bufs[x][h]
            else:
                group_max = jnp.max(s, axis=0, keepdims=True)
            m_old = m_buf[h]
            m_new = jnp.maximum(m_old, group_max)
            m_buf[h] = m_new
            a_bufs[x][h] = jnp.exp2(m_old - m_new)
            p_bufs[x][h] = jnp.exp2(s - m_new).astype(BF16)

    ones = jnp.ones((SUBLANES, ATT_TILE), BF16)

    def fold_values(t, x):
        vt = vt_ref[group_of(t)]
        for h in range(N_HEADS):
            lhs = jnp.concatenate([vt[h * HEAD_DIM:(h + 1) * HEAD_DIM, :], ones], axis=0)
            acc_buf[h] = (a_bufs[x][h] * acc_buf[h]
                          + jnp.dot(lhs, p_bufs[x][h], preferred_element_type=F32))

    def regular_step(t, x):
        issue_scores(t + 1, 1 - x)
        fold_values(jnp.maximum(t - 1, 0), 1 - x)
        softmax(x, None)

    m_buf[...] = jnp.full(m_buf.shape, M_INIT, F32)
    acc_buf[...] = jnp.zeros(acc_buf.shape, F32)
    for x in range(2):
        a_bufs[x][...] = jnp.ones(a_bufs[x].shape, F32)
        p_bufs[x][...] = jnp.zeros(p_bufs[x].shape, BF16)

    n_regular = n_steps - 1
    odd = n_regular % 2

    @pl.when(odd == 1)
    def _():
        issue_scores(0, 1)
        regular_step(0, 1)

    @pl.when(odd == 0)
    def _():
        issue_scores(0, 0)

    def step_pair(u, _):
        t = odd + 2 * u
        regular_step(t, 0)
        regular_step(t + 1, 1)
        return 0

    lax.fori_loop(0, n_regular // 2, step_pair, 0)
    fold_values(jnp.maximum(n_steps - 2, 0), 1)
    softmax(0, last_mask)
    fold_values(n_steps - 1, 0)
    out_t = jnp.concatenate(
        [acc_buf[h, 0:HEAD_DIM, :] / acc_buf[h, HEAD_DIM:HEAD_DIM + 1, :] for h in range(N_HEADS)],
        axis=0)
    o_ref[...] = out_t.T * _silu(g_ref[...])


def _head_operands(qt):
    first = lax.broadcasted_iota(jnp.int32, (LANES, qt.shape[1]), 0) < HEAD_DIM
    zero = jnp.zeros((), qt.dtype)
    out = []
    for h in range(N_HEADS):
        pair = qt[(h // 2) * LANES:(h // 2 + 1) * LANES]
        out.append(jnp.where(first, pair, zero) if h % 2 == 0 else jnp.where(first, zero, pair))
    return out


def _attn_scratch():
    stat = pltpu.VMEM((N_HEADS, 1, ATT_TILE), F32)
    return ([pltpu.VMEM((N_HEADS, 2 * LANES, ATT_TILE), BF16)]
            + [pltpu.VMEM((N_HEADS, ATT_TILE, ATT_TILE), F32)] * 2
            + [pltpu.VMEM((N_HEADS, ATT_TILE, ATT_TILE), BF16)] * 2
            + [stat, stat]
            + [stat, stat]
            + [stat]
            + [pltpu.VMEM((N_HEADS, HEAD_DIM + SUBLANES, ATT_TILE), F32)])


def _attn_specs(seq):
    nq = seq // ATT_TILE
    q_tile = pl.BlockSpec((1, BRANCH, ATT_TILE), lambda b, i: (b * nq + i, 0, 0))
    keys = pl.BlockSpec((seq, 2 * BRANCH), lambda b, i: (b, 0))
    values = pl.BlockSpec((nq, BRANCH, ATT_TILE), lambda b, i: (b, 0, 0))
    gate = pl.BlockSpec((ATT_TILE, BRANCH), lambda b, i: (b * nq + i, 0))
    qx = pl.BlockSpec((N_HEADS, LANES, ATT_TILE), lambda b, i: (0, 0, 0),
                      pipeline_mode=pl.Buffered(1))
    return nq, q_tile, keys, values, gate, qx


def _top_k_rows(gate, row_f):
    sel = jnp.zeros(gate.shape, F32)
    for _ in range(MOBA_TOPK):
        top = jnp.max(gate, axis=0, keepdims=True)
        first = jnp.min(jnp.where(gate == top, row_f, 1e9), axis=0, keepdims=True)
        pick = row_f == first
        sel = jnp.where(pick, 1.0, sel)
        gate = jnp.where(pick, -jnp.inf, gate)
    return sel > 0.5


def _moba_kernel(qt_ref, k_ref, vt_ref, km_ref, g_ref, qx_ref, o_ref, qft_buf, *bufs):
    i = pl.program_id(1)
    n_blk = km_ref.shape[0]
    km = km_ref[...]

    blk = lax.broadcasted_iota(jnp.int32, (n_blk, ATT_TILE), 0)
    own = (i * BLOCKS_PER_TILE
           + lax.broadcasted_iota(jnp.int32, (n_blk, ATT_TILE), 1) // MOBA_BLOCK)
    blk_f = blk.astype(F32)
    past = blk < own
    for h, qh in enumerate(_head_operands(qt_ref[0])):
        gate = jnp.dot(km[:, (h // 2) * LANES:(h // 2 + 1) * LANES], qh,
                       precision=lax.Precision.HIGHEST, preferred_element_type=F32)
        keep = (past & _top_k_rows(jnp.where(past, gate, -jnp.inf), blk_f)) | (blk == own)
        bias = jnp.concatenate([jnp.zeros((SEL_LANE0, ATT_TILE), F32),
                                jnp.where(keep, 0.0, -MASK_BIAS),
                                jnp.zeros((LANES - SEL_LANE0 - n_blk, ATT_TILE), F32)], axis=0)
        qft_buf[h] = jnp.concatenate([(qh * (HEAD_DIM ** -0.5 * LOG2E)).astype(BF16),
                                      (bias + qx_ref[h]).astype(BF16)], axis=0)

    causal = (lax.broadcasted_iota(jnp.int32, (ATT_TILE, ATT_TILE), 0)
              <= lax.broadcasted_iota(jnp.int32, (ATT_TILE, ATT_TILE), 1))
    _flash_sweep(i + 1, lambda t: t, None, lambda s: jnp.where(causal, s, NEG),
                 k_ref, vt_ref, g_ref, o_ref, qft_buf, bufs[0:2], bufs[2:4], bufs[4:6], bufs[6:8],
                 *bufs[8:])


def _moba(qt, k, vt, kmean, g, qx, batch, seq):
    nq, q_tile, keys, values, gate, qx_spec = _attn_specs(seq)
    n_blk = seq // MOBA_BLOCK
    assert SEL_LANE0 + n_blk <= LANES
    return pl.pallas_call(
        _moba_kernel,
        grid=(batch, nq),
        in_specs=[q_tile, keys, values,
                  pl.BlockSpec((n_blk, BRANCH), lambda b, i: (b, 0)),
                  gate, qx_spec],
        out_specs=gate,
        out_shape=jax.ShapeDtypeStruct(g.shape, F32),
        scratch_shapes=_attn_scratch(),
        compiler_params=_params(2),
        name="moba",
    )(qt, k, vt, kmean, g, qx)


def _dilated_multiplicity_table():
    idx = np.arange(ATT_TILE)
    delta = (np.arange(DIL_GROUPS_BACK + 1)[:, None, None] * ATT_TILE
             + idx[None, None, :] - idx[None, :, None])
    mult = np.zeros(delta.shape, np.float64)
    for window, dil in DIL_PATTERNS:
        mult += (delta >= 0) & (delta <= window) & (delta % dil == 0)
    return jnp.asarray(np.where(mult > 0, np.log2(np.maximum(mult, 1.0)), NEG), F32)


def _dilated_kernel(qt_ref, k_ref, vt_ref, t_ref, g_ref, qx_ref, o_ref, qft_buf, *bufs):
    i = pl.program_id(1)
    for h, qh in enumerate(_head_operands(qt_ref[0])):
        qft_buf[h] = jnp.concatenate([qh, qx_ref[h].astype(BF16)], axis=0)

    _flash_sweep(jnp.minimum(i, DIL_GROUPS_BACK) + 1, lambda t: i - t, lambda t: t_ref[t], None,
                 k_ref, vt_ref, g_ref, o_ref, qft_buf, bufs[0:2], bufs[2:4], bufs[4:6], bufs[6:8],
                 *bufs[8:])


def _dilated(qt, k, vt, table, g, qx, batch, seq):
    nq, q_tile, keys, values, gate, qx_spec = _attn_specs(seq)
    return pl.pallas_call(
        _dilated_kernel,
        grid=(batch, nq),
        in_specs=[q_tile, keys, values,
                  pl.BlockSpec(table.shape, lambda b, i: (0, 0, 0), pipeline_mode=pl.Buffered(1)),
                  gate, qx_spec],
        out_specs=gate,
        out_shape=jax.ShapeDtypeStruct(g.shape, F32),
        scratch_shapes=_attn_scratch(),
        compiler_params=_params(2),
        name="dilated",
    )(qt, k, vt, table, g, qx)


def _conv_kernel(z_ref, halo_ref, w_ref, b_ref, lng_ref, lnb_ref, pw_ref, pwb_ref, o_ref, u_buf):
    j = pl.program_id(1)

    def glu(z):
        return z[:, 0:BRANCH] * jax.nn.sigmoid(z[:, BRANCH:2 * BRANCH])

    z = z_ref[...]
    u_buf[0, 0:CONV_HALO, :] = jnp.where(j > 0, glu(halo_ref[...]), 0.0)
    u_buf[0, CONV_HALO:, :] = glu(z)
    shifted = CONV_HALO + ROW_TILE - SUBLANES
    for phase in range(1, SUBLANES):
        u_buf[phase, 0:shifted, :] = u_buf[0, phase:phase + shifted, :]
    acc = jnp.zeros((ROW_TILE, BRANCH), F32) + b_ref[...]
    first = CONV_HALO - (CONV_WIDTH - 1)
    for tap in range(CONV_WIDTH):
        phase, start = (first + tap) % SUBLANES, (first + tap) // SUBLANES * SUBLANES
        acc = acc + w_ref[tap:tap + 1, :] * u_buf[phase, start:start + ROW_TILE, :]
    mu = jnp.mean(acc, axis=-1, keepdims=True)
    cen = acc - mu
    var = jnp.mean(cen * cen, axis=-1, keepdims=True)
    un = cen * lax.rsqrt(var + EPS) * lng_ref[...] + lnb_ref[...]
    y = jnp.dot(_silu(un).astype(BF16), pw_ref[...], preferred_element_type=F32) + pwb_ref[...]
    o_ref[...] = y * _silu(z[:, 2 * BRANCH:3 * BRANCH])


def _conv(zc, w, b, lng, lnb, pw, pwb, batch, seq):
    nt = seq // ROW_TILE
    per = ROW_TILE // CONV_HALO
    const = lambda bi, j: (0, 0)
    return pl.pallas_call(
        _conv_kernel,
        grid=(batch, nt),
        in_specs=[pl.BlockSpec((ROW_TILE, ZC_COLS), lambda bi, j: (bi * nt + j, 0)),
                  pl.BlockSpec((CONV_HALO, ZC_COLS),
                               lambda bi, j: (jnp.maximum((bi * nt + j) * per - 1, 0), 0)),
                  pl.BlockSpec((CONV_WIDTH, BRANCH), const),
                  pl.BlockSpec((1, BRANCH), const),
                  pl.BlockSpec((1, BRANCH), const),
                  pl.BlockSpec((1, BRANCH), const),
                  pl.BlockSpec((BRANCH, BRANCH), const),
                  pl.BlockSpec((1, BRANCH), const)],
        out_specs=pl.BlockSpec((ROW_TILE, BRANCH), lambda bi, j: (bi * nt + j, 0)),
        out_shape=jax.ShapeDtypeStruct((batch * seq, BRANCH), F32),
        scratch_shapes=[pltpu.VMEM((SUBLANES, CONV_HALO + ROW_TILE, BRANCH), F32)],
        compiler_params=_params(2),
        name="conv",
    )(zc, zc, w, b, lng, lnb, pw, pwb)


def _gla_sum_matrices():
    c = GLA_CHUNK
    i = np.arange(c)[:, None]
    t = np.arange(c)[None, :]
    mats = [t <= i]
    for l in range(GLA_LEVELS):
        h = (c // 2) >> l
        mid = (i // (2 * h)) * (2 * h) + h
        later = (i & h) != 0
        mats.append((later & (t >= mid) & (t <= i)) | (~later & (t > i) & (t < mid)))
    return jnp.asarray(np.concatenate(mats, axis=0), BF16)


def _gla_kernel(z_ref, sums_ref, wg_ref, bg_ref, gn_ref, bd_ref, o_ref, state_ref):
    c = GLA_CHUNK
    nh = GLA_HEADS
    kw = nh * GLA_DK
    vw = nh * GLA_DV

    @pl.when(pl.program_id(0) == 0)
    def _():
        state_ref[...] = jnp.zeros_like(state_ref)

    row = lax.broadcasted_iota(jnp.int32, (c, kw), 0)
    qi = lax.broadcasted_iota(jnp.int32, (c, nh * c), 0)
    kj = lax.broadcasted_iota(jnp.int32, (c, nh * c), 1) % c
    level_mask = [(qi >> (GLA_LEVELS - l)) == (kj >> (GLA_LEVELS - l)) for l in range(GLA_LEVELS)]
    diag_mask = qi == kj
    k_head = (lax.broadcasted_iota(jnp.int32, (nh * c, kw), 0) // c
              == lax.broadcasted_iota(jnp.int32, (nh * c, kw), 1) // GLA_DK)
    v_head = (lax.broadcasted_iota(jnp.int32, (nh * c, vw), 0) // c
              == lax.broadcasted_iota(jnp.int32, (nh * c, vw), 1) // GLA_DV)
    s_head = (lax.broadcasted_iota(jnp.int32, (vw, kw), 0) // GLA_DV
              == lax.broadcasted_iota(jnp.int32, (vw, kw), 1) // GLA_DK)

    def per_head_keys(kt):
        return jnp.where(k_head, jnp.concatenate([kt] * nh, axis=0), 0.0).astype(BF16)

    def chunk_of(b, rows):
        q = z_ref[b, rows, 0:kw] * GLA_DK ** -0.5
        k = z_ref[b, rows, kw:2 * kw]
        v = z_ref[b, rows, 2 * kw:2 * kw + vw]
        gd = z_ref[b, rows, 2 * kw + vw:2 * kw + 2 * vw]
        lr = z_ref[b, rows, 2 * kw + 2 * vw:2 * kw + 2 * vw + LR_PAD]

        g = jnp.dot(lr.astype(BF16), wg_ref[...], preferred_element_type=F32) + bg_ref[...]
        la = (jnp.minimum(g, 0.0) - jnp.log(1.0 + jnp.exp(-jnp.abs(g)))) / GLA_TAU
        a1 = la.astype(BF16)
        a2 = (la - a1.astype(F32)).astype(BF16)
        parts = jnp.dot(sums_ref[...], jnp.concatenate([a1, a2], axis=1),
                        preferred_element_type=F32)
        sums = parts[:, 0:kw] + parts[:, kw:2 * kw]
        bc = sums[0:c]

        attn = jnp.where(diag_mask,
                         lax.dot_general(q.astype(BF16), per_head_keys(k), _NT,
                                         preferred_element_type=F32), 0.0)
        for l in range(GLA_LEVELS):
            later = (row & ((c // 2) >> l)) != 0
            scaled = jnp.where(later, q, k) * jnp.exp(sums[(1 + l) * c:(2 + l) * c])
            qt = jnp.where(later, scaled, 0.0).astype(BF16)
            a = lax.dot_general(qt, per_head_keys(jnp.where(later, 0.0, scaled)), _NT,
                                preferred_element_type=F32)
            attn = attn + jnp.where(level_mask[l], a, 0.0)

        vb = v.astype(BF16)
        v_stack = jnp.where(v_head, jnp.concatenate([vb] * nh, axis=0), jnp.zeros((), BF16))
        o = jnp.dot(attn.astype(BF16), v_stack, preferred_element_type=F32)

        state = state_ref[b]
        o = o + lax.dot_general((q * jnp.exp(bc)).astype(BF16), state.astype(BF16), _NT,
                                preferred_element_type=F32)
        b_last = bc[c - 1:c, :]
        k_dec = (k * jnp.exp(b_last - bc)).astype(BF16)
        upd = lax.dot_general(vb, k_dec, _TN, preferred_element_type=F32)
        state_ref[b] = state * jnp.exp(b_last) + jnp.where(s_head, upd, 0.0)

        on = o * lax.rsqrt(_group_mean_sq(o, bd_ref[...]) + EPS) * gn_ref[...]
        o_ref[b, rows, :] = on * _silu(gd)

    def chunk(ci, _):
        rows = pl.ds(pl.multiple_of(ci * c, c), c)
        for b in range(z_ref.shape[0]):
            chunk_of(b, rows)
        return 0

    lax.fori_loop(0, ROW_TILE // c, chunk, 0)


def _gla(zd, sums, wg, bg, gn, bd, batch, seq):
    const = lambda j: (0, 0)
    out = pl.pallas_call(
        _gla_kernel,
        grid=(seq // ROW_TILE,),
        in_specs=[pl.BlockSpec((batch, ROW_TILE, ZD_COLS), lambda j: (0, j, 0)),
                  pl.BlockSpec(sums.shape, const),
                  pl.BlockSpec((LR_PAD, GLA_HEADS * GLA_DK), const),
                  pl.BlockSpec((1, GLA_HEADS * GLA_DK), const),
                  pl.BlockSpec((1, BRANCH), const),
                  pl.BlockSpec((BRANCH, BRANCH), const)],
        out_specs=pl.BlockSpec((batch, ROW_TILE, BRANCH), lambda j: (0, j, 0)),
        out_shape=jax.ShapeDtypeStruct((batch, seq, BRANCH), F32),
        scratch_shapes=[pltpu.VMEM((batch, GLA_HEADS * GLA_DV, GLA_HEADS * GLA_DK), F32)],
        compiler_params=_params(1),
        name="gla",
    )(zd.reshape(batch, seq, ZD_COLS), sums, wg, bg, gn, bd)
    return out.reshape(batch * seq, BRANCH)


def _outproj_kernel(x_ref, ya_ref, yb_ref, yc_ref, yd_ref, w_ref, o_ref):
    acc = x_ref[...]
    for g, y_ref in enumerate((ya_ref, yb_ref, yc_ref, yd_ref)):
        acc = acc + jnp.dot(y_ref[...].astype(BF16), w_ref[g * BRANCH:(g + 1) * BRANCH, :],
                            preferred_element_type=F32)
    o_ref[...] = acc


def _outproj(x2, ya, yb, yc, yd, w):
    n = x2.shape[0]
    row = lambda i: (i, 0)
    branch = pl.BlockSpec((ROW_TILE, BRANCH), row)
    return pl.pallas_call(
        _outproj_kernel,
        grid=(n // ROW_TILE,),
        in_specs=[pl.BlockSpec((ROW_TILE, D_MODEL), row), branch, branch, branch, branch,
                  pl.BlockSpec((D_MODEL, D_MODEL), lambda i: (0, 0))],
        out_specs=pl.BlockSpec((ROW_TILE, D_MODEL), row),
        out_shape=jax.ShapeDtypeStruct((n, D_MODEL), F32),
        compiler_params=_params(1),
        name="outproj",
    )(x2, ya, yb, yc, yd, w)


def _pack_w_in(w_in):
    col = lambda j: w_in[:, j * BRANCH:(j + 1) * BRANCH]
    gla0 = 11 * BRANCH
    qkv = 2 * GLA_HEADS * GLA_DK + BRANCH
    pad = jnp.zeros((w_in.shape[0], LR_PAD - GLA_RANK), w_in.dtype)
    wn = jnp.concatenate([col(1), col(3), col(5), col(7), w_in[:, 8 * BRANCH:gla0 + qkv],
                          w_in[:, gla0 + qkv + GLA_RANK:], w_in[:, gla0 + qkv:gla0 + qkv + GLA_RANK], pad],
                         axis=1)
    wt = jnp.concatenate([col(0), col(2), col(4), col(6)], axis=1).T
    return wn.astype(BF16), wt.astype(BF16)


def _layer(x2, batch, seq, consts, norm_g, w_in, q_gain_a, k_gain_a, q_gain_b, k_gain_b, conv_w, conv_b,
           conv_ln_g, conv_ln_b, conv_pw_w, conv_pw_b, gla_gate_w, gla_gate_b, gla_norm_g, w_out):
    bd, kx, qx_moba, qx_dil, dil_table, gla_sums = consts
    kgains = jnp.stack([jnp.tile(k_gain_a, N_HEADS), jnp.tile(k_gain_b, N_HEADS)])
    qgains = jnp.stack([jnp.tile(q_gain_a, N_HEADS), jnp.tile(q_gain_b, N_HEADS)])[:, :, None]
    wn, wt = _pack_w_in(w_in)
    (qat, ka, vat, ga, kma, qbt, kb, vbt, gb, zc, zd) = _inproj(
        x2, norm_g[None, :], wn, wt, kgains, qgains, bd, kx, seq)
    ya = _moba(qat, ka, vat, kma.reshape(-1, BRANCH), ga, qx_moba, batch, seq)
    yb = _dilated(qbt, kb, vbt, dil_table, gb, qx_dil, batch, seq)
    yc = _conv(zc, conv_w, conv_b[None, :], conv_ln_g[None, :], conv_ln_b[None, :],
               conv_pw_w.astype(BF16), conv_pw_b[None, :], batch, seq)
    wg = jnp.concatenate([gla_gate_w, jnp.zeros((LR_PAD - GLA_RANK, gla_gate_w.shape[1]), F32)],
                         axis=0).astype(BF16)
    yd = _gla(zd, gla_sums, wg, gla_gate_b[None, :], jnp.tile(gla_norm_g, GLA_HEADS)[None, :], bd,
              batch, seq)
    return _outproj(x2, ya, yb, yc, yd, w_out.astype(BF16))


def kernel(x, norm_g, w_in, q_gain_a, k_gain_a, q_gain_b, k_gain_b, conv_w, conv_b, conv_ln_g, conv_ln_b,
           conv_pw_w, conv_pw_b, gla_gate_w, gla_gate_b, gla_norm_g, w_out):
    batch, seq, d = x.shape
    assert d == D_MODEL and seq % ROW_TILE == 0 and ROW_TILE == ATT_TILE
    group = np.arange(BRANCH) // HEAD_DIM
    bd = jnp.asarray((group[:, None] == group[None, :]) / HEAD_DIM, BF16)
    heads = np.arange(N_HEADS)
    consts = (bd, _key_position_lanes(seq),
              _query_alibi_rows(2.0 ** -(1.0 + 2 * heads)),
              _query_alibi_rows(2.0 ** -(2.0 + 2 * heads)),
              _dilated_multiplicity_table(), _gla_sum_matrices())
    x2 = x.reshape(batch * seq, d)
    params = (norm_g, w_in, q_gain_a, k_gain_a, q_gain_b, k_gain_b, conv_w, conv_b, conv_ln_g,
              conv_ln_b, conv_pw_w, conv_pw_b, gla_gate_w, gla_gate_b, gla_norm_g, w_out)
    for layer in range(norm_g.shape[0]):
        x2 = _layer(x2, batch, seq, consts, *(p[layer] for p in params))
    return x2.reshape(batch, seq, d)
```

```python
import numpy as np
import jax
import jax.numpy as jnp
from jax import lax
from jax.experimental import pallas as pl
from jax.experimental.pallas import tpu as pltpu

F32 = jnp.float32
BF16 = jnp.bfloat16

D_MODEL = 1024
BRANCH = 256
HEAD_DIM = 64
N_HEADS = BRANCH // HEAD_DIM
MOBA_BLOCK = 256
MOBA_TOPK = 3
DIL_PATTERNS = ((128, 1), (512, 4), (2048, 16))
CONV_WIDTH = 31
GLA_HEADS = 4
GLA_DK = 32
GLA_DV = 64
GLA_RANK = 16
GLA_TAU = 16.0
EPS = 1e-6
NEG = -1e30
LOG2E = 1.4426950408889634

LANES = 128
SUBLANES = 8
ROW_TILE = 512
ATT_TILE = 512
BLOCKS_PER_TILE = ATT_TILE // MOBA_BLOCK
DIL_GROUPS_BACK = max(w for w, _ in DIL_PATTERNS) // ATT_TILE
ALIBI_PIECES = 4
SEL_LANE0 = 16
MASK_BIAS = 2.0 ** 100
M_INIT = -1e29
GLA_CHUNK = 128
GLA_LEVELS = 7
CONV_HALO = 32
LR_PAD = 128
VMEM_LIMIT = 56 * 1024 * 1024

WT_ROWS = 4 * BRANCH
ZC_COLS = 3 * BRANCH
ZD_COLS = 2 * GLA_HEADS * GLA_DK + 2 * BRANCH + LR_PAD
WN_COLS = 4 * BRANCH + ZC_COLS + ZD_COLS

_NT = (((1,), (1,)), ((), ()))
_TN = (((0,), (0,)), ((), ()))


def _params(n_grid):
    return pltpu.CompilerParams(dimension_semantics=("arbitrary",) * n_grid,
                                vmem_limit_bytes=VMEM_LIMIT)


def _silu(x):
    return x * jax.nn.sigmoid(x)


def _group_mean_sq(z, bd):
    z2 = z * z
    hi = z2.astype(BF16)
    lo = (z2 - hi.astype(F32)).astype(BF16)
    return (jnp.dot(hi, bd, preferred_element_type=F32)
            + jnp.dot(lo, bd, preferred_element_type=F32))


def _inproj_kernel(x_ref, ng_ref, wn_ref, wt_ref, kgain_ref, qgain_ref, bd_ref, kx_ref,
                   qat_ref, ka_ref, vat_ref, ga_ref, kma_ref,
                   qbt_ref, kb_ref, vbt_ref, gb_ref, zc_ref, zd_ref):
    x = x_ref[...]
    ms = jnp.mean(x * x, axis=-1, keepdims=True)
    h = (x * lax.rsqrt(ms + EPS) * ng_ref[...]).astype(BF16)
    bd = bd_ref[...]
    kx = kx_ref[...]

    def proj(c0, width):
        return jnp.dot(h, wn_ref[:, c0:c0 + width], preferred_element_type=F32)

    def proj_t(r0):
        return lax.dot_general(wt_ref[r0:r0 + BRANCH, :], h, _NT, preferred_element_type=F32)

    def head_norm(z, row):
        return z * lax.rsqrt(_group_mean_sq(z, bd) + EPS) * kgain_ref[row:row + 1, :]

    def head_norm_t(zt, idx):
        parts = []
        for g in range(N_HEADS):
            part = zt[g * HEAD_DIM:(g + 1) * HEAD_DIM]
            parts.append(part * lax.rsqrt(jnp.mean(part * part, axis=0, keepdims=True) + EPS))
        return jnp.concatenate(parts, axis=0) * qgain_ref[idx]

    def store_keys(ref, kn):
        for hp in range(BRANCH // LANES):
            ref[:, 2 * hp * LANES:(2 * hp + 1) * LANES] = kn[:, hp * LANES:(hp + 1) * LANES].astype(BF16)
            ref[:, (2 * hp + 1) * LANES:(2 * hp + 2) * LANES] = kx

    qat_ref[0] = head_norm_t(proj_t(0), 0)
    ka = head_norm(proj(0, BRANCH), 0)
    store_keys(ka_ref, ka)
    for blk in range(ROW_TILE // MOBA_BLOCK):
        kma_ref[0, blk:blk + 1, :] = jnp.mean(
            ka[blk * MOBA_BLOCK:(blk + 1) * MOBA_BLOCK], axis=0, keepdims=True)
    vat_ref[0] = proj_t(BRANCH).astype(BF16)
    ga_ref[...] = proj(BRANCH, BRANCH)

    qbt_ref[0] = (head_norm_t(proj_t(2 * BRANCH), 1) * (HEAD_DIM ** -0.5 * LOG2E)).astype(BF16)
    store_keys(kb_ref, head_norm(proj(2 * BRANCH, BRANCH), 1))
    vbt_ref[0] = proj_t(3 * BRANCH).astype(BF16)
    gb_ref[...] = proj(3 * BRANCH, BRANCH)

    zc_ref[...] = proj(4 * BRANCH, ZC_COLS)
    zd_ref[...] = proj(4 * BRANCH + ZC_COLS, ZD_COLS)


def _inproj(x2, ng, wn, wt, kgains, qgains, bd, kx, seq):
    n = x2.shape[0]
    nt = n // ROW_TILE
    per_seq = seq // ROW_TILE
    row = lambda i: (i, 0)
    const = lambda i: (0, 0)

    def nat(cols, dtype):
        return (jax.ShapeDtypeStruct((n, cols), dtype), pl.BlockSpec((ROW_TILE, cols), row))

    def tr(dtype):
        return (jax.ShapeDtypeStruct((nt, BRANCH, ROW_TILE), dtype),
                pl.BlockSpec((1, BRANCH, ROW_TILE), lambda i: (i, 0, 0)))

    kmean = (jax.ShapeDtypeStruct((nt, ROW_TILE // MOBA_BLOCK, BRANCH), F32),
             pl.BlockSpec((1, ROW_TILE // MOBA_BLOCK, BRANCH), lambda i: (i, 0, 0)))
    outs = [tr(F32), nat(2 * BRANCH, BF16), tr(BF16), nat(BRANCH, F32), kmean,
            tr(BF16), nat(2 * BRANCH, BF16), tr(BF16), nat(BRANCH, F32),
            nat(ZC_COLS, F32), nat(ZD_COLS, F32)]
    return pl.pallas_call(
        _inproj_kernel,
        grid=(nt,),
        in_specs=[pl.BlockSpec((ROW_TILE, D_MODEL), row),
                  pl.BlockSpec((1, D_MODEL), const),
                  pl.BlockSpec((D_MODEL, WN_COLS), const),
                  pl.BlockSpec((WT_ROWS, D_MODEL), const),
                  pl.BlockSpec((2, BRANCH), const),
                  pl.BlockSpec((2, BRANCH, 1), lambda i: (0, 0, 0)),
                  pl.BlockSpec((BRANCH, BRANCH), const),
                  pl.BlockSpec((ROW_TILE, LANES), lambda i: (i % per_seq, 0))],
        out_specs=[o[1] for o in outs],
        out_shape=[o[0] for o in outs],
        compiler_params=_params(1),
        name="inproj",
    )(x2, ng, wn, wt, kgains, qgains, bd, kx)


def _key_position_lanes(seq):
    pos = np.arange(seq)
    c, n = pos % MOBA_BLOCK, pos // MOBA_BLOCK
    kx = np.zeros((seq, LANES), np.float32)
    p = ALIBI_PIECES
    kx[:, 0:p] = (c // 16)[:, None]
    kx[:, p:2 * p] = (c % 16)[:, None]
    kx[:, 2 * p:3 * p] = n[:, None]
    kx[pos, SEL_LANE0 + n] = 1.0
    return jnp.asarray(kx, BF16)


def _query_alibi_rows(slopes):
    pieces, rest = [], LOG2E
    for _ in range(ALIBI_PIECES):
        piece = float(np.asarray(rest, dtype=BF16).astype(np.float64))
        pieces.append(piece)
        rest -= piece
    p = ALIBI_PIECES
    qx = np.zeros((len(slopes), LANES, ATT_TILE), np.float32)
    for h, slope in enumerate(slopes):
        for g, weight in enumerate((16.0, 1.0, float(MOBA_BLOCK))):
            qx[h, g * p:(g + 1) * p, :] = np.asarray([weight * slope * piece for piece in pieces])[:, None]
    return jnp.asarray(qx, F32)


def _flash_sweep(n_steps, group_of, score_bias, last_mask, k_ref, vt_ref, g_ref, o_ref,
                 qft_buf, s_bufs, p_bufs, a_bufs, smax_bufs, m_buf, acc_buf):
    def issue_scores(t, x):
        rows = pl.ds(pl.multiple_of(group_of(t) * ATT_TILE, ATT_TILE), ATT_TILE)
        for h in range(N_HEADS):
            pair = h // 2
            keys = k_ref[rows, 2 * pair * LANES:2 * (pair + 1) * LANES]
            s = jnp.dot(keys, qft_buf[h], preferred_element_type=F32)
            if score_bias is not None:
                s = s + score_bias(t)
            s_bufs[x][h] = s
            if score_bias is None:
                smax_bufs[x][h] = jnp.max(s, axis=0, keepdims=True)

    def softmax(x, mask):
        for h in range(N_HEADS):
            s = s_bufs[x][h]
            if mask is not None:
                s = mask(s)
            if mask is None and score_bias is None:
                group_max = smax_bufs[x][h]
            else:
                group_max = jnp.max(s, axis=0, keepdims=True)
            m_old = m_buf[h]
            m_new = jnp.maximum(m_old, group_max)
            m_buf[h] = m_new
            a_bufs[x][h] = jnp.exp2(m_old - m_new)
            p_bufs[x][h] = jnp.exp2(s - m_new).astype(BF16)

    ones = jnp.ones((SUBLANES, ATT_TILE), BF16)

    def fold_values(t, x):
        vt = vt_ref[group_of(t)]
        for h in range(N_HEADS):
            lhs = jnp.concatenate([vt[h * HEAD_DIM:(h + 1) * HEAD_DIM, :], ones], axis=0)
            acc_buf[h] = (a_bufs[x][h] * acc_buf[h]
                          + jnp.dot(lhs, p_bufs[x][h], preferred_element_type=F32))

    def regular_step(t, x):
        issue_scores(t + 1, 1 - x)
        fold_values(jnp.maximum(t - 1, 0), 1 - x)
        softmax(x, None)

    m_buf[...] = jnp.full(m_buf.shape, M_INIT, F32)
    acc_buf[...] = jnp.zeros(acc_buf.shape, F32)
    for x in range(2):
        a_bufs[x][...] = jnp.ones(a_bufs[x].shape, F32)
        p_bufs[x][...] = jnp.zeros(p_bufs[x].shape, BF16)

    n_regular = n_steps - 1
    odd = n_regular % 2

    @pl.when(odd == 1)
    def _():
        issue_scores(0, 1)
        regular_step(0, 1)

    @pl.when(odd == 0)
    def _():
        issue_scores(0, 0)

    def step_pair(u, _):
        t = odd + 2 * u
        regular_step(t, 0)
        regular_step(t + 1, 1)
        return 0

    lax.fori_loop(0, n_regular // 2, step_pair, 0)
    fold_values(jnp.maximum(n_steps - 2, 0), 1)
    softmax(0, last_mask)
    fold_values(n_steps - 1, 0)
    out_t = jnp.concatenate(
        [acc_buf[h, 0:HEAD_DIM, :] / acc_buf[h, HEAD_DIM:HEAD_DIM + 1, :] for h in range(N_HEADS)],
        axis=0)
    o_ref[...] = out_t.T * _silu(g_ref[...])


def _head_operands(qt):
    first = lax.broadcasted_iota(jnp.int32, (LANES, qt.shape[1]), 0) < HEAD_DIM
    zero = jnp.zeros((), qt.dtype)
    out = []
    for h in range(N_HEADS):
        pair = qt[(h // 2) * LANES:(h // 2 + 1) * LANES]
        out.append(jnp.where(first, pair, zero) if h % 2 == 0 else jnp.where(first, zero, pair))
    return out


def _attn_scratch():
    stat = pltpu.VMEM((N_HEADS, 1, ATT_TILE), F32)
    return ([pltpu.VMEM((N_HEADS, 2 * LANES, ATT_TILE), BF16)]
            + [pltpu.VMEM((N_HEADS, ATT_TILE, ATT_TILE), F32)] * 2
            + [pltpu.VMEM((N_HEADS, ATT_TILE, ATT_TILE), BF16)] * 2
            + [stat, stat]
            + [stat, stat]
            + [stat]
            + [pltpu.VMEM((N_HEADS, HEAD_DIM + SUBLANES, ATT_TILE), F32)])


def _attn_specs(seq):
    nq = seq // ATT_TILE
    q_tile = pl.BlockSpec((1, BRANCH, ATT_TILE), lambda b, i: (b * nq + i, 0, 0))
    keys = pl.BlockSpec((seq, 2 * BRANCH), lambda b, i: (b, 0))
    values = pl.BlockSpec((nq, BRANCH, ATT_TILE), lambda b, i: (b, 0, 0))
    gate = pl.BlockSpec((ATT_TILE, BRANCH), lambda b, i: (b * nq + i, 0))
    qx = pl.BlockSpec((N_HEADS, LANES, ATT_TILE), lambda b, i: (0, 0, 0),
                      pipeline_mode=pl.Buffered(1))
    return nq, q_tile, keys, values, gate, qx


def _top_k_rows(gate, row_f):
    sel = jnp.zeros(gate.shape, F32)
    for _ in range(MOBA_TOPK):
        top = jnp.max(gate, axis=0, keepdims=True)
        first = jnp.min(jnp.where(gate == top, row_f, 1e9), axis=0, keepdims=True)
        pick = row_f == first
        sel = jnp.where(pick, 1.0, sel)
        gate = jnp.where(pick, -jnp.inf, gate)
    return sel > 0.5


def _moba_kernel(qt_ref, k_ref, vt_ref, km_ref, g_ref, qx_ref, o_ref, qft_buf, *bufs):
    i = pl.program_id(1)
    n_blk = km_ref.shape[0]
    km = km_ref[...]

    blk = lax.broadcasted_iota(jnp.int32, (n_blk, ATT_TILE), 0)
    own = (i * BLOCKS_PER_TILE
           + lax.broadcasted_iota(jnp.int32, (n_blk, ATT_TILE), 1) // MOBA_BLOCK)
    blk_f = blk.astype(F32)
    past = blk < own
    for h, qh in enumerate(_head_operands(qt_ref[0])):
        gate = jnp.dot(km[:, (h // 2) * LANES:(h // 2 + 1) * LANES], qh,
                       precision=lax.Precision.HIGHEST, preferred_element_type=F32)
        keep = (past & _top_k_rows(jnp.where(past, gate, -jnp.inf), blk_f)) | (blk == own)
        bias = jnp.concatenate([jnp.zeros((SEL_LANE0, ATT_TILE), F32),
                                jnp.where(keep, 0.0, -MASK_BIAS),
                                jnp.zeros((LANES - SEL_LANE0 - n_blk, ATT_TILE), F32)], axis=0)
        qft_buf[h] = jnp.concatenate([(qh * (HEAD_DIM ** -0.5 * LOG2E)).astype(BF16),
                                      (bias + qx_ref[h]).astype(BF16)], axis=0)

    causal = (lax.broadcasted_iota(jnp.int32, (ATT_TILE, ATT_TILE), 0)
              <= lax.broadcasted_iota(jnp.int32, (ATT_TILE, ATT_TILE), 1))
    _flash_sweep(i + 1, lambda t: t, None, lambda s: jnp.where(causal, s, NEG),
                 k_ref, vt_ref, g_ref, o_ref, qft_buf, bufs[0:2], bufs[2:4], bufs[4:6], bufs[6:8],
                 *bufs[8:])


def _moba(qt, k, vt, kmean, g, qx, batch, seq):
    nq, q_tile, keys, values, gate, qx_spec = _attn_specs(seq)
    n_blk = seq // MOBA_BLOCK
    assert SEL_LANE0 + n_blk <= LANES
    return pl.pallas_call(
        _moba_kernel,
        grid=(batch, nq),
        in_specs=[q_tile, keys, values,
                  pl.BlockSpec((n_blk, BRANCH), lambda b, i: (b, 0)),
                  gate, qx_spec],
        out_specs=gate,
        out_shape=jax.ShapeDtypeStruct(g.shape, F32),
        scratch_shapes=_attn_scratch(),
        compiler_params=_params(2),
        name="moba",
    )(qt, k, vt, kmean, g, qx)


def _dilated_multiplicity_table():
    idx = np.arange(ATT_TILE)
    delta = (np.arange(DIL_GROUPS_BACK + 1)[:, None, None] * ATT_TILE
             + idx[None, None, :] - idx[None, :, None])
    mult = np.zeros(delta.shape, np.float64)
    for window, dil in DIL_PATTERNS:
        mult += (delta >= 0) & (delta <= window) & (delta % dil == 0)
    return jnp.asarray(np.where(mult > 0, np.log2(np.maximum(mult, 1.0)), NEG), F32)


def _dilated_kernel(qt_ref, k_ref, vt_ref, t_ref, g_ref, qx_ref, o_ref, qft_buf, *bufs):
    i = pl.program_id(1)
    for h, qh in enumerate(_head_operands(qt_ref[0])):
        qft_buf[h] = jnp.concatenate([qh, qx_ref[h].astype(BF16)], axis=0)

    _flash_sweep(jnp.minimum(i, DIL_GROUPS_BACK) + 1, lambda t: i - t, lambda t: t_ref[t], None,
                 k_ref, vt_ref, g_ref, o_ref, qft_buf, bufs[0:2], bufs[2:4], bufs[4:6], bufs[6:8],
                 *bufs[8:])


def _dilated(qt, k, vt, table, g, qx, batch, seq):
    nq, q_tile, keys, values, gate, qx_spec = _attn_specs(seq)
    return pl.pallas_call(
        _dilated_kernel,
        grid=(batch, nq),
        in_specs=[q_tile, keys, values,
                  pl.BlockSpec(table.shape, lambda b, i: (0, 0, 0), pipeline_mode=pl.Buffered(1)),
                  gate, qx_spec],
        out_specs=gate,
        out_shape=jax.ShapeDtypeStruct(g.shape, F32),
        scratch_shapes=_attn_scratch(),
        compiler_params=_params(2),
        name="dilated",
    )(qt, k, vt, table, g, qx)


def _conv_module(z, halo, has_history, w_ref, b_ref, lng_ref, lnb_ref, pw_ref, pwb_ref, u_buf):
    def glu(z):
        return z[:, 0:BRANCH] * jax.nn.sigmoid(z[:, BRANCH:2 * BRANCH])

    u_buf[0, 0:CONV_HALO, :] = jnp.where(has_history, glu(halo), 0.0)
    u_buf[0, CONV_HALO:, :] = glu(z)
    shifted = CONV_HALO + ROW_TILE - SUBLANES
    for phase in range(1, SUBLANES):
        u_buf[phase, 0:shifted, :] = u_buf[0, phase:phase + shifted, :]
    acc = jnp.zeros((ROW_TILE, BRANCH), F32) + b_ref[...]
    first = CONV_HALO - (CONV_WIDTH - 1)
    for tap in range(CONV_WIDTH):
        phase, start = (first + tap) % SUBLANES, (first + tap) // SUBLANES * SUBLANES
        acc = acc + w_ref[tap:tap + 1, :] * u_buf[phase, start:start + ROW_TILE, :]
    mu = jnp.mean(acc, axis=-1, keepdims=True)
    cen = acc - mu
    var = jnp.mean(cen * cen, axis=-1, keepdims=True)
    un = cen * lax.rsqrt(var + EPS) * lng_ref[...] + lnb_ref[...]
    y = jnp.dot(_silu(un).astype(BF16), pw_ref[...], preferred_element_type=F32) + pwb_ref[...]
    return y * _silu(z[:, 2 * BRANCH:3 * BRANCH])


def _gla_sum_matrices():
    c = GLA_CHUNK
    i = np.arange(c)[:, None]
    t = np.arange(c)[None, :]
    mats = [t <= i]
    for l in range(GLA_LEVELS):
        h = (c // 2) >> l
        mid = (i // (2 * h)) * (2 * h) + h
        later = (i & h) != 0
        mats.append((later & (t >= mid) & (t <= i)) | (~later & (t > i) & (t < mid)))
    return jnp.asarray(np.concatenate(mats, axis=0), BF16)


def _gla_tile(z_ref, sums_ref, wg_ref, bg_ref, gn_ref, bd_ref, o_ref, state_ref):
    c = GLA_CHUNK
    nh = GLA_HEADS
    kw = nh * GLA_DK
    vw = nh * GLA_DV

    row = lax.broadcasted_iota(jnp.int32, (c, kw), 0)
    qi = lax.broadcasted_iota(jnp.int32, (c, nh * c), 0)
    kj = lax.broadcasted_iota(jnp.int32, (c, nh * c), 1) % c
    level_mask = [(qi >> (GLA_LEVELS - l)) == (kj >> (GLA_LEVELS - l)) for l in range(GLA_LEVELS)]
    diag_mask = qi == kj
    k_head = (lax.broadcasted_iota(jnp.int32, (nh * c, kw), 0) // c
              == lax.broadcasted_iota(jnp.int32, (nh * c, kw), 1) // GLA_DK)
    v_head = (lax.broadcasted_iota(jnp.int32, (nh * c, vw), 0) // c
              == lax.broadcasted_iota(jnp.int32, (nh * c, vw), 1) // GLA_DV)
    s_head = (lax.broadcasted_iota(jnp.int32, (vw, kw), 0) // GLA_DV
              == lax.broadcasted_iota(jnp.int32, (vw, kw), 1) // GLA_DK)

    def per_head_keys(kt):
        return jnp.where(k_head, jnp.concatenate([kt] * nh, axis=0), 0.0).astype(BF16)

    def chunk_of(b, rows):
        q = z_ref[b, rows, 0:kw] * GLA_DK ** -0.5
        k = z_ref[b, rows, kw:2 * kw]
        v = z_ref[b, rows, 2 * kw:2 * kw + vw]
        gd = z_ref[b, rows, 2 * kw + vw:2 * kw + 2 * vw]
        lr = z_ref[b, rows, 2 * kw + 2 * vw:2 * kw + 2 * vw + LR_PAD]

        g = jnp.dot(lr.astype(BF16), wg_ref[...], preferred_element_type=F32) + bg_ref[...]
        la = (jnp.minimum(g, 0.0) - jnp.log(1.0 + jnp.exp(-jnp.abs(g)))) / GLA_TAU
        a1 = la.astype(BF16)
        a2 = (la - a1.astype(F32)).astype(BF16)
        parts = jnp.dot(sums_ref[...], jnp.concatenate([a1, a2], axis=1),
                        preferred_element_type=F32)
        sums = parts[:, 0:kw] + parts[:, kw:2 * kw]
        bc = sums[0:c]

        attn = jnp.where(diag_mask,
                         lax.dot_general(q.astype(BF16), per_head_keys(k), _NT,
                                         preferred_element_type=F32), 0.0)
        for l in range(GLA_LEVELS):
            later = (row & ((c // 2) >> l)) != 0
            scaled = jnp.where(later, q, k) * jnp.exp(sums[(1 + l) * c:(2 + l) * c])
            qt = jnp.where(later, scaled, 0.0).astype(BF16)
            a = lax.dot_general(qt, per_head_keys(jnp.where(later, 0.0, scaled)), _NT,
                                preferred_element_type=F32)
            attn = attn + jnp.where(level_mask[l], a, 0.0)

        vb = v.astype(BF16)
        v_stack = jnp.where(v_head, jnp.concatenate([vb] * nh, axis=0), jnp.zeros((), BF16))
        o = jnp.dot(attn.astype(BF16), v_stack, preferred_element_type=F32)

        state = state_ref[b]
        o = o + lax.dot_general((q * jnp.exp(bc)).astype(BF16), state.astype(BF16), _NT,
                                preferred_element_type=F32)
        b_last = bc[c - 1:c, :]
        k_dec = (k * jnp.exp(b_last - bc)).astype(BF16)
        upd = lax.dot_general(vb, k_dec, _TN, preferred_element_type=F32)
        state_ref[b] = state * jnp.exp(b_last) + jnp.where(s_head, upd, 0.0)

        on = o * lax.rsqrt(_group_mean_sq(o, bd_ref[...]) + EPS) * gn_ref[...]
        o_ref[b, rows, :] = on * _silu(gd)

    for ci in range(ROW_TILE // c):
        for b in range(z_ref.shape[0]):
            chunk_of(b, pl.ds(ci * c, c))


def _tail_kernel(x_ref, ya_ref, yb_ref, zc_ref, halo_ref, zd_ref,
                 cw_ref, cb_ref, lng_ref, lnb_ref, pw_ref, pwb_ref,
                 sums_ref, wg_ref, bg_ref, gn_ref, bd_ref, wo_ref,
                 o_ref, u_buf, yd_buf, state_ref):
    j = pl.program_id(0)

    @pl.when(j == 0)
    def _():
        state_ref[...] = jnp.zeros_like(state_ref)

    def project(y, g):
        return jnp.dot(y.astype(BF16), wo_ref[g * BRANCH:(g + 1) * BRANCH, :],
                       preferred_element_type=F32)

    _gla_tile(zd_ref, sums_ref, wg_ref, bg_ref, gn_ref, bd_ref, yd_buf, state_ref)
    for b in range(x_ref.shape[0]):
        yc = _conv_module(zc_ref[b], halo_ref[b], j > 0, cw_ref, cb_ref, lng_ref, lnb_ref,
                          pw_ref, pwb_ref, u_buf.at[b])
        o_ref[b] = (x_ref[b] + project(ya_ref[b], 0) + project(yb_ref[b], 1) + project(yc, 2)
                    + project(yd_buf[b], 3))


def _tail(x2, ya, yb, zc, zd, conv_consts, gla_consts, w_out, batch, seq):
    per = ROW_TILE // CONV_HALO
    tile = lambda cols: pl.BlockSpec((batch, ROW_TILE, cols), lambda j: (0, j, 0))
    whole = lambda a: pl.BlockSpec(a.shape, lambda j: (0,) * a.ndim, pipeline_mode=pl.Buffered(1))
    by_seq = lambda a: a.reshape(batch, seq, a.shape[-1])
    consts = (*conv_consts, *gla_consts, w_out)
    out = pl.pallas_call(
        _tail_kernel,
        grid=(seq // ROW_TILE,),
        in_specs=[tile(D_MODEL), tile(BRANCH), tile(BRANCH), tile(ZC_COLS),
                  pl.BlockSpec((batch, CONV_HALO, ZC_COLS),
                               lambda j: (0, jnp.maximum(j * per - 1, 0), 0)),
                  tile(ZD_COLS)] + [whole(a) for a in consts],
        out_specs=tile(D_MODEL),
        out_shape=jax.ShapeDtypeStruct((batch, seq, D_MODEL), F32),
        scratch_shapes=[pltpu.VMEM((batch, SUBLANES, CONV_HALO + ROW_TILE, BRANCH), F32),
                        pltpu.VMEM((batch, ROW_TILE, BRANCH), F32),
                        pltpu.VMEM((batch, GLA_HEADS * GLA_DV, GLA_HEADS * GLA_DK), F32)],
        compiler_params=_params(1),
        name="tail",
    )(by_seq(x2), by_seq(ya), by_seq(yb), by_seq(zc), by_seq(zc), by_seq(zd), *consts)
    return out.reshape(batch * seq, D_MODEL)


def _pack_w_in(w_in):
    col = lambda j: w_in[:, j * BRANCH:(j + 1) * BRANCH]
    gla0 = 11 * BRANCH
    qkv = 2 * GLA_HEADS * GLA_DK + BRANCH
    pad = jnp.zeros((w_in.shape[0], LR_PAD - GLA_RANK), w_in.dtype)
    wn = jnp.concatenate([col(1), col(3), col(5), col(7), w_in[:, 8 * BRANCH:gla0 + qkv],
                          w_in[:, gla0 + qkv + GLA_RANK:], w_in[:, gla0 + qkv:gla0 + qkv + GLA_RANK], pad],
                         axis=1)
    wt = jnp.concatenate([col(0), col(2), col(4), col(6)], axis=1).T
    return wn.astype(BF16), wt.astype(BF16)


def _layer(x2, batch, seq, consts, norm_g, w_in, q_gain_a, k_gain_a, q_gain_b, k_gain_b, conv_w, conv_b,
           conv_ln_g, conv_ln_b, conv_pw_w, conv_pw_b, gla_gate_w, gla_gate_b, gla_norm_g, w_out):
    bd, kx, qx_moba, qx_dil, dil_table, gla_sums = consts
    kgains = jnp.stack([jnp.tile(k_gain_a, N_HEADS), jnp.tile(k_gain_b, N_HEADS)])
    qgains = jnp.stack([jnp.tile(q_gain_a, N_HEADS), jnp.tile(q_gain_b, N_HEADS)])[:, :, None]
    wn, wt = _pack_w_in(w_in)
    (qat, ka, vat, ga, kma, qbt, kb, vbt, gb, zc, zd) = _inproj(
        x2, norm_g[None, :], wn, wt, kgains, qgains, bd, kx, seq)
    ya = _moba(qat, ka, vat, kma.reshape(-1, BRANCH), ga, qx_moba, batch, seq)
    yb = _dilated(qbt, kb, vbt, dil_table, gb, qx_dil, batch, seq)
    wg = jnp.concatenate([gla_gate_w, jnp.zeros((LR_PAD - GLA_RANK, gla_gate_w.shape[1]), F32)],
                         axis=0).astype(BF16)
    conv_consts = (conv_w, conv_b[None, :], conv_ln_g[None, :], conv_ln_b[None, :],
                   conv_pw_w.astype(BF16), conv_pw_b[None, :])
    gla_consts = (gla_sums, wg, gla_gate_b[None, :], jnp.tile(gla_norm_g, GLA_HEADS)[None, :], bd)
    return _tail(x2, ya, yb, zc, zd, conv_consts, gla_consts, w_out.astype(BF16), batch, seq)


def kernel(x, norm_g, w_in, q_gain_a, k_gain_a, q_gain_b, k_gain_b, conv_w, conv_b, conv_ln_g, conv_ln_b,
           conv_pw_w, conv_pw_b, gla_gate_w, gla_gate_b, gla_norm_g, w_out):
    batch, seq, d = x.shape
    assert d == D_MODEL and seq % ROW_TILE == 0 and ROW_TILE == ATT_TILE
    group = np.arange(BRANCH) // HEAD_DIM
    bd = jnp.asarray((group[:, None] == group[None, :]) / HEAD_DIM, BF16)
    heads = np.arange(N_HEADS)
    consts = (bd, _key_position_lanes(seq),
              _query_alibi_rows(2.0 ** -(1.0 + 2 * heads)),
              _query_alibi_rows(2.0 ** -(2.0 + 2 * heads)),
              _dilated_multiplicity_table(), _gla_sum_matrices())
    x2 = x.reshape(batch * seq, d)
    params = (norm_g, w_in, q_gain_a, k_gain_a, q_gain_b, k_gain_b, conv_w, conv_b, conv_ln_g,
              conv_ln_b, conv_pw_w, conv_pw_b, gla_gate_w, gla_gate_b, gla_norm_g, w_out)
    for layer in range(norm_g.shape[0]):
        x2 = _layer(x2, batch, seq, consts, *(p[layer] for p in params))
    return x2.reshape(batch, seq, d)
```

```python
import functools

import numpy as np
import jax
import jax.numpy as jnp
from jax import lax
from jax.experimental import pallas as pl
from jax.experimental.pallas import tpu as pltpu

F32 = jnp.float32
BF16 = jnp.bfloat16

D_MODEL = 1024
BRANCH = 256
HEAD_DIM = 64
N_HEADS = BRANCH // HEAD_DIM
MOBA_BLOCK = 256
MOBA_TOPK = 3
DIL_PATTERNS = ((128, 1), (512, 4), (2048, 16))
CONV_WIDTH = 31
GLA_HEADS = 4
GLA_DK = 32
GLA_DV = 64
GLA_RANK = 16
GLA_TAU = 16.0
EPS = 1e-6
NEG = -1e30
LOG2E = 1.4426950408889634

LANES = 128
SUBLANES = 8
ROW_TILE = 512
ATT_TILE = 512
BLOCKS_PER_TILE = ATT_TILE // MOBA_BLOCK
DIL_GROUPS_BACK = max(w for w, _ in DIL_PATTERNS) // ATT_TILE
ALIBI_PIECES = 4
SEL_LANE0 = 16
MASK_BIAS = 2.0 ** 100
M_INIT = -1e29
GLA_CHUNK = 128
GLA_LEVELS = 7
CONV_HALO = 32
LR_PAD = 128
VMEM_LIMIT = 56 * 1024 * 1024

WT_ROWS = 4 * BRANCH
ZC_COLS = 3 * BRANCH
ZD_COLS = 2 * GLA_HEADS * GLA_DK + 2 * BRANCH + LR_PAD
WN_COLS = 4 * BRANCH + ZC_COLS + ZD_COLS

_NT = (((1,), (1,)), ((), ()))
_TN = (((0,), (0,)), ((), ()))


def _params(n_grid):
    return pltpu.CompilerParams(dimension_semantics=("arbitrary",) * n_grid,
                                vmem_limit_bytes=VMEM_LIMIT)


def _silu(x):
    return x * jax.nn.sigmoid(x)


def _group_mean_sq(z, bd):
    z2 = z * z
    hi = z2.astype(BF16)
    lo = (z2 - hi.astype(F32)).astype(BF16)
    return (jnp.dot(hi, bd, preferred_element_type=F32)
            + jnp.dot(lo, bd, preferred_element_type=F32))


def _inproj_kernel(tiles_per_seq, x_ref, ng_ref, wn_ref, wt_ref, kgain_ref, qgain_ref, bd_ref, kx_ref,
                   qat_ref, ka_ref, vat_ref, ga_ref, sel_ref,
                   qbt_ref, kb_ref, vbt_ref, gb_ref, zc_ref, zd_ref, km_buf):
    tile = pl.program_id(0) % tiles_per_seq

    @pl.when(tile == 0)
    def _():
        km_buf[...] = jnp.zeros_like(km_buf)

    x = x_ref[...]
    ms = jnp.mean(x * x, axis=-1, keepdims=True)
    h = (x * lax.rsqrt(ms + EPS) * ng_ref[...]).astype(BF16)
    bd = bd_ref[...]
    kx = kx_ref[...]

    def proj(c0, width):
        return jnp.dot(h, wn_ref[:, c0:c0 + width], preferred_element_type=F32)

    def proj_t(r0):
        return lax.dot_general(wt_ref[r0:r0 + BRANCH, :], h, _NT, preferred_element_type=F32)

    def head_norm(z, row):
        return z * lax.rsqrt(_group_mean_sq(z, bd) + EPS) * kgain_ref[row:row + 1, :]

    def head_norm_t(zt, idx):
        parts = []
        for g in range(N_HEADS):
            part = zt[g * HEAD_DIM:(g + 1) * HEAD_DIM]
            parts.append(part * lax.rsqrt(jnp.mean(part * part, axis=0, keepdims=True) + EPS))
        return jnp.concatenate(parts, axis=0) * qgain_ref[idx]

    def store_keys(ref, kn):
        for hp in range(BRANCH // LANES):
            ref[:, 2 * hp * LANES:(2 * hp + 1) * LANES] = kn[:, hp * LANES:(hp + 1) * LANES].astype(BF16)
            ref[:, (2 * hp + 1) * LANES:(2 * hp + 2) * LANES] = kx

    qa = head_norm_t(proj_t(0), 0)
    ka = head_norm(proj(0, BRANCH), 0)
    store_keys(ka_ref, ka)
    vat_ref[0] = proj_t(BRANCH).astype(BF16)
    ga_ref[...] = proj(BRANCH, BRANCH)

    for blk in range(BLOCKS_PER_TILE):
        km_buf[pl.ds(tile * BLOCKS_PER_TILE + blk, 1), :] = jnp.mean(
            ka[blk * MOBA_BLOCK:(blk + 1) * MOBA_BLOCK], axis=0, keepdims=True)
    km = km_buf[...]
    n_blk = km.shape[0]
    blk = lax.broadcasted_iota(jnp.int32, (n_blk, ROW_TILE), 0)
    own = (tile * BLOCKS_PER_TILE
           + lax.broadcasted_iota(jnp.int32, (n_blk, ROW_TILE), 1) // MOBA_BLOCK)
    blk_f = blk.astype(F32)
    past = blk < own
    for head, qh in enumerate(_head_operands(qa)):
        gate = jnp.dot(km[:, (head // 2) * LANES:(head // 2 + 1) * LANES], qh,
                       precision=lax.Precision.HIGHEST, preferred_element_type=F32)
        keep = (past & _top_k_rows(jnp.where(past, gate, -jnp.inf), blk_f)) | (blk == own)
        sel_ref[0, head] = jnp.where(keep, 0.0, -MASK_BIAS).astype(BF16)

    qat_ref[0] = (qa * (HEAD_DIM ** -0.5 * LOG2E)).astype(BF16)
    qbt_ref[0] = (head_norm_t(proj_t(2 * BRANCH), 1) * (HEAD_DIM ** -0.5 * LOG2E)).astype(BF16)
    store_keys(kb_ref, head_norm(proj(2 * BRANCH, BRANCH), 1))
    vbt_ref[0] = proj_t(3 * BRANCH).astype(BF16)
    gb_ref[...] = proj(3 * BRANCH, BRANCH)

    zc_ref[...] = proj(4 * BRANCH, ZC_COLS)
    zd_ref[...] = proj(4 * BRANCH + ZC_COLS, ZD_COLS)


def _inproj(x2, ng, wn, wt, kgains, qgains, bd, kx, seq):
    n = x2.shape[0]
    nt = n // ROW_TILE
    per_seq = seq // ROW_TILE
    row = lambda i: (i, 0)
    const = lambda i: (0, 0)

    def nat(cols, dtype):
        return (jax.ShapeDtypeStruct((n, cols), dtype), pl.BlockSpec((ROW_TILE, cols), row))

    def tr(dtype):
        return (jax.ShapeDtypeStruct((nt, BRANCH, ROW_TILE), dtype),
                pl.BlockSpec((1, BRANCH, ROW_TILE), lambda i: (i, 0, 0)))

    n_blk = seq // MOBA_BLOCK
    choice = (jax.ShapeDtypeStruct((nt, N_HEADS, n_blk, ROW_TILE), BF16),
              pl.BlockSpec((1, N_HEADS, n_blk, ROW_TILE), lambda i: (i, 0, 0, 0)))
    outs = [tr(BF16), nat(2 * BRANCH, BF16), tr(BF16), nat(BRANCH, F32), choice,
            tr(BF16), nat(2 * BRANCH, BF16), tr(BF16), nat(BRANCH, F32),
            nat(ZC_COLS, F32), nat(ZD_COLS, F32)]
    return pl.pallas_call(
        functools.partial(_inproj_kernel, per_seq),
        grid=(nt,),
        in_specs=[pl.BlockSpec((ROW_TILE, D_MODEL), row),
                  pl.BlockSpec((1, D_MODEL), const),
                  pl.BlockSpec((D_MODEL, WN_COLS), const),
                  pl.BlockSpec((WT_ROWS, D_MODEL), const),
                  pl.BlockSpec((2, BRANCH), const),
                  pl.BlockSpec((2, BRANCH, 1), lambda i: (0, 0, 0)),
                  pl.BlockSpec((BRANCH, BRANCH), const),
                  pl.BlockSpec((ROW_TILE, LANES), lambda i: (i % per_seq, 0))],
        out_specs=[o[1] for o in outs],
        out_shape=[o[0] for o in outs],
        scratch_shapes=[pltpu.VMEM((n_blk, BRANCH), F32)],
        compiler_params=_params(1),
        name="inproj",
    )(x2, ng, wn, wt, kgains, qgains, bd, kx)


def _key_position_lanes(seq):
    pos = np.arange(seq)
    c, n = pos % MOBA_BLOCK, pos // MOBA_BLOCK
    kx = np.zeros((seq, LANES), np.float32)
    p = ALIBI_PIECES
    kx[:, 0:p] = (c // 16)[:, None]
    kx[:, p:2 * p] = (c % 16)[:, None]
    kx[:, 2 * p:3 * p] = n[:, None]
    kx[pos, SEL_LANE0 + n] = 1.0
    return jnp.asarray(kx, BF16)


def _query_alibi_rows(slopes):
    pieces, rest = [], LOG2E
    for _ in range(ALIBI_PIECES):
        piece = float(np.asarray(rest, dtype=BF16).astype(np.float64))
        pieces.append(piece)
        rest -= piece
    p = ALIBI_PIECES
    qx = np.zeros((len(slopes), LANES, ATT_TILE), np.float32)
    for h, slope in enumerate(slopes):
        for g, weight in enumerate((16.0, 1.0, float(MOBA_BLOCK))):
            qx[h, g * p:(g + 1) * p, :] = np.asarray([weight * slope * piece for piece in pieces])[:, None]
    return jnp.asarray(qx, F32)


def _flash_sweep(n_steps, group_of, score_bias, last_mask, k_ref, vt_ref, g_ref, o_ref,
                 qft_buf, s_bufs, p_bufs, a_bufs, smax_bufs, m_buf, acc_buf):
    def issue_scores(t, x):
        rows = pl.ds(pl.multiple_of(group_of(t) * ATT_TILE, ATT_TILE), ATT_TILE)
        for h in range(N_HEADS):
            pair = h // 2
            keys = k_ref[rows, 2 * pair * LANES:2 * (pair + 1) * LANES]
            s = jnp.dot(keys, qft_buf[h], preferred_element_type=F32)
            if score_bias is not None:
                s = s + score_bias(t)
            s_bufs[x][h] = s
            if score_bias is None:
                smax_bufs[x][h] = jnp.max(s, axis=0, keepdims=True)

    def softmax(x, mask):
        for h in range(N_HEADS):
            s = s_bufs[x][h]
            if mask is not None:
                s = mask(s)
            if mask is None and score_bias is None:
                group_max = smax_bufs[x][h]
            else:
                group_max = jnp.max(s, axis=0, keepdims=True)
            m_old = m_buf[h]
            m_new = jnp.maximum(m_old, group_max)
            m_buf[h] = m_new
            a_bufs[x][h] = jnp.exp2(m_old - m_new)
            p_bufs[x][h] = jnp.exp2(s - m_new).astype(BF16)

    ones = jnp.ones((SUBLANES, ATT_TILE), BF16)

    def fold_values(t, x):
        vt = vt_ref[group_of(t)]
        for h in range(N_HEADS):
            lhs = jnp.concatenate([vt[h * HEAD_DIM:(h + 1) * HEAD_DIM, :], ones], axis=0)
            acc_buf[h] = (a_bufs[x][h] * acc_buf[h]
                          + jnp.dot(lhs, p_bufs[x][h], preferred_element_type=F32))

    def regular_step(t, x):
        issue_scores(t + 1, 1 - x)
        fold_values(jnp.maximum(t - 1, 0), 1 - x)
        softmax(x, None)

    m_buf[...] = jnp.full(m_buf.shape, M_INIT, F32)
    acc_buf[...] = jnp.zeros(acc_buf.shape, F32)
    for x in range(2):
        a_bufs[x][...] = jnp.ones(a_bufs[x].shape, F32)
        p_bufs[x][...] = jnp.zeros(p_bufs[x].shape, BF16)

    n_regular = n_steps - 1
    odd = n_regular % 2

    @pl.when(odd == 1)
    def _():
        issue_scores(0, 1)
        regular_step(0, 1)

    @pl.when(odd == 0)
    def _():
        issue_scores(0, 0)

    def step_pair(u, _):
        t = odd + 2 * u
        regular_step(t, 0)
        regular_step(t + 1, 1)
        return 0

    lax.fori_loop(0, n_regular // 2, step_pair, 0)
    fold_values(jnp.maximum(n_steps - 2, 0), 1)
    softmax(0, last_mask)
    fold_values(n_steps - 1, 0)
    out_t = jnp.concatenate(
        [acc_buf[h, 0:HEAD_DIM, :] / acc_buf[h, HEAD_DIM:HEAD_DIM + 1, :] for h in range(N_HEADS)],
        axis=0)
    o_ref[...] = out_t.T * _silu(g_ref[...])


def _head_operands(qt):
    first = lax.broadcasted_iota(jnp.int32, (LANES, qt.shape[1]), 0) < HEAD_DIM
    zero = jnp.zeros((), qt.dtype)
    out = []
    for h in range(N_HEADS):
        pair = qt[(h // 2) * LANES:(h // 2 + 1) * LANES]
        out.append(jnp.where(first, pair, zero) if h % 2 == 0 else jnp.where(first, zero, pair))
    return out


def _attn_scratch():
    stat = pltpu.VMEM((N_HEADS, 1, ATT_TILE), F32)
    return ([pltpu.VMEM((N_HEADS, 2 * LANES, ATT_TILE), BF16)]
            + [pltpu.VMEM((N_HEADS, ATT_TILE, ATT_TILE), F32)] * 2
            + [pltpu.VMEM((N_HEADS, ATT_TILE, ATT_TILE), BF16)] * 2
            + [stat, stat]
            + [stat, stat]
            + [stat]
            + [pltpu.VMEM((N_HEADS, HEAD_DIM + SUBLANES, ATT_TILE), F32)])


def _attn_specs(seq):
    nq = seq // ATT_TILE
    q_tile = pl.BlockSpec((1, BRANCH, ATT_TILE), lambda b, i: (b * nq + i, 0, 0))
    keys = pl.BlockSpec((seq, 2 * BRANCH), lambda b, i: (b, 0))
    values = pl.BlockSpec((nq, BRANCH, ATT_TILE), lambda b, i: (b, 0, 0))
    gate = pl.BlockSpec((ATT_TILE, BRANCH), lambda b, i: (b * nq + i, 0))
    qx = pl.BlockSpec((N_HEADS, LANES, ATT_TILE), lambda b, i: (0, 0, 0),
                      pipeline_mode=pl.Buffered(1))
    return nq, q_tile, keys, values, gate, qx


def _top_k_rows(gate, row_f):
    sel = jnp.zeros(gate.shape, F32)
    for _ in range(MOBA_TOPK):
        top = jnp.max(gate, axis=0, keepdims=True)
        first = jnp.min(jnp.where(gate == top, row_f, 1e9), axis=0, keepdims=True)
        pick = row_f == first
        sel = jnp.where(pick, 1.0, sel)
        gate = jnp.where(pick, -jnp.inf, gate)
    return sel > 0.5


def _moba_kernel(qt_ref, k_ref, vt_ref, sel_ref, g_ref, qx_ref, o_ref, qft_buf, *bufs):
    i = pl.program_id(1)
    n_blk = sel_ref.shape[2]
    unused = jnp.zeros((LANES - SEL_LANE0 - n_blk, ATT_TILE), BF16)
    for h, qh in enumerate(_head_operands(qt_ref[0])):
        qft_buf[h] = jnp.concatenate(
            [qh, qx_ref[h, 0:SEL_LANE0, :].astype(BF16), sel_ref[0, h], unused], axis=0)

    causal = (lax.broadcasted_iota(jnp.int32, (ATT_TILE, ATT_TILE), 0)
              <= lax.broadcasted_iota(jnp.int32, (ATT_TILE, ATT_TILE), 1))
    _flash_sweep(i + 1, lambda t: t, None, lambda s: jnp.where(causal, s, NEG),
                 k_ref, vt_ref, g_ref, o_ref, qft_buf, bufs[0:2], bufs[2:4], bufs[4:6], bufs[6:8],
                 *bufs[8:])


def _moba(qt, k, vt, sel, g, qx, batch, seq):
    nq, q_tile, keys, values, gate, qx_spec = _attn_specs(seq)
    n_blk = seq // MOBA_BLOCK
    assert SEL_LANE0 + n_blk <= LANES and 3 * ALIBI_PIECES <= SEL_LANE0
    return pl.pallas_call(
        _moba_kernel,
        grid=(batch, nq),
        in_specs=[q_tile, keys, values,
                  pl.BlockSpec((1, N_HEADS, n_blk, ATT_TILE), lambda b, i: (b * nq + i, 0, 0, 0)),
                  gate, qx_spec],
        out_specs=gate,
        out_shape=jax.ShapeDtypeStruct(g.shape, F32),
        scratch_shapes=_attn_scratch(),
        compiler_params=_params(2),
        name="moba",
    )(qt, k, vt, sel, g, qx)


def _dilated_multiplicity_table():
    idx = np.arange(ATT_TILE)
    delta = (np.arange(DIL_GROUPS_BACK + 1)[:, None, None] * ATT_TILE
             + idx[None, None, :] - idx[None, :, None])
    mult = np.zeros(delta.shape, np.float64)
    for window, dil in DIL_PATTERNS:
        mult += (delta >= 0) & (delta <= window) & (delta % dil == 0)
    return jnp.asarray(np.where(mult > 0, np.log2(np.maximum(mult, 1.0)), NEG), F32)


def _dilated_kernel(qt_ref, k_ref, vt_ref, t_ref, g_ref, qx_ref, o_ref, qft_buf, *bufs):
    i = pl.program_id(1)
    for h, qh in enumerate(_head_operands(qt_ref[0])):
        qft_buf[h] = jnp.concatenate([qh, qx_ref[h].astype(BF16)], axis=0)

    _flash_sweep(jnp.minimum(i, DIL_GROUPS_BACK) + 1, lambda t: i - t, lambda t: t_ref[t], None,
                 k_ref, vt_ref, g_ref, o_ref, qft_buf, bufs[0:2], bufs[2:4], bufs[4:6], bufs[6:8],
                 *bufs[8:])


def _dilated(qt, k, vt, table, g, qx, batch, seq):
    nq, q_tile, keys, values, gate, qx_spec = _attn_specs(seq)
    return pl.pallas_call(
        _dilated_kernel,
        grid=(batch, nq),
        in_specs=[q_tile, keys, values,
                  pl.BlockSpec(table.shape, lambda b, i: (0, 0, 0), pipeline_mode=pl.Buffered(1)),
                  gate, qx_spec],
        out_specs=gate,
        out_shape=jax.ShapeDtypeStruct(g.shape, F32),
        scratch_shapes=_attn_scratch(),
        compiler_params=_params(2),
        name="dilated",
    )(qt, k, vt, table, g, qx)


def _conv_module(z, halo, has_history, w_ref, b_ref, lng_ref, lnb_ref, pw_ref, pwb_ref, u_buf):
    def glu(z):
        return z[:, 0:BRANCH] * jax.nn.sigmoid(z[:, BRANCH:2 * BRANCH])

    u_buf[0, 0:CONV_HALO, :] = jnp.where(has_history, glu(halo), 0.0)
    u_buf[0, CONV_HALO:, :] = glu(z)
    shifted = CONV_HALO + ROW_TILE - SUBLANES
    for phase in range(1, SUBLANES):
        u_buf[phase, 0:shifted, :] = u_buf[0, phase:phase + shifted, :]
    acc = jnp.zeros((ROW_TILE, BRANCH), F32) + b_ref[...]
    first = CONV_HALO - (CONV_WIDTH - 1)
    for tap in range(CONV_WIDTH):
        phase, start = (first + tap) % SUBLANES, (first + tap) // SUBLANES * SUBLANES
        acc = acc + w_ref[tap:tap + 1, :] * u_buf[phase, start:start + ROW_TILE, :]
    mu = jnp.mean(acc, axis=-1, keepdims=True)
    cen = acc - mu
    var = jnp.mean(cen * cen, axis=-1, keepdims=True)
    un = cen * lax.rsqrt(var + EPS) * lng_ref[...] + lnb_ref[...]
    y = jnp.dot(_silu(un).astype(BF16), pw_ref[...], preferred_element_type=F32) + pwb_ref[...]
    return y * _silu(z[:, 2 * BRANCH:3 * BRANCH])


def _gla_sum_matrices():
    c = GLA_CHUNK
    i = np.arange(c)[:, None]
    t = np.arange(c)[None, :]
    mats = [t <= i]
    for l in range(GLA_LEVELS):
        h = (c // 2) >> l
        mid = (i // (2 * h)) * (2 * h) + h
        later = (i & h) != 0
        mats.append((later & (t >= mid) & (t <= i)) | (~later & (t > i) & (t < mid)))
    return jnp.asarray(np.concatenate(mats, axis=0), BF16)


def _gla_tile(z_ref, sums_ref, wg_ref, bg_ref, gn_ref, bd_ref, o_ref, state_ref):
    c = GLA_CHUNK
    nh = GLA_HEADS
    kw = nh * GLA_DK
    vw = nh * GLA_DV

    row = lax.broadcasted_iota(jnp.int32, (c, kw), 0)
    qi = lax.broadcasted_iota(jnp.int32, (c, nh * c), 0)
    kj = lax.broadcasted_iota(jnp.int32, (c, nh * c), 1) % c
    level_mask = [(qi >> (GLA_LEVELS - l)) == (kj >> (GLA_LEVELS - l)) for l in range(GLA_LEVELS)]
    diag_mask = qi == kj
    k_head = (lax.broadcasted_iota(jnp.int32, (nh * c, kw), 0) // c
              == lax.broadcasted_iota(jnp.int32, (nh * c, kw), 1) // GLA_DK)
    v_head = (lax.broadcasted_iota(jnp.int32, (nh * c, vw), 0) // c
              == lax.broadcasted_iota(jnp.int32, (nh * c, vw), 1) // GLA_DV)
    s_head = (lax.broadcasted_iota(jnp.int32, (vw, kw), 0) // GLA_DV
              == lax.broadcasted_iota(jnp.int32, (vw, kw), 1) // GLA_DK)

    k_head_bf = jnp.where(k_head, 1.0, 0.0).astype(BF16)

    def per_head_keys(kt):
        return jnp.concatenate([kt.astype(BF16)] * nh, axis=0) * k_head_bf

    def chunk_of(b, rows):
        q = z_ref[b, rows, 0:kw] * GLA_DK ** -0.5
        k = z_ref[b, rows, kw:2 * kw]
        v = z_ref[b, rows, 2 * kw:2 * kw + vw]
        gd = z_ref[b, rows, 2 * kw + vw:2 * kw + 2 * vw]
        lr = z_ref[b, rows, 2 * kw + 2 * vw:2 * kw + 2 * vw + LR_PAD]

        g = jnp.dot(lr.astype(BF16), wg_ref[...], preferred_element_type=F32) + bg_ref[...]
        la = (jnp.minimum(g, 0.0) - jnp.log(1.0 + jnp.exp(-jnp.abs(g)))) / GLA_TAU
        a1 = la.astype(BF16)
        a2 = (la - a1.astype(F32)).astype(BF16)
        parts = jnp.dot(sums_ref[...], jnp.concatenate([a1, a2], axis=1),
                        preferred_element_type=F32)
        sums = parts[:, 0:kw] + parts[:, kw:2 * kw]
        bc = sums[0:c]

        attn = jnp.where(diag_mask,
                         lax.dot_general(q.astype(BF16), per_head_keys(k), _NT,
                                         preferred_element_type=F32), 0.0)
        for l in range(GLA_LEVELS):
            later = (row & ((c // 2) >> l)) != 0
            scaled = jnp.where(later, q, k) * jnp.exp(sums[(1 + l) * c:(2 + l) * c])
            qt = jnp.where(later, scaled, 0.0).astype(BF16)
            a = lax.dot_general(qt, per_head_keys(jnp.where(later, 0.0, scaled)), _NT,
                                preferred_element_type=F32)
            attn = attn + (a if l == 0 else jnp.where(level_mask[l], a, 0.0))

        vb = v.astype(BF16)
        v_stack = jnp.where(v_head, jnp.concatenate([vb] * nh, axis=0), jnp.zeros((), BF16))
        o = jnp.dot(attn.astype(BF16), v_stack, preferred_element_type=F32)

        state = state_ref[b]
        o = o + lax.dot_general((q * jnp.exp(bc)).astype(BF16), state.astype(BF16), _NT,
                                preferred_element_type=F32)
        b_last = bc[c - 1:c, :]
        k_dec = (k * jnp.exp(b_last - bc)).astype(BF16)
        upd = lax.dot_general(vb, k_dec, _TN, preferred_element_type=F32)
        state_ref[b] = state * jnp.exp(b_last) + jnp.where(s_head, upd, 0.0)

        on = o * lax.rsqrt(_group_mean_sq(o, bd_ref[...]) + EPS) * gn_ref[...]
        o_ref[b, rows, :] = on * _silu(gd)

    for ci in range(ROW_TILE // c):
        for b in range(z_ref.shape[0]):
            chunk_of(b, pl.ds(ci * c, c))


def _tail_kernel(x_ref, ya_ref, yb_ref, zc_ref, halo_ref, zd_ref,
                 cw_ref, cb_ref, lng_ref, lnb_ref, pw_ref, pwb_ref,
                 sums_ref, wg_ref, bg_ref, gn_ref, bd_ref, wo_ref,
                 o_ref, u_buf, yd_buf, state_ref):
    j = pl.program_id(0)

    @pl.when(j == 0)
    def _():
        state_ref[...] = jnp.zeros_like(state_ref)

    def project(y, g):
        return jnp.dot(y.astype(BF16), wo_ref[g * BRANCH:(g + 1) * BRANCH, :],
                       preferred_element_type=F32)

    _gla_tile(zd_ref, sums_ref, wg_ref, bg_ref, gn_ref, bd_ref, yd_buf, state_ref)
    for b in range(x_ref.shape[0]):
        yc = _conv_module(zc_ref[b], halo_ref[b], j > 0, cw_ref, cb_ref, lng_ref, lnb_ref,
                          pw_ref, pwb_ref, u_buf.at[b])
        o_ref[b] = (x_ref[b] + project(ya_ref[b], 0) + project(yb_ref[b], 1) + project(yc, 2)
                    + project(yd_buf[b], 3))


def _tail(x2, ya, yb, zc, zd, conv_consts, gla_consts, w_out, batch, seq):
    per = ROW_TILE // CONV_HALO
    tile = lambda cols: pl.BlockSpec((batch, ROW_TILE, cols), lambda j: (0, j, 0))
    whole = lambda a: pl.BlockSpec(a.shape, lambda j: (0,) * a.ndim, pipeline_mode=pl.Buffered(1))
    by_seq = lambda a: a.reshape(batch, seq, a.shape[-1])
    consts = (*conv_consts, *gla_consts, w_out)
    out = pl.pallas_call(
        _tail_kernel,
        grid=(seq // ROW_TILE,),
        in_specs=[tile(D_MODEL), tile(BRANCH), tile(BRANCH), tile(ZC_COLS),
                  pl.BlockSpec((batch, CONV_HALO, ZC_COLS),
                               lambda j: (0, jnp.maximum(j * per - 1, 0), 0)),
                  tile(ZD_COLS)] + [whole(a) for a in consts],
        out_specs=tile(D_MODEL),
        out_shape=jax.ShapeDtypeStruct((batch, seq, D_MODEL), F32),
        scratch_shapes=[pltpu.VMEM((batch, SUBLANES, CONV_HALO + ROW_TILE, BRANCH), F32),
                        pltpu.VMEM((batch, ROW_TILE, BRANCH), F32),
                        pltpu.VMEM((batch, GLA_HEADS * GLA_DV, GLA_HEADS * GLA_DK), F32)],
        compiler_params=_params(1),
        name="tail",
    )(by_seq(x2), by_seq(ya), by_seq(yb), by_seq(zc), by_seq(zc), by_seq(zd), *consts)
    return out.reshape(batch * seq, D_MODEL)


def _pack_w_in(w_in):
    col = lambda j: w_in[:, j * BRANCH:(j + 1) * BRANCH]
    gla0 = 11 * BRANCH
    qkv = 2 * GLA_HEADS * GLA_DK + BRANCH
    pad = jnp.zeros((w_in.shape[0], LR_PAD - GLA_RANK), w_in.dtype)
    wn = jnp.concatenate([col(1), col(3), col(5), col(7), w_in[:, 8 * BRANCH:gla0 + qkv],
                          w_in[:, gla0 + qkv + GLA_RANK:], w_in[:, gla0 + qkv:gla0 + qkv + GLA_RANK], pad],
                         axis=1)
    wt = jnp.concatenate([col(0), col(2), col(4), col(6)], axis=1).T
    return wn.astype(BF16), wt.astype(BF16)


def _layer(x2, batch, seq, consts, norm_g, w_in, q_gain_a, k_gain_a, q_gain_b, k_gain_b, conv_w, conv_b,
           conv_ln_g, conv_ln_b, conv_pw_w, conv_pw_b, gla_gate_w, gla_gate_b, gla_norm_g, w_out):
    bd, kx, qx_moba, qx_dil, dil_table, gla_sums = consts
    kgains = jnp.stack([jnp.tile(k_gain_a, N_HEADS), jnp.tile(k_gain_b, N_HEADS)])
    qgains = jnp.stack([jnp.tile(q_gain_a, N_HEADS), jnp.tile(q_gain_b, N_HEADS)])[:, :, None]
    wn, wt = _pack_w_in(w_in)
    (qat, ka, vat, ga, sel, qbt, kb, vbt, gb, zc, zd) = _inproj(
        x2, norm_g[None, :], wn, wt, kgains, qgains, bd, kx, seq)
    ya = _moba(qat, ka, vat, sel, ga, qx_moba, batch, seq)
    yb = _dilated(qbt, kb, vbt, dil_table, gb, qx_dil, batch, seq)
    wg = jnp.concatenate([gla_gate_w, jnp.zeros((LR_PAD - GLA_RANK, gla_gate_w.shape[1]), F32)],
                         axis=0).astype(BF16)
    conv_consts = (conv_w, conv_b[None, :], conv_ln_g[None, :], conv_ln_b[None, :],
                   conv_pw_w.astype(BF16), conv_pw_b[None, :])
    gla_consts = (gla_sums, wg, gla_gate_b[None, :], jnp.tile(gla_norm_g, GLA_HEADS)[None, :], bd)
    return _tail(x2, ya, yb, zc, zd, conv_consts, gla_consts, w_out.astype(BF16), batch, seq)


def kernel(x, norm_g, w_in, q_gain_a, k_gain_a, q_gain_b, k_gain_b, conv_w, conv_b, conv_ln_g, conv_ln_b,
           conv_pw_w, conv_pw_b, gla_gate_w, gla_gate_b, gla_norm_g, w_out):
    batch, seq, d = x.shape
    assert d == D_MODEL and seq % ROW_TILE == 0 and ROW_TILE == ATT_TILE
    group = np.arange(BRANCH) // HEAD_DIM
    bd = jnp.asarray((group[:, None] == group[None, :]) / HEAD_DIM, BF16)
    heads = np.arange(N_HEADS)
    consts = (bd, _key_position_lanes(seq),
              _query_alibi_rows(2.0 ** -(1.0 + 2 * heads)),
              _query_alibi_rows(2.0 ** -(2.0 + 2 * heads)),
              _dilated_multiplicity_table(), _gla_sum_matrices())
    x2 = x.reshape(batch * seq, d)
    params = (norm_g, w_in, q_gain_a, k_gain_a, q_gain_b, k_gain_b, conv_w, conv_b, conv_ln_g,
              conv_ln_b, conv_pw_w, conv_pw_b, gla_gate_w, gla_gate_b, gla_norm_g, w_out)
    for layer in range(norm_g.shape[0]):
        x2 = _layer(x2, batch, seq, consts, *(p[layer] for p in params))
    return x2.reshape(batch, seq, d)
```

```python
import functools

import numpy as np
import jax
import jax.numpy as jnp
from jax import lax
from jax.experimental import pallas as pl
from jax.experimental.pallas import tpu as pltpu

F32 = jnp.float32
BF16 = jnp.bfloat16

D_MODEL = 1024
BRANCH = 256
HEAD_DIM = 64
N_HEADS = BRANCH // HEAD_DIM
MOBA_BLOCK = 256
MOBA_TOPK = 3
DIL_PATTERNS = ((128, 1), (512, 4), (2048, 16))
CONV_WIDTH = 31
GLA_HEADS = 4
GLA_DK = 32
GLA_DV = 64
GLA_RANK = 16
GLA_TAU = 16.0
EPS = 1e-6
NEG = -1e30
LOG2E = 1.4426950408889634

LANES = 128
SUBLANES = 8
ROW_TILE = 512
ATT_TILE = 512
BLOCKS_PER_TILE = ATT_TILE // MOBA_BLOCK
DIL_GROUPS_BACK = max(w for w, _ in DIL_PATTERNS) // ATT_TILE
ALIBI_PIECES = 4
SEL_LANE0 = 16
MASK_BIAS = 2.0 ** 100
M_INIT = -1e29
GLA_CHUNK = 128
GLA_LEVELS = 7
CONV_HALO = 32
LR_PAD = 128
VMEM_LIMIT = 56 * 1024 * 1024

WT_ROWS = 4 * BRANCH
ZC_COLS = 3 * BRANCH
ZD_COLS = 2 * GLA_HEADS * GLA_DK + 2 * BRANCH + LR_PAD
WN_COLS = 4 * BRANCH + ZC_COLS + ZD_COLS

_NT = (((1,), (1,)), ((), ()))
_TN = (((0,), (0,)), ((), ()))


def _params(n_grid):
    return pltpu.CompilerParams(dimension_semantics=("arbitrary",) * n_grid,
                                vmem_limit_bytes=VMEM_LIMIT)


def _silu(x):
    return x * jax.nn.sigmoid(x)


def _group_mean_sq(z, bd):
    z2 = z * z
    hi = z2.astype(BF16)
    lo = (z2 - hi.astype(F32)).astype(BF16)
    return (jnp.dot(hi, bd, preferred_element_type=F32)
            + jnp.dot(lo, bd, preferred_element_type=F32))


def _inproj_kernel(tiles_per_seq, x_ref, ng_ref, wn_ref, wt_ref, kgain_ref, qgain_ref, bd_ref, kx_ref,
                   qat_ref, ka_ref, vat_ref, ga_ref, sel_ref,
                   qbt_ref, kb_ref, vbt_ref, gb_ref, zc_ref, zd_ref, km_buf):
    tile = pl.program_id(0) % tiles_per_seq

    @pl.when(tile == 0)
    def _():
        km_buf[...] = jnp.zeros_like(km_buf)

    x = x_ref[...]
    ms = jnp.mean(x * x, axis=-1, keepdims=True)
    h = (x * lax.rsqrt(ms + EPS) * ng_ref[...]).astype(BF16)
    bd = bd_ref[...]
    kx = kx_ref[...]

    def proj(c0, width):
        return jnp.dot(h, wn_ref[:, c0:c0 + width], preferred_element_type=F32)

    def proj_t(r0):
        return lax.dot_general(wt_ref[r0:r0 + BRANCH, :], h, _NT, preferred_element_type=F32)

    def head_norm(z, row):
        return z * lax.rsqrt(_group_mean_sq(z, bd) + EPS) * kgain_ref[row:row + 1, :]

    def head_norm_t(zt, idx):
        parts = []
        for g in range(N_HEADS):
            part = zt[g * HEAD_DIM:(g + 1) * HEAD_DIM]
            parts.append(part * lax.rsqrt(jnp.mean(part * part, axis=0, keepdims=True) + EPS))
        return jnp.concatenate(parts, axis=0) * qgain_ref[idx]

    def store_keys(ref, kn):
        for hp in range(BRANCH // LANES):
            ref[:, 2 * hp * LANES:(2 * hp + 1) * LANES] = kn[:, hp * LANES:(hp + 1) * LANES].astype(BF16)
            ref[:, (2 * hp + 1) * LANES:(2 * hp + 2) * LANES] = kx

    qa = head_norm_t(proj_t(0), 0)
    ka = head_norm(proj(0, BRANCH), 0)
    store_keys(ka_ref, ka)
    vat_ref[0] = proj_t(BRANCH).astype(BF16)
    ga_ref[...] = proj(BRANCH, BRANCH)

    for blk in range(BLOCKS_PER_TILE):
        km_buf[pl.ds(tile * BLOCKS_PER_TILE + blk, 1), :] = jnp.mean(
            ka[blk * MOBA_BLOCK:(blk + 1) * MOBA_BLOCK], axis=0, keepdims=True)
    km = km_buf[...]
    n_blk = km.shape[0]
    blk = lax.broadcasted_iota(jnp.int32, (n_blk, ROW_TILE), 0)
    own = (tile * BLOCKS_PER_TILE
           + lax.broadcasted_iota(jnp.int32, (n_blk, ROW_TILE), 1) // MOBA_BLOCK)
    blk_f = blk.astype(F32)
    past = blk < own
    for head, qh in enumerate(_head_operands(qa)):
        gate = jnp.dot(km[:, (head // 2) * LANES:(head // 2 + 1) * LANES], qh,
                       precision=lax.Precision.HIGHEST, preferred_element_type=F32)
        keep = (past & _top_k_rows(jnp.where(past, gate, -jnp.inf), blk_f)) | (blk == own)
        sel_ref[0, head] = jnp.where(keep, 0.0, -MASK_BIAS).astype(BF16)

    qat_ref[0] = (qa * (HEAD_DIM ** -0.5 * LOG2E)).astype(BF16)
    qbt_ref[0] = (head_norm_t(proj_t(2 * BRANCH), 1) * (HEAD_DIM ** -0.5 * LOG2E)).astype(BF16)
    store_keys(kb_ref, head_norm(proj(2 * BRANCH, BRANCH), 1))
    vbt_ref[0] = proj_t(3 * BRANCH).astype(BF16)
    gb_ref[...] = proj(3 * BRANCH, BRANCH)

    zc_ref[...] = proj(4 * BRANCH, ZC_COLS)
    zd_ref[...] = proj(4 * BRANCH + ZC_COLS, ZD_COLS)


def _inproj(x2, ng, wn, wt, kgains, qgains, bd, kx, seq):
    n = x2.shape[0]
    nt = n // ROW_TILE
    per_seq = seq // ROW_TILE
    row = lambda i: (i, 0)
    const = lambda i: (0, 0)

    def nat(cols, dtype):
        return (jax.ShapeDtypeStruct((n, cols), dtype), pl.BlockSpec((ROW_TILE, cols), row))

    def tr(dtype):
        return (jax.ShapeDtypeStruct((nt, BRANCH, ROW_TILE), dtype),
                pl.BlockSpec((1, BRANCH, ROW_TILE), lambda i: (i, 0, 0)))

    n_blk = seq // MOBA_BLOCK
    choice = (jax.ShapeDtypeStruct((nt, N_HEADS, n_blk, ROW_TILE), BF16),
              pl.BlockSpec((1, N_HEADS, n_blk, ROW_TILE), lambda i: (i, 0, 0, 0)))
    outs = [tr(BF16), nat(2 * BRANCH, BF16), tr(BF16), nat(BRANCH, F32), choice,
            tr(BF16), nat(2 * BRANCH, BF16), tr(BF16), nat(BRANCH, F32),
            nat(ZC_COLS, F32), nat(ZD_COLS, F32)]
    return pl.pallas_call(
        functools.partial(_inproj_kernel, per_seq),
        grid=(nt,),
        in_specs=[pl.BlockSpec((ROW_TILE, D_MODEL), row),
                  pl.BlockSpec((1, D_MODEL), const),
                  pl.BlockSpec((D_MODEL, WN_COLS), const),
                  pl.BlockSpec((WT_ROWS, D_MODEL), const),
                  pl.BlockSpec((2, BRANCH), const),
                  pl.BlockSpec((2, BRANCH, 1), lambda i: (0, 0, 0)),
                  pl.BlockSpec((BRANCH, BRANCH), const),
                  pl.BlockSpec((ROW_TILE, LANES), lambda i: (i % per_seq, 0))],
        out_specs=[o[1] for o in outs],
        out_shape=[o[0] for o in outs],
        scratch_shapes=[pltpu.VMEM((n_blk, BRANCH), F32)],
        compiler_params=_params(1),
        name="inproj",
    )(x2, ng, wn, wt, kgains, qgains, bd, kx)


def _key_position_lanes(seq):
    pos = np.arange(seq)
    c, n = pos % MOBA_BLOCK, pos // MOBA_BLOCK
    kx = np.zeros((seq, LANES), np.float32)
    p = ALIBI_PIECES
    kx[:, 0:p] = (c // 16)[:, None]
    kx[:, p:2 * p] = (c % 16)[:, None]
    kx[:, 2 * p:3 * p] = n[:, None]
    kx[pos, SEL_LANE0 + n] = 1.0
    return jnp.asarray(kx, BF16)


def _query_alibi_rows(slopes):
    pieces, rest = [], LOG2E
    for _ in range(ALIBI_PIECES):
        piece = float(np.asarray(rest, dtype=BF16).astype(np.float64))
        pieces.append(piece)
        rest -= piece
    p = ALIBI_PIECES
    qx = np.zeros((len(slopes), LANES, ATT_TILE), np.float32)
    for h, slope in enumerate(slopes):
        for g, weight in enumerate((16.0, 1.0, float(MOBA_BLOCK))):
            qx[h, g * p:(g + 1) * p, :] = np.asarray([weight * slope * piece for piece in pieces])[:, None]
    return jnp.asarray(qx, F32)


class _SweepOps:
    def __init__(self, k_ref, vt_ref, qft_buf, s_bufs, p_bufs, a_bufs, smax_bufs, m_buf, acc_buf):
        self.k_ref, self.vt_ref, self.qft_buf = k_ref, vt_ref, qft_buf
        self.s_bufs, self.p_bufs, self.a_bufs, self.smax_bufs = s_bufs, p_bufs, a_bufs, smax_bufs
        self.m_buf, self.acc_buf = m_buf, acc_buf

    def init(self):
        self.m_buf[...] = jnp.full(self.m_buf.shape, M_INIT, F32)
        self.acc_buf[...] = jnp.zeros(self.acc_buf.shape, F32)

    def issue_scores(self, group, x, bias=None):
        rows = pl.ds(pl.multiple_of(group * ATT_TILE, ATT_TILE), ATT_TILE)
        for h in range(N_HEADS):
            pair = h // 2
            keys = self.k_ref[rows, 2 * pair * LANES:2 * (pair + 1) * LANES]
            s = jnp.dot(keys, self.qft_buf[h], preferred_element_type=F32)
            if bias is None:
                self.smax_bufs[x][h] = jnp.max(s, axis=0, keepdims=True)
            else:
                s = s + bias
            self.s_bufs[x][h] = s

    def softmax(self, x, mask=None, issued_max=True):
        for h in range(N_HEADS):
            s = self.s_bufs[x][h]
            if mask is not None:
                s = mask(s)
            if mask is None and issued_max:
                group_max = self.smax_bufs[x][h]
            else:
                group_max = jnp.max(s, axis=0, keepdims=True)
            m_old = self.m_buf[h]
            m_new = jnp.maximum(m_old, group_max)
            self.m_buf[h] = m_new
            self.a_bufs[x][h] = jnp.exp2(m_old - m_new)
            self.p_bufs[x][h] = jnp.exp2(s - m_new).astype(BF16)

    def fold_values(self, group, x):
        ones = jnp.ones((SUBLANES, ATT_TILE), BF16)
        vt = self.vt_ref[group]
        for h in range(N_HEADS):
            lhs = jnp.concatenate([vt[h * HEAD_DIM:(h + 1) * HEAD_DIM, :], ones], axis=0)
            self.acc_buf[h] = (self.a_bufs[x][h] * self.acc_buf[h]
                               + jnp.dot(lhs, self.p_bufs[x][h], preferred_element_type=F32))

    def finish(self, g_ref, o_ref):
        acc = self.acc_buf
        out_t = jnp.concatenate(
            [acc[h, 0:HEAD_DIM, :] / acc[h, HEAD_DIM:HEAD_DIM + 1, :] for h in range(N_HEADS)], axis=0)
        o_ref[...] = out_t.T * _silu(g_ref[...])


def _flash_sweep(ops, n_steps, last_mask, g_ref, o_ref):
    def regular_step(t, x):
        ops.issue_scores(t + 1, 1 - x)
        ops.fold_values(jnp.maximum(t - 1, 0), 1 - x)
        ops.softmax(x)

    ops.init()
    for x in range(2):
        ops.a_bufs[x][...] = jnp.ones(ops.a_bufs[x].shape, F32)
        ops.p_bufs[x][...] = jnp.zeros(ops.p_bufs[x].shape, BF16)

    n_regular = n_steps - 1
    odd = n_regular % 2

    @pl.when(odd == 1)
    def _():
        ops.issue_scores(0, 1)
        regular_step(0, 1)

    @pl.when(odd == 0)
    def _():
        ops.issue_scores(0, 0)

    def step_pair(u, _):
        t = odd + 2 * u
        regular_step(t, 0)
        regular_step(t + 1, 1)
        return 0

    lax.fori_loop(0, n_regular // 2, step_pair, 0)
    ops.fold_values(jnp.maximum(n_steps - 2, 0), 1)
    ops.softmax(0, last_mask)
    ops.fold_values(n_steps - 1, 0)
    ops.finish(g_ref, o_ref)


def _head_operands(qt):
    first = lax.broadcasted_iota(jnp.int32, (LANES, qt.shape[1]), 0) < HEAD_DIM
    zero = jnp.zeros((), qt.dtype)
    out = []
    for h in range(N_HEADS):
        pair = qt[(h // 2) * LANES:(h // 2 + 1) * LANES]
        out.append(jnp.where(first, pair, zero) if h % 2 == 0 else jnp.where(first, zero, pair))
    return out


def _attn_scratch():
    stat = pltpu.VMEM((N_HEADS, 1, ATT_TILE), F32)
    return ([pltpu.VMEM((N_HEADS, 2 * LANES, ATT_TILE), BF16)]
            + [pltpu.VMEM((N_HEADS, ATT_TILE, ATT_TILE), F32)] * 2
            + [pltpu.VMEM((N_HEADS, ATT_TILE, ATT_TILE), BF16)] * 2
            + [stat, stat]
            + [stat, stat]
            + [stat]
            + [pltpu.VMEM((N_HEADS, HEAD_DIM + SUBLANES, ATT_TILE), F32)])


def _attn_specs(seq):
    nq = seq // ATT_TILE
    q_tile = pl.BlockSpec((1, BRANCH, ATT_TILE), lambda b, i: (b * nq + i, 0, 0))
    keys = pl.BlockSpec((seq, 2 * BRANCH), lambda b, i: (b, 0))
    values = pl.BlockSpec((nq, BRANCH, ATT_TILE), lambda b, i: (b, 0, 0))
    gate = pl.BlockSpec((ATT_TILE, BRANCH), lambda b, i: (b * nq + i, 0))
    qx = pl.BlockSpec((N_HEADS, LANES, ATT_TILE), lambda b, i: (0, 0, 0),
                      pipeline_mode=pl.Buffered(1))
    return nq, q_tile, keys, values, gate, qx


def _top_k_rows(gate, row_f):
    sel = jnp.zeros(gate.shape, F32)
    for _ in range(MOBA_TOPK):
        top = jnp.max(gate, axis=0, keepdims=True)
        first = jnp.min(jnp.where(gate == top, row_f, 1e9), axis=0, keepdims=True)
        pick = row_f == first
        sel = jnp.where(pick, 1.0, sel)
        gate = jnp.where(pick, -jnp.inf, gate)
    return sel > 0.5


def _moba_kernel(qt_ref, k_ref, vt_ref, sel_ref, g_ref, qx_ref, o_ref, qft_buf, *bufs):
    i = pl.program_id(1)
    n_blk = sel_ref.shape[2]
    unused = jnp.zeros((LANES - SEL_LANE0 - n_blk, ATT_TILE), BF16)
    for h, qh in enumerate(_head_operands(qt_ref[0])):
        qft_buf[h] = jnp.concatenate(
            [qh, qx_ref[h, 0:SEL_LANE0, :].astype(BF16), sel_ref[0, h], unused], axis=0)

    causal = (lax.broadcasted_iota(jnp.int32, (ATT_TILE, ATT_TILE), 0)
              <= lax.broadcasted_iota(jnp.int32, (ATT_TILE, ATT_TILE), 1))
    ops = _SweepOps(k_ref, vt_ref, qft_buf, bufs[0:2], bufs[2:4], bufs[4:6], bufs[6:8], *bufs[8:])
    _flash_sweep(ops, i + 1, lambda s: jnp.where(causal, s, NEG), g_ref, o_ref)


def _moba(qt, k, vt, sel, g, qx, batch, seq):
    nq, q_tile, keys, values, gate, qx_spec = _attn_specs(seq)
    n_blk = seq // MOBA_BLOCK
    assert SEL_LANE0 + n_blk <= LANES and 3 * ALIBI_PIECES <= SEL_LANE0
    return pl.pallas_call(
        _moba_kernel,
        grid=(batch, nq),
        in_specs=[q_tile, keys, values,
                  pl.BlockSpec((1, N_HEADS, n_blk, ATT_TILE), lambda b, i: (b * nq + i, 0, 0, 0)),
                  gate, qx_spec],
        out_specs=gate,
        out_shape=jax.ShapeDtypeStruct(g.shape, F32),
        scratch_shapes=_attn_scratch(),
        compiler_params=_params(2),
        name="moba",
    )(qt, k, vt, sel, g, qx)


def _dilated_multiplicity_table():
    idx = np.arange(ATT_TILE)
    delta = (np.arange(DIL_GROUPS_BACK + 1)[:, None, None] * ATT_TILE
             + idx[None, None, :] - idx[None, :, None])
    mult = np.zeros(delta.shape, np.float64)
    for window, dil in DIL_PATTERNS:
        mult += (delta >= 0) & (delta <= window) & (delta % dil == 0)
    table = np.where(mult > 0, np.log2(np.maximum(mult, 1.0)), NEG)
    return jnp.asarray(np.concatenate([table, np.full_like(table[:1], NEG)]), F32)


def _dilated_kernel(qt_ref, qt_next_ref, k_ref, vt_ref, t_ref, g_ref, qx_ref, o_ref, qft_buf, *bufs):
    i = pl.program_id(1)
    n_steps = DIL_GROUPS_BACK + 1
    ops = _SweepOps(k_ref, vt_ref, qft_buf, bufs[0:2], bufs[2:4], bufs[4:6], bufs[6:8], *bufs[8:])

    def group_and_table(tile, t):
        inside = t <= tile
        return jnp.where(inside, tile - t, 0), t_ref[jnp.where(inside, t, n_steps)]

    def prepare(tile, q_ref):
        for h, qh in enumerate(_head_operands(q_ref[0])):
            qft_buf[h] = jnp.concatenate([qh, qx_ref[h].astype(BF16)], axis=0)
        ops.issue_scores(tile, 0, t_ref[0])

    ops.init()

    @pl.when(i == 0)
    def _():
        prepare(i, qt_ref)

    for t in range(n_steps):
        x = t % 2
        if t + 1 < n_steps:
            group, table = group_and_table(i, t + 1)
            ops.issue_scores(group, 1 - x, table)
        if t >= 1:
            ops.fold_values(group_and_table(i, t - 1)[0], 1 - x)
        ops.softmax(x, issued_max=False)
    prepare(jnp.minimum(i + 1, pl.num_programs(1) - 1), qt_next_ref)
    ops.fold_values(group_and_table(i, n_steps - 1)[0], (n_steps - 1) % 2)
    ops.finish(g_ref, o_ref)


def _dilated(qt, k, vt, table, g, qx, batch, seq):
    nq, q_tile, keys, values, gate, qx_spec = _attn_specs(seq)
    assert DIL_GROUPS_BACK % 2 == 0
    q_next = pl.BlockSpec((1, BRANCH, ATT_TILE),
                          lambda b, i: (b * nq + jnp.minimum(i + 1, nq - 1), 0, 0))
    return pl.pallas_call(
        _dilated_kernel,
        grid=(batch, nq),
        in_specs=[q_tile, q_next, keys, values,
                  pl.BlockSpec(table.shape, lambda b, i: (0, 0, 0), pipeline_mode=pl.Buffered(1)),
                  gate, qx_spec],
        out_specs=gate,
        out_shape=jax.ShapeDtypeStruct(g.shape, F32),
        scratch_shapes=_attn_scratch(),
        compiler_params=_params(2),
        name="dilated",
    )(qt, qt, k, vt, table, g, qx)


def _conv_module(z, halo, has_history, w_ref, b_ref, lng_ref, lnb_ref, pw_ref, pwb_ref, u_buf):
    def glu(z):
        return z[:, 0:BRANCH] * jax.nn.sigmoid(z[:, BRANCH:2 * BRANCH])

    u_buf[0, 0:CONV_HALO, :] = jnp.where(has_history, glu(halo), 0.0)
    u_buf[0, CONV_HALO:, :] = glu(z)
    shifted = CONV_HALO + ROW_TILE - SUBLANES
    for phase in range(1, SUBLANES):
        u_buf[phase, 0:shifted, :] = u_buf[0, phase:phase + shifted, :]
    acc = jnp.zeros((ROW_TILE, BRANCH), F32) + b_ref[...]
    first = CONV_HALO - (CONV_WIDTH - 1)
    for tap in range(CONV_WIDTH):
        phase, start = (first + tap) % SUBLANES, (first + tap) // SUBLANES * SUBLANES
        acc = acc + w_ref[tap:tap + 1, :] * u_buf[phase, start:start + ROW_TILE, :]
    mu = jnp.mean(acc, axis=-1, keepdims=True)
    cen = acc - mu
    var = jnp.mean(cen * cen, axis=-1, keepdims=True)
    un = cen * lax.rsqrt(var + EPS) * lng_ref[...] + lnb_ref[...]
    y = jnp.dot(_silu(un).astype(BF16), pw_ref[...], preferred_element_type=F32) + pwb_ref[...]
    return y * _silu(z[:, 2 * BRANCH:3 * BRANCH])


def _gla_sum_matrices():
    c = GLA_CHUNK
    i = np.arange(c)[:, None]
    t = np.arange(c)[None, :]
    mats = [t <= i]
    for l in range(GLA_LEVELS):
        h = (c // 2) >> l
        mid = (i // (2 * h)) * (2 * h) + h
        later = (i & h) != 0
        mats.append((later & (t >= mid) & (t <= i)) | (~later & (t > i) & (t < mid)))
    return jnp.asarray(np.concatenate(mats, axis=0), BF16)


def _gla_tile(z_ref, sums_ref, wg_ref, bg_ref, gn_ref, bd_ref, o_ref, state_ref):
    c = GLA_CHUNK
    nh = GLA_HEADS
    kw = nh * GLA_DK
    vw = nh * GLA_DV

    row = lax.broadcasted_iota(jnp.int32, (c, kw), 0)
    qi = lax.broadcasted_iota(jnp.int32, (c, nh * c), 0)
    kj = lax.broadcasted_iota(jnp.int32, (c, nh * c), 1) % c
    level_mask = [(qi >> (GLA_LEVELS - l)) == (kj >> (GLA_LEVELS - l)) for l in range(GLA_LEVELS)]
    diag_mask = qi == kj
    k_head = (lax.broadcasted_iota(jnp.int32, (nh * c, kw), 0) // c
              == lax.broadcasted_iota(jnp.int32, (nh * c, kw), 1) // GLA_DK)
    v_head = (lax.broadcasted_iota(jnp.int32, (nh * c, vw), 0) // c
              == lax.broadcasted_iota(jnp.int32, (nh * c, vw), 1) // GLA_DV)
    s_head = (lax.broadcasted_iota(jnp.int32, (vw, kw), 0) // GLA_DV
              == lax.broadcasted_iota(jnp.int32, (vw, kw), 1) // GLA_DK)

    k_head_bf = jnp.where(k_head, 1.0, 0.0).astype(BF16)

    def per_head_keys(kt):
        return jnp.concatenate([kt.astype(BF16)] * nh, axis=0) * k_head_bf

    def chunk_of(b, rows):
        q = z_ref[b, rows, 0:kw] * GLA_DK ** -0.5
        k = z_ref[b, rows, kw:2 * kw]
        v = z_ref[b, rows, 2 * kw:2 * kw + vw]
        gd = z_ref[b, rows, 2 * kw + vw:2 * kw + 2 * vw]
        lr = z_ref[b, rows, 2 * kw + 2 * vw:2 * kw + 2 * vw + LR_PAD]

        g = jnp.dot(lr.astype(BF16), wg_ref[...], preferred_element_type=F32) + bg_ref[...]
        la = (jnp.minimum(g, 0.0) - jnp.log(1.0 + jnp.exp(-jnp.abs(g)))) / GLA_TAU
        a1 = la.astype(BF16)
        a2 = (la - a1.astype(F32)).astype(BF16)
        parts = jnp.dot(sums_ref[...], jnp.concatenate([a1, a2], axis=1),
                        preferred_element_type=F32)
        sums = parts[:, 0:kw] + parts[:, kw:2 * kw]
        bc = sums[0:c]

        attn = jnp.where(diag_mask,
                         lax.dot_general(q.astype(BF16), per_head_keys(k), _NT,
                                         preferred_element_type=F32), 0.0)
        for l in range(GLA_LEVELS):
            later = (row & ((c // 2) >> l)) != 0
            scaled = jnp.where(later, q, k) * jnp.exp(sums[(1 + l) * c:(2 + l) * c])
            qt = jnp.where(later, scaled, 0.0).astype(BF16)
            a = lax.dot_general(qt, per_head_keys(jnp.where(later, 0.0, scaled)), _NT,
                                preferred_element_type=F32)
            attn = attn + (a if l == 0 else jnp.where(level_mask[l], a, 0.0))

        vb = v.astype(BF16)
        v_stack = jnp.where(v_head, jnp.concatenate([vb] * nh, axis=0), jnp.zeros((), BF16))
        o = jnp.dot(attn.astype(BF16), v_stack, preferred_element_type=F32)

        state = state_ref[b]
        o = o + lax.dot_general((q * jnp.exp(bc)).astype(BF16), state.astype(BF16), _NT,
                                preferred_element_type=F32)
        b_last = bc[c - 1:c, :]
        k_dec = (k * jnp.exp(b_last - bc)).astype(BF16)
        upd = lax.dot_general(vb, k_dec, _TN, preferred_element_type=F32)
        state_ref[b] = state * jnp.exp(b_last) + jnp.where(s_head, upd, 0.0)

        on = o * lax.rsqrt(_group_mean_sq(o, bd_ref[...]) + EPS) * gn_ref[...]
        o_ref[b, rows, :] = on * _silu(gd)

    for ci in range(ROW_TILE // c):
        for b in range(z_ref.shape[0]):
            chunk_of(b, pl.ds(ci * c, c))


def _tail_kernel(x_ref, ya_ref, yb_ref, zc_ref, halo_ref, zd_ref,
                 cw_ref, cb_ref, lng_ref, lnb_ref, pw_ref, pwb_ref,
                 sums_ref, wg_ref, bg_ref, gn_ref, bd_ref, wo_ref,
                 o_ref, u_buf, yd_buf, state_ref):
    j = pl.program_id(0)

    @pl.when(j == 0)
    def _():
        state_ref[...] = jnp.zeros_like(state_ref)

    def project(y, g):
        return jnp.dot(y.astype(BF16), wo_ref[g * BRANCH:(g + 1) * BRANCH, :],
                       preferred_element_type=F32)

    _gla_tile(zd_ref, sums_ref, wg_ref, bg_ref, gn_ref, bd_ref, yd_buf, state_ref)
    for b in range(x_ref.shape[0]):
        yc = _conv_module(zc_ref[b], halo_ref[b], j > 0, cw_ref, cb_ref, lng_ref, lnb_ref,
                          pw_ref, pwb_ref, u_buf.at[b])
        o_ref[b] = (x_ref[b] + project(ya_ref[b], 0) + project(yb_ref[b], 1) + project(yc, 2)
                    + project(yd_buf[b], 3))


def _tail(x2, ya, yb, zc, zd, conv_consts, gla_consts, w_out, batch, seq):
    per = ROW_TILE // CONV_HALO
    tile = lambda cols: pl.BlockSpec((batch, ROW_TILE, cols), lambda j: (0, j, 0))
    whole = lambda a: pl.BlockSpec(a.shape, lambda j: (0,) * a.ndim, pipeline_mode=pl.Buffered(1))
    by_seq = lambda a: a.reshape(batch, seq, a.shape[-1])
    consts = (*conv_consts, *gla_consts, w_out)
    out = pl.pallas_call(
        _tail_kernel,
        grid=(seq // ROW_TILE,),
        in_specs=[tile(D_MODEL), tile(BRANCH), tile(BRANCH), tile(ZC_COLS),
                  pl.BlockSpec((batch, CONV_HALO, ZC_COLS),
                               lambda j: (0, jnp.maximum(j * per - 1, 0), 0)),
                  tile(ZD_COLS)] + [whole(a) for a in consts],
        out_specs=tile(D_MODEL),
        out_shape=jax.ShapeDtypeStruct((batch, seq, D_MODEL), F32),
        scratch_shapes=[pltpu.VMEM((batch, SUBLANES, CONV_HALO + ROW_TILE, BRANCH), F32),
                        pltpu.VMEM((batch, ROW_TILE, BRANCH), F32),
                        pltpu.VMEM((batch, GLA_HEADS * GLA_DV, GLA_HEADS * GLA_DK), F32)],
        compiler_params=_params(1),
        name="tail",
    )(by_seq(x2), by_seq(ya), by_seq(yb), by_seq(zc), by_seq(zc), by_seq(zd), *consts)
    return out.reshape(batch * seq, D_MODEL)


def _pack_w_in(w_in):
    col = lambda j: w_in[:, j * BRANCH:(j + 1) * BRANCH]
    gla0 = 11 * BRANCH
    qkv = 2 * GLA_HEADS * GLA_DK + BRANCH
    pad = jnp.zeros((w_in.shape[0], LR_PAD - GLA_RANK), w_in.dtype)
    wn = jnp.concatenate([col(1), col(3), col(5), col(7), w_in[:, 8 * BRANCH:gla0 + qkv],
                          w_in[:, gla0 + qkv + GLA_RANK:], w_in[:, gla0 + qkv:gla0 + qkv + GLA_RANK], pad],
                         axis=1)
    wt = jnp.concatenate([col(0), col(2), col(4), col(6)], axis=1).T
    return wn.astype(BF16), wt.astype(BF16)


def _layer(x2, batch, seq, consts, norm_g, w_in, q_gain_a, k_gain_a, q_gain_b, k_gain_b, conv_w, conv_b,
           conv_ln_g, conv_ln_b, conv_pw_w, conv_pw_b, gla_gate_w, gla_gate_b, gla_norm_g, w_out):
    bd, kx, qx_moba, qx_dil, dil_table, gla_sums = consts
    kgains = jnp.stack([jnp.tile(k_gain_a, N_HEADS), jnp.tile(k_gain_b, N_HEADS)])
    qgains = jnp.stack([jnp.tile(q_gain_a, N_HEADS), jnp.tile(q_gain_b, N_HEADS)])[:, :, None]
    wn, wt = _pack_w_in(w_in)
    (qat, ka, vat, ga, sel, qbt, kb, vbt, gb, zc, zd) = _inproj(
        x2, norm_g[None, :], wn, wt, kgains, qgains, bd, kx, seq)
    ya = _moba(qat, ka, vat, sel, ga, qx_moba, batch, seq)
    yb = _dilated(qbt, kb, vbt, dil_table, gb, qx_dil, batch, seq)
    wg = jnp.concatenate([gla_gate_w, jnp.zeros((LR_PAD - GLA_RANK, gla_gate_w.shape[1]), F32)],
                         axis=0).astype(BF16)
    conv_consts = (conv_w, conv_b[None, :], conv_ln_g[None, :], conv_ln_b[None, :],
                   conv_pw_w.astype(BF16), conv_pw_b[None, :])
    gla_consts = (gla_sums, wg, gla_gate_b[None, :], jnp.tile(gla_norm_g, GLA_HEADS)[None, :], bd)
    return _tail(x2, ya, yb, zc, zd, conv_consts, gla_consts, w_out.astype(BF16), batch, seq)


def kernel(x, norm_g, w_in, q_gain_a, k_gain_a, q_gain_b, k_gain_b, conv_w, conv_b, conv_ln_g, conv_ln_b,
           conv_pw_w, conv_pw_b, gla_gate_w, gla_gate_b, gla_norm_g, w_out):
    batch, seq, d = x.shape
    assert d == D_MODEL and seq % ROW_TILE == 0 and ROW_TILE == ATT_TILE
    group = np.arange(BRANCH) // HEAD_DIM
    bd = jnp.asarray((group[:, None] == group[None, :]) / HEAD_DIM, BF16)
    heads = np.arange(N_HEADS)
    consts = (bd, _key_position_lanes(seq),
              _query_alibi_rows(2.0 ** -(1.0 + 2 * heads)),
              _query_alibi_rows(2.0 ** -(2.0 + 2 * heads)),
              _dilated_multiplicity_table(), _gla_sum_matrices())
    x2 = x.reshape(batch * seq, d)
    params = (norm_g, w_in, q_gain_a, k_gain_a, q_gain_b, k_gain_b, conv_w, conv_b, conv_ln_g,
              conv_ln_b, conv_pw_w, conv_pw_b, gla_gate_w, gla_gate_b, gla_norm_g, w_out)
    for layer in range(norm_g.shape[0]):
        x2 = _layer(x2, batch, seq, consts, *(p[layer] for p in params))
    return x2.reshape(batch, seq, d)
```

```python
import functools

import numpy as np
import jax
import jax.numpy as jnp
from jax import lax
from jax.experimental import pallas as pl
from jax.experimental.pallas import tpu as pltpu

F32 = jnp.float32
BF16 = jnp.bfloat16

D_MODEL = 1024
BRANCH = 256
HEAD_DIM = 64
N_HEADS = BRANCH // HEAD_DIM
MOBA_BLOCK = 256
MOBA_TOPK = 3
DIL_PATTERNS = ((128, 1), (512, 4), (2048, 16))
CONV_WIDTH = 31
GLA_HEADS = 4
GLA_DK = 32
GLA_DV = 64
GLA_RANK = 16
GLA_TAU = 16.0
EPS = 1e-6
NEG = -1e30
LOG2E = 1.4426950408889634

LANES = 128
SUBLANES = 8
ROW_TILE = 512
ATT_TILE = 512
BLOCKS_PER_TILE = ATT_TILE // MOBA_BLOCK
DIL_GROUPS_BACK = max(w for w, _ in DIL_PATTERNS) // ATT_TILE
ALIBI_PIECES = 4
SEL_LANE0 = 16
MASK_BIAS = 2.0 ** 100
M_INIT = -1e29
GLA_CHUNK = 128
GLA_LEVELS = 7
CONV_HALO = 32
LR_PAD = 128
VMEM_LIMIT = 56 * 1024 * 1024

WT_ROWS = 4 * BRANCH
ZC_COLS = 3 * BRANCH
ZD_COLS = 2 * GLA_HEADS * GLA_DK + 2 * BRANCH + LR_PAD
WN_COLS = 4 * BRANCH + ZC_COLS + ZD_COLS

_NT = (((1,), (1,)), ((), ()))
_TN = (((0,), (0,)), ((), ()))


def _params(n_grid):
    return pltpu.CompilerParams(dimension_semantics=("arbitrary",) * n_grid,
                                vmem_limit_bytes=VMEM_LIMIT)


def _silu(x):
    return x * jax.nn.sigmoid(x)


def _group_mean_sq(z, bd):
    z2 = z * z
    hi = z2.astype(BF16)
    lo = (z2 - hi.astype(F32)).astype(BF16)
    return (jnp.dot(hi, bd, preferred_element_type=F32)
            + jnp.dot(lo, bd, preferred_element_type=F32))


def _inproj_kernel(tiles_per_seq, x_ref, ng_ref, wn_ref, wt_ref, kgain_ref, qgain_ref, bd_ref, kx_ref,
                   qat_ref, ka_ref, vat_ref, ga_ref, sel_ref,
                   qbt_ref, kb_ref, vbt_ref, gb_ref, zc_ref, zd_ref, km_buf):
    tile = pl.program_id(0) % tiles_per_seq

    @pl.when(tile == 0)
    def _():
        km_buf[...] = jnp.zeros_like(km_buf)

    x = x_ref[...]
    ms = jnp.mean(x * x, axis=-1, keepdims=True)
    h = (x * lax.rsqrt(ms + EPS) * ng_ref[...]).astype(BF16)
    bd = bd_ref[...]
    kx = kx_ref[...]

    def proj(c0, width):
        return jnp.dot(h, wn_ref[:, c0:c0 + width], preferred_element_type=F32)

    def proj_t(r0):
        return lax.dot_general(wt_ref[r0:r0 + BRANCH, :], h, _NT, preferred_element_type=F32)

    def head_norm(z, row):
        return z * lax.rsqrt(_group_mean_sq(z, bd) + EPS) * kgain_ref[row:row + 1, :]

    def head_norm_t(zt, idx):
        parts = []
        for g in range(N_HEADS):
            part = zt[g * HEAD_DIM:(g + 1) * HEAD_DIM]
            parts.append(part * lax.rsqrt(jnp.mean(part * part, axis=0, keepdims=True) + EPS))
        return jnp.concatenate(parts, axis=0) * qgain_ref[idx]

    def store_keys(ref, kn):
        for hp in range(BRANCH // LANES):
            ref[:, 2 * hp * LANES:(2 * hp + 1) * LANES] = kn[:, hp * LANES:(hp + 1) * LANES].astype(BF16)
            ref[:, (2 * hp + 1) * LANES:(2 * hp + 2) * LANES] = kx

    qa = head_norm_t(proj_t(0), 0)
    ka = head_norm(proj(0, BRANCH), 0)
    store_keys(ka_ref, ka)
    vat_ref[0] = proj_t(BRANCH).astype(BF16)
    ga_ref[...] = proj(BRANCH, BRANCH)

    for blk in range(BLOCKS_PER_TILE):
        km_buf[pl.ds(tile * BLOCKS_PER_TILE + blk, 1), :] = jnp.mean(
            ka[blk * MOBA_BLOCK:(blk + 1) * MOBA_BLOCK], axis=0, keepdims=True)
    km = km_buf[...]
    n_blk = km.shape[0]
    blk = lax.broadcasted_iota(jnp.int32, (n_blk, ROW_TILE), 0)
    own = (tile * BLOCKS_PER_TILE
           + lax.broadcasted_iota(jnp.int32, (n_blk, ROW_TILE), 1) // MOBA_BLOCK)
    blk_f = blk.astype(F32)
    past = blk < own
    for head, qh in enumerate(_head_operands(qa)):
        gate = jnp.dot(km[:, (head // 2) * LANES:(head // 2 + 1) * LANES], qh,
                       precision=lax.Precision.HIGHEST, preferred_element_type=F32)
        keep = (past & _top_k_rows(jnp.where(past, gate, -jnp.inf), blk_f)) | (blk == own)
        sel_ref[0, head] = jnp.where(keep, 0.0, -MASK_BIAS).astype(BF16)

    qat_ref[0] = (qa * (HEAD_DIM ** -0.5 * LOG2E)).astype(BF16)
    qbt_ref[0] = (head_norm_t(proj_t(2 * BRANCH), 1) * (HEAD_DIM ** -0.5 * LOG2E)).astype(BF16)
    store_keys(kb_ref, head_norm(proj(2 * BRANCH, BRANCH), 1))
    vbt_ref[0] = proj_t(3 * BRANCH).astype(BF16)
    gb_ref[...] = proj(3 * BRANCH, BRANCH)

    zc_ref[...] = proj(4 * BRANCH, ZC_COLS)
    zd_ref[...] = proj(4 * BRANCH + ZC_COLS, ZD_COLS)


def _inproj(x2, ng, wn_all, wt_all, layer, kgains, qgains, bd, kx, seq):
    n = x2.shape[0]
    nt = n // ROW_TILE
    per_seq = seq // ROW_TILE
    row = lambda i: (i, 0)
    const = lambda i: (0, 0)

    def nat(cols, dtype):
        return (jax.ShapeDtypeStruct((n, cols), dtype), pl.BlockSpec((ROW_TILE, cols), row))

    def tr(dtype):
        return (jax.ShapeDtypeStruct((nt, BRANCH, ROW_TILE), dtype),
                pl.BlockSpec((1, BRANCH, ROW_TILE), lambda i: (i, 0, 0)))

    n_blk = seq // MOBA_BLOCK
    choice = (jax.ShapeDtypeStruct((nt, N_HEADS, n_blk, ROW_TILE), BF16),
              pl.BlockSpec((1, N_HEADS, n_blk, ROW_TILE), lambda i: (i, 0, 0, 0)))
    outs = [tr(BF16), nat(2 * BRANCH, BF16), tr(BF16), nat(BRANCH, F32), choice,
            tr(BF16), nat(2 * BRANCH, BF16), tr(BF16), nat(BRANCH, F32),
            nat(ZC_COLS, F32), nat(ZD_COLS, F32)]
    return pl.pallas_call(
        functools.partial(_inproj_kernel, per_seq),
        grid=(nt,),
        in_specs=[pl.BlockSpec((ROW_TILE, D_MODEL), row),
                  pl.BlockSpec((1, D_MODEL), const),
                  pl.BlockSpec((None, D_MODEL, WN_COLS), lambda i: (layer, 0, 0)),
                  pl.BlockSpec((None, WT_ROWS, D_MODEL), lambda i: (layer, 0, 0)),
                  pl.BlockSpec((2, BRANCH), const),
                  pl.BlockSpec((2, BRANCH, 1), lambda i: (0, 0, 0)),
                  pl.BlockSpec((BRANCH, BRANCH), const),
                  pl.BlockSpec((ROW_TILE, LANES), lambda i: (i % per_seq, 0))],
        out_specs=[o[1] for o in outs],
        out_shape=[o[0] for o in outs],
        scratch_shapes=[pltpu.VMEM((n_blk, BRANCH), F32)],
        compiler_params=_params(1),
        name="inproj",
    )(x2, ng, wn_all, wt_all, kgains, qgains, bd, kx)


def _key_position_lanes(seq):
    pos = np.arange(seq)
    c, n = pos % MOBA_BLOCK, pos // MOBA_BLOCK
    kx = np.zeros((seq, LANES), np.float32)
    p = ALIBI_PIECES
    kx[:, 0:p] = (c // 16)[:, None]
    kx[:, p:2 * p] = (c % 16)[:, None]
    kx[:, 2 * p:3 * p] = n[:, None]
    kx[pos, SEL_LANE0 + n] = 1.0
    return jnp.asarray(kx, BF16)


def _query_alibi_rows(slopes):
    pieces, rest = [], LOG2E
    for _ in range(ALIBI_PIECES):
        piece = float(np.asarray(rest, dtype=BF16).astype(np.float64))
        pieces.append(piece)
        rest -= piece
    p = ALIBI_PIECES
    qx = np.zeros((len(slopes), LANES, ATT_TILE), np.float32)
    for h, slope in enumerate(slopes):
        for g, weight in enumerate((16.0, 1.0, float(MOBA_BLOCK))):
            qx[h, g * p:(g + 1) * p, :] = np.asarray([weight * slope * piece for piece in pieces])[:, None]
    return jnp.asarray(qx, F32)


class _SweepOps:
    def __init__(self, k_ref, vt_ref, qft_buf, s_bufs, p_bufs, a_bufs, smax_bufs, m_buf, acc_buf):
        self.k_ref, self.vt_ref, self.qft_buf = k_ref, vt_ref, qft_buf
        self.s_bufs, self.p_bufs, self.a_bufs, self.smax_bufs = s_bufs, p_bufs, a_bufs, smax_bufs
        self.m_buf, self.acc_buf = m_buf, acc_buf

    def init(self):
        self.m_buf[...] = jnp.full(self.m_buf.shape, M_INIT, F32)
        self.acc_buf[...] = jnp.zeros(self.acc_buf.shape, F32)

    def issue_scores(self, group, x, bias=None):
        rows = pl.ds(pl.multiple_of(group * ATT_TILE, ATT_TILE), ATT_TILE)
        for h in range(N_HEADS):
            pair = h // 2
            keys = self.k_ref[rows, 2 * pair * LANES:2 * (pair + 1) * LANES]
            s = jnp.dot(keys, self.qft_buf[h], preferred_element_type=F32)
            if bias is None:
                self.smax_bufs[x][h] = jnp.max(s, axis=0, keepdims=True)
            else:
                s = s + bias
            self.s_bufs[x][h] = s

    def softmax(self, x, mask=None, issued_max=True):
        for h in range(N_HEADS):
            s = self.s_bufs[x][h]
            if mask is not None:
                s = mask(s)
            if mask is None and issued_max:
                group_max = self.smax_bufs[x][h]
            else:
                group_max = jnp.max(s, axis=0, keepdims=True)
            m_old = self.m_buf[h]
            m_new = jnp.maximum(m_old, group_max)
            self.m_buf[h] = m_new
            self.a_bufs[x][h] = jnp.exp2(m_old - m_new)
            self.p_bufs[x][h] = jnp.exp2(s - m_new).astype(BF16)

    def fold_values(self, group, x):
        ones = jnp.ones((SUBLANES, ATT_TILE), BF16)
        vt = self.vt_ref[group]
        for h in range(N_HEADS):
            lhs = jnp.concatenate([vt[h * HEAD_DIM:(h + 1) * HEAD_DIM, :], ones], axis=0)
            self.acc_buf[h] = (self.a_bufs[x][h] * self.acc_buf[h]
                               + jnp.dot(lhs, self.p_bufs[x][h], preferred_element_type=F32))

    def finish(self, g_ref, o_ref):
        acc = self.acc_buf
        out_t = jnp.concatenate(
            [acc[h, 0:HEAD_DIM, :] / acc[h, HEAD_DIM:HEAD_DIM + 1, :] for h in range(N_HEADS)], axis=0)
        o_ref[...] = out_t.T * _silu(g_ref[...])


def _flash_sweep(ops, n_steps, last_mask, g_ref, o_ref):
    def regular_step(t, x):
        ops.issue_scores(t + 1, 1 - x)
        ops.fold_values(jnp.maximum(t - 1, 0), 1 - x)
        ops.softmax(x)

    def neutral_fold(x):
        ops.a_bufs[x][...] = jnp.ones(ops.a_bufs[x].shape, F32)
        ops.p_bufs[x][...] = jnp.zeros(ops.p_bufs[x].shape, BF16)

    ops.init()
    n_regular = n_steps - 1
    odd = n_regular % 2

    @pl.when(odd == 1)
    def _():
        neutral_fold(0)
        ops.issue_scores(0, 1)
        regular_step(0, 1)

    @pl.when(odd == 0)
    def _():
        neutral_fold(1)
        ops.issue_scores(0, 0)

    def step_pair(u, _):
        t = odd + 2 * u
        regular_step(t, 0)
        regular_step(t + 1, 1)
        return 0

    lax.fori_loop(0, n_regular // 2, step_pair, 0)
    ops.fold_values(jnp.maximum(n_steps - 2, 0), 1)
    ops.softmax(0, last_mask)
    ops.fold_values(n_steps - 1, 0)
    ops.finish(g_ref, o_ref)


def _head_operands(qt):
    first = lax.broadcasted_iota(jnp.int32, (LANES, qt.shape[1]), 0) < HEAD_DIM
    zero = jnp.zeros((), qt.dtype)
    out = []
    for h in range(N_HEADS):
        pair = qt[(h // 2) * LANES:(h // 2 + 1) * LANES]
        out.append(jnp.where(first, pair, zero) if h % 2 == 0 else jnp.where(first, zero, pair))
    return out


def _attn_scratch():
    stat = pltpu.VMEM((N_HEADS, 1, ATT_TILE), F32)
    return ([pltpu.VMEM((N_HEADS, 2 * LANES, ATT_TILE), BF16)]
            + [pltpu.VMEM((N_HEADS, ATT_TILE, ATT_TILE), F32)] * 2
            + [pltpu.VMEM((N_HEADS, ATT_TILE, ATT_TILE), BF16)] * 2
            + [stat, stat]
            + [stat, stat]
            + [stat]
            + [pltpu.VMEM((N_HEADS, HEAD_DIM + SUBLANES, ATT_TILE), F32)])


def _attn_specs(seq):
    nq = seq // ATT_TILE
    q_tile = pl.BlockSpec((1, BRANCH, ATT_TILE), lambda b, i: (b * nq + i, 0, 0))
    keys = pl.BlockSpec((seq, 2 * BRANCH), lambda b, i: (b, 0))
    values = pl.BlockSpec((nq, BRANCH, ATT_TILE), lambda b, i: (b, 0, 0))
    gate = pl.BlockSpec((ATT_TILE, BRANCH), lambda b, i: (b * nq + i, 0))
    qx = pl.BlockSpec((N_HEADS, LANES, ATT_TILE), lambda b, i: (0, 0, 0),
                      pipeline_mode=pl.Buffered(1))
    return nq, q_tile, keys, values, gate, qx


def _top_k_rows(gate, row_f):
    sel = jnp.zeros(gate.shape, F32)
    for _ in range(MOBA_TOPK):
        top = jnp.max(gate, axis=0, keepdims=True)
        first = jnp.min(jnp.where(gate == top, row_f, 1e9), axis=0, keepdims=True)
        pick = row_f == first
        sel = jnp.where(pick, 1.0, sel)
        gate = jnp.where(pick, -jnp.inf, gate)
    return sel > 0.5


def _moba_kernel(qt_ref, k_ref, vt_ref, sel_ref, g_ref, qx_ref, o_ref, qft_buf, *bufs):
    i = pl.program_id(1)
    n_blk = sel_ref.shape[2]
    unused = jnp.zeros((LANES - SEL_LANE0 - n_blk, ATT_TILE), BF16)
    for h, qh in enumerate(_head_operands(qt_ref[0])):
        qft_buf[h] = jnp.concatenate(
            [qh, qx_ref[h, 0:SEL_LANE0, :].astype(BF16), sel_ref[0, h], unused], axis=0)

    causal = (lax.broadcasted_iota(jnp.int32, (ATT_TILE, ATT_TILE), 0)
              <= lax.broadcasted_iota(jnp.int32, (ATT_TILE, ATT_TILE), 1))
    ops = _SweepOps(k_ref, vt_ref, qft_buf, bufs[0:2], bufs[2:4], bufs[4:6], bufs[6:8], *bufs[8:])
    _flash_sweep(ops, i + 1, lambda s: jnp.where(causal, s, NEG), g_ref, o_ref)


def _moba(qt, k, vt, sel, g, qx, batch, seq):
    nq, q_tile, keys, values, gate, qx_spec = _attn_specs(seq)
    n_blk = seq // MOBA_BLOCK
    assert SEL_LANE0 + n_blk <= LANES and 3 * ALIBI_PIECES <= SEL_LANE0
    return pl.pallas_call(
        _moba_kernel,
        grid=(batch, nq),
        in_specs=[q_tile, keys, values,
                  pl.BlockSpec((1, N_HEADS, n_blk, ATT_TILE), lambda b, i: (b * nq + i, 0, 0, 0)),
                  gate, qx_spec],
        out_specs=gate,
        out_shape=jax.ShapeDtypeStruct(g.shape, F32),
        scratch_shapes=_attn_scratch(),
        compiler_params=_params(2),
        name="moba",
    )(qt, k, vt, sel, g, qx)


def _dilated_multiplicity_table():
    idx = np.arange(ATT_TILE)
    delta = (np.arange(DIL_GROUPS_BACK + 1)[:, None, None] * ATT_TILE
             + idx[None, None, :] - idx[None, :, None])
    mult = np.zeros(delta.shape, np.float64)
    for window, dil in DIL_PATTERNS:
        mult += (delta >= 0) & (delta <= window) & (delta % dil == 0)
    table = np.where(mult > 0, np.log2(np.maximum(mult, 1.0)), NEG)
    return jnp.asarray(np.concatenate([table, np.full_like(table[:1], NEG)]), F32)


def _dilated_kernel(qt_ref, qt_next_ref, k_ref, vt_ref, t_ref, g_ref, qx_ref, o_ref, qft_buf, *bufs):
    i = pl.program_id(1)
    n_steps = DIL_GROUPS_BACK + 1
    ops = _SweepOps(k_ref, vt_ref, qft_buf, bufs[0:2], bufs[2:4], bufs[4:6], bufs[6:8], *bufs[8:])

    def group_and_table(tile, t):
        inside = t <= tile
        return jnp.where(inside, tile - t, 0), t_ref[jnp.where(inside, t, n_steps)]

    def prepare(tile, q_ref):
        for h, qh in enumerate(_head_operands(q_ref[0])):
            qft_buf[h] = jnp.concatenate([qh, qx_ref[h].astype(BF16)], axis=0)
        ops.issue_scores(tile, 0, t_ref[0])

    ops.init()

    @pl.when(i == 0)
    def _():
        prepare(i, qt_ref)

    for t in range(n_steps):
        x = t % 2
        if t + 1 < n_steps:
            group, table = group_and_table(i, t + 1)
            ops.issue_scores(group, 1 - x, table)
        if t >= 1:
            ops.fold_values(group_and_table(i, t - 1)[0], 1 - x)
        ops.softmax(x, issued_max=False)
    prepare(jnp.minimum(i + 1, pl.num_programs(1) - 1), qt_next_ref)
    ops.fold_values(group_and_table(i, n_steps - 1)[0], (n_steps - 1) % 2)
    ops.finish(g_ref, o_ref)


def _dilated(qt, k, vt, table, g, qx, batch, seq):
    nq, q_tile, keys, values, gate, qx_spec = _attn_specs(seq)
    assert DIL_GROUPS_BACK % 2 == 0
    q_next = pl.BlockSpec((1, BRANCH, ATT_TILE),
                          lambda b, i: (b * nq + jnp.minimum(i + 1, nq - 1), 0, 0))
    return pl.pallas_call(
        _dilated_kernel,
        grid=(batch, nq),
        in_specs=[q_tile, q_next, keys, values,
                  pl.BlockSpec(table.shape, lambda b, i: (0, 0, 0), pipeline_mode=pl.Buffered(1)),
                  gate, qx_spec],
        out_specs=gate,
        out_shape=jax.ShapeDtypeStruct(g.shape, F32),
        scratch_shapes=_attn_scratch(),
        compiler_params=_params(2),
        name="dilated",
    )(qt, qt, k, vt, table, g, qx)


def _conv_module(z, halo, has_history, w_ref, b_ref, lng_ref, lnb_ref, pw_ref, pwb_ref, u_buf):
    def glu(z):
        return z[:, 0:BRANCH] * jax.nn.sigmoid(z[:, BRANCH:2 * BRANCH])

    u_buf[0, 0:CONV_HALO, :] = jnp.where(has_history, glu(halo), 0.0)
    u_buf[0, CONV_HALO:, :] = glu(z)
    shifted = CONV_HALO + ROW_TILE - SUBLANES
    for phase in range(1, SUBLANES):
        u_buf[phase, 0:shifted, :] = u_buf[0, phase:phase + shifted, :]
    acc = jnp.zeros((ROW_TILE, BRANCH), F32) + b_ref[...]
    first = CONV_HALO - (CONV_WIDTH - 1)
    for tap in range(CONV_WIDTH):
        phase, start = (first + tap) % SUBLANES, (first + tap) // SUBLANES * SUBLANES
        acc = acc + w_ref[tap:tap + 1, :] * u_buf[phase, start:start + ROW_TILE, :]
    mu = jnp.mean(acc, axis=-1, keepdims=True)
    cen = acc - mu
    var = jnp.mean(cen * cen, axis=-1, keepdims=True)
    un = cen * lax.rsqrt(var + EPS) * lng_ref[...] + lnb_ref[...]
    y = jnp.dot(_silu(un).astype(BF16), pw_ref[...], preferred_element_type=F32) + pwb_ref[...]
    return y * _silu(z[:, 2 * BRANCH:3 * BRANCH])


def _gla_sum_matrices():
    c = GLA_CHUNK
    i = np.arange(c)[:, None]
    t = np.arange(c)[None, :]
    mats = [t <= i]
    for l in range(GLA_LEVELS):
        h = (c // 2) >> l
        mid = (i // (2 * h)) * (2 * h) + h
        later = (i & h) != 0
        mats.append((later & (t >= mid) & (t <= i)) | (~later & (t > i) & (t < mid)))
    return jnp.asarray(np.concatenate(mats, axis=0), BF16)


def _gla_tile(z_ref, sums_ref, wg_ref, bg_ref, gn_ref, bd_ref, o_ref, state_ref):
    c = GLA_CHUNK
    nh = GLA_HEADS
    kw = nh * GLA_DK
    vw = nh * GLA_DV

    row = lax.broadcasted_iota(jnp.int32, (c, kw), 0)
    qi = lax.broadcasted_iota(jnp.int32, (c, nh * c), 0)
    kj = lax.broadcasted_iota(jnp.int32, (c, nh * c), 1) % c
    level_mask = [(qi >> (GLA_LEVELS - l)) == (kj >> (GLA_LEVELS - l)) for l in range(GLA_LEVELS)]
    diag_mask = qi == kj
    k_head = (lax.broadcasted_iota(jnp.int32, (nh * c, kw), 0) // c
              == lax.broadcasted_iota(jnp.int32, (nh * c, kw), 1) // GLA_DK)
    v_head = (lax.broadcasted_iota(jnp.int32, (nh * c, vw), 0) // c
              == lax.broadcasted_iota(jnp.int32, (nh * c, vw), 1) // GLA_DV)
    s_head = (lax.broadcasted_iota(jnp.int32, (vw, kw), 0) // GLA_DV
              == lax.broadcasted_iota(jnp.int32, (vw, kw), 1) // GLA_DK)

    k_head_bf = jnp.where(k_head, 1.0, 0.0).astype(BF16)

    def per_head_keys(kt):
        return jnp.concatenate([kt.astype(BF16)] * nh, axis=0) * k_head_bf

    def chunk_of(b, rows):
        q = z_ref[b, rows, 0:kw] * GLA_DK ** -0.5
        k = z_ref[b, rows, kw:2 * kw]
        v = z_ref[b, rows, 2 * kw:2 * kw + vw]
        gd = z_ref[b, rows, 2 * kw + vw:2 * kw + 2 * vw]
        lr = z_ref[b, rows, 2 * kw + 2 * vw:2 * kw + 2 * vw + LR_PAD]

        g = jnp.dot(lr.astype(BF16), wg_ref[...], preferred_element_type=F32) + bg_ref[...]
        la = (jnp.minimum(g, 0.0) - jnp.log(1.0 + jnp.exp(-jnp.abs(g)))) / GLA_TAU
        a1 = la.astype(BF16)
        a2 = (la - a1.astype(F32)).astype(BF16)
        parts = jnp.dot(sums_ref[...], jnp.concatenate([a1, a2], axis=1),
                        preferred_element_type=F32)
        sums = parts[:, 0:kw] + parts[:, kw:2 * kw]
        bc = sums[0:c]

        attn = jnp.where(diag_mask,
                         lax.dot_general(q.astype(BF16), per_head_keys(k), _NT,
                                         preferred_element_type=F32), 0.0)
        for l in range(GLA_LEVELS):
            later = (row & ((c // 2) >> l)) != 0
            scaled = jnp.where(later, q, k) * jnp.exp(sums[(1 + l) * c:(2 + l) * c])
            qt = jnp.where(later, scaled, 0.0).astype(BF16)
            a = lax.dot_general(qt, per_head_keys(jnp.where(later, 0.0, scaled)), _NT,
                                preferred_element_type=F32)
            attn = attn + (a if l == 0 else jnp.where(level_mask[l], a, 0.0))

        vb = v.astype(BF16)
        v_stack = jnp.where(v_head, jnp.concatenate([vb] * nh, axis=0), jnp.zeros((), BF16))
        o = jnp.dot(attn.astype(BF16), v_stack, preferred_element_type=F32)

        state = state_ref[b]
        o = o + lax.dot_general((q * jnp.exp(bc)).astype(BF16), state.astype(BF16), _NT,
                                preferred_element_type=F32)
        b_last = bc[c - 1:c, :]
        k_dec = (k * jnp.exp(b_last - bc)).astype(BF16)
        upd = lax.dot_general(vb, k_dec, _TN, preferred_element_type=F32)
        state_ref[b] = state * jnp.exp(b_last) + jnp.where(s_head, upd, 0.0)

        on = o * lax.rsqrt(_group_mean_sq(o, bd_ref[...]) + EPS) * gn_ref[...]
        o_ref[b, rows, :] = on * _silu(gd)

    for ci in range(ROW_TILE // c):
        for b in range(z_ref.shape[0]):
            chunk_of(b, pl.ds(ci * c, c))


def _tail_kernel(x_ref, ya_ref, yb_ref, zc_ref, halo_ref, zd_ref,
                 cw_ref, cb_ref, lng_ref, lnb_ref, pw_ref, pwb_ref,
                 sums_ref, wg_ref, bg_ref, gn_ref, bd_ref, wo_ref,
                 o_ref, u_buf, yd_buf, state_ref):
    j = pl.program_id(0)

    @pl.when(j == 0)
    def _():
        state_ref[...] = jnp.zeros_like(state_ref)

    def project(y, g):
        return jnp.dot(y.astype(BF16), wo_ref[g * BRANCH:(g + 1) * BRANCH, :],
                       preferred_element_type=F32)

    _gla_tile(zd_ref, sums_ref, wg_ref, bg_ref, gn_ref, bd_ref, yd_buf, state_ref)
    for b in range(x_ref.shape[0]):
        yc = _conv_module(zc_ref[b], halo_ref[b], j > 0, cw_ref, cb_ref, lng_ref, lnb_ref,
                          pw_ref, pwb_ref, u_buf.at[b])
        o_ref[b] = (x_ref[b] + project(ya_ref[b], 0) + project(yb_ref[b], 1) + project(yc, 2)
                    + project(yd_buf[b], 3))


def _tail(x2, ya, yb, zc, zd, conv_consts, gla_consts, w_out_all, layer, batch, seq):
    per = ROW_TILE // CONV_HALO
    tile = lambda cols: pl.BlockSpec((batch, ROW_TILE, cols), lambda j: (0, j, 0))
    whole = lambda a: pl.BlockSpec(a.shape, lambda j: (0,) * a.ndim, pipeline_mode=pl.Buffered(1))
    by_seq = lambda a: a.reshape(batch, seq, a.shape[-1])
    consts = (*conv_consts, *gla_consts, w_out_all)
    w_out_spec = pl.BlockSpec((None,) + w_out_all.shape[1:], lambda j: (layer, 0, 0),
                              pipeline_mode=pl.Buffered(1))
    out = pl.pallas_call(
        _tail_kernel,
        grid=(seq // ROW_TILE,),
        in_specs=[tile(D_MODEL), tile(BRANCH), tile(BRANCH), tile(ZC_COLS),
                  pl.BlockSpec((batch, CONV_HALO, ZC_COLS),
                               lambda j: (0, jnp.maximum(j * per - 1, 0), 0)),
                  tile(ZD_COLS)] + [whole(a) for a in consts[:-1]] + [w_out_spec],
        out_specs=tile(D_MODEL),
        out_shape=jax.ShapeDtypeStruct((batch, seq, D_MODEL), F32),
        scratch_shapes=[pltpu.VMEM((batch, SUBLANES, CONV_HALO + ROW_TILE, BRANCH), F32),
                        pltpu.VMEM((batch, ROW_TILE, BRANCH), F32),
                        pltpu.VMEM((batch, GLA_HEADS * GLA_DV, GLA_HEADS * GLA_DK), F32)],
        compiler_params=_params(1),
        name="tail",
    )(by_seq(x2), by_seq(ya), by_seq(yb), by_seq(zc), by_seq(zc), by_seq(zd), *consts)
    return out.reshape(batch * seq, D_MODEL)


def _pack_w_in(w_in):
    col = lambda j: w_in[..., j * BRANCH:(j + 1) * BRANCH]
    gla0 = 11 * BRANCH
    qkv = 2 * GLA_HEADS * GLA_DK + BRANCH
    pad = jnp.zeros(w_in.shape[:-1] + (LR_PAD - GLA_RANK,), w_in.dtype)
    wn = jnp.concatenate([col(1), col(3), col(5), col(7), w_in[..., 8 * BRANCH:gla0 + qkv],
                          w_in[..., gla0 + qkv + GLA_RANK:], w_in[..., gla0 + qkv:gla0 + qkv + GLA_RANK],
                          pad], axis=-1)
    wt = jnp.swapaxes(jnp.concatenate([col(0), col(2), col(4), col(6)], axis=-1), -1, -2)
    return wn.astype(BF16), wt.astype(BF16)


def _layer(x2, batch, seq, consts, layer, big_weights, norm_g, q_gain_a, k_gain_a, q_gain_b, k_gain_b,
           conv_w, conv_b, conv_ln_g, conv_ln_b, conv_pw_w, conv_pw_b, gla_gate_w, gla_gate_b, gla_norm_g):
    bd, kx, qx_moba, qx_dil, dil_table, gla_sums = consts
    wn_all, wt_all, w_out_all = big_weights
    kgains = jnp.stack([jnp.tile(k_gain_a, N_HEADS), jnp.tile(k_gain_b, N_HEADS)])
    qgains = jnp.stack([jnp.tile(q_gain_a, N_HEADS), jnp.tile(q_gain_b, N_HEADS)])[:, :, None]
    (qat, ka, vat, ga, sel, qbt, kb, vbt, gb, zc, zd) = _inproj(
        x2, norm_g[None, :], wn_all, wt_all, layer, kgains, qgains, bd, kx, seq)
    ya = _moba(qat, ka, vat, sel, ga, qx_moba, batch, seq)
    yb = _dilated(qbt, kb, vbt, dil_table, gb, qx_dil, batch, seq)
    wg = jnp.concatenate([gla_gate_w, jnp.zeros((LR_PAD - GLA_RANK, gla_gate_w.shape[1]), F32)],
                         axis=0).astype(BF16)
    conv_consts = (conv_w, conv_b[None, :], conv_ln_g[None, :], conv_ln_b[None, :],
                   conv_pw_w.astype(BF16), conv_pw_b[None, :])
    gla_consts = (gla_sums, wg, gla_gate_b[None, :], jnp.tile(gla_norm_g, GLA_HEADS)[None, :], bd)
    return _tail(x2, ya, yb, zc, zd, conv_consts, gla_consts, w_out_all, layer, batch, seq)


def kernel(x, norm_g, w_in, q_gain_a, k_gain_a, q_gain_b, k_gain_b, conv_w, conv_b, conv_ln_g, conv_ln_b,
           conv_pw_w, conv_pw_b, gla_gate_w, gla_gate_b, gla_norm_g, w_out):
    batch, seq, d = x.shape
    assert d == D_MODEL and seq % ROW_TILE == 0 and ROW_TILE == ATT_TILE
    group = np.arange(BRANCH) // HEAD_DIM
    bd = jnp.asarray((group[:, None] == group[None, :]) / HEAD_DIM, BF16)
    heads = np.arange(N_HEADS)
    consts = (bd, _key_position_lanes(seq),
              _query_alibi_rows(2.0 ** -(1.0 + 2 * heads)),
              _query_alibi_rows(2.0 ** -(2.0 + 2 * heads)),
              _dilated_multiplicity_table(), _gla_sum_matrices())
    x2 = x.reshape(batch * seq, d)
    big_weights = (*_pack_w_in(w_in), w_out.astype(BF16))
    params = (norm_g, q_gain_a, k_gain_a, q_gain_b, k_gain_b, conv_w, conv_b, conv_ln_g,
              conv_ln_b, conv_pw_w, conv_pw_b, gla_gate_w, gla_gate_b, gla_norm_g)
    for layer in range(norm_g.shape[0]):
        x2 = _layer(x2, batch, seq, consts, layer, big_weights, *(p[layer] for p in params))
    return x2.reshape(batch, seq, d)
```

```python
import functools

import numpy as np
import jax
import jax.numpy as jnp
from jax import lax
from jax.experimental import pallas as pl
from jax.experimental.pallas import tpu as pltpu

F32 = jnp.float32
BF16 = jnp.bfloat16

D_MODEL = 1024
BRANCH = 256
HEAD_DIM = 64
N_HEADS = BRANCH // HEAD_DIM
MOBA_BLOCK = 256
MOBA_TOPK = 3
DIL_PATTERNS = ((128, 1), (512, 4), (2048, 16))
CONV_WIDTH = 31
GLA_HEADS = 4
GLA_DK = 32
GLA_DV = 64
GLA_RANK = 16
GLA_TAU = 16.0
EPS = 1e-6
NEG = -1e30
LOG2E = 1.4426950408889634

LANES = 128
SUBLANES = 8
ROW_TILE = 512
ATT_TILE = 512
BLOCKS_PER_TILE = ATT_TILE // MOBA_BLOCK
DIL_GROUPS_BACK = max(w for w, _ in DIL_PATTERNS) // ATT_TILE
ALIBI_PIECES = 4
SEL_LANE0 = 16
MASK_BIAS = 2.0 ** 100
M_INIT = -1e29
GLA_CHUNK = 128
GLA_LEVELS = 7
CONV_HALO = 32
LR_PAD = 128
VMEM_LIMIT = 56 * 1024 * 1024

WT_ROWS = 4 * BRANCH
ZC_COLS = 3 * BRANCH
ZD_COLS = 2 * GLA_HEADS * GLA_DK + 2 * BRANCH + LR_PAD
WN_COLS = 4 * BRANCH + ZC_COLS + ZD_COLS

_NT = (((1,), (1,)), ((), ()))
_TN = (((0,), (0,)), ((), ()))


def _params(n_grid):
    return pltpu.CompilerParams(dimension_semantics=("arbitrary",) * n_grid,
                                vmem_limit_bytes=VMEM_LIMIT)


def _silu(x):
    return x * jax.nn.sigmoid(x)


def _group_mean_sq(z, bd):
    z2 = z * z
    hi = z2.astype(BF16)
    lo = (z2 - hi.astype(F32)).astype(BF16)
    return (jnp.dot(hi, bd, preferred_element_type=F32)
            + jnp.dot(lo, bd, preferred_element_type=F32))


def _inproj_kernel(tiles_per_seq, x_ref, ng_ref, wn_ref, wt_ref, kgain_ref, qgain_ref, bd_ref, kx_ref,
                   qat_ref, ka_ref, vat_ref, ga_ref, sel_ref,
                   qbt_ref, kb_ref, vbt_ref, gb_ref, zc_ref, zd_ref, km_buf):
    tile = pl.program_id(0) % tiles_per_seq

    @pl.when(tile == 0)
    def _():
        km_buf[...] = jnp.zeros_like(km_buf)

    x = x_ref[...]
    ms = jnp.mean(x * x, axis=-1, keepdims=True)
    h = (x * lax.rsqrt(ms + EPS) * ng_ref[...]).astype(BF16)
    bd = bd_ref[...]
    kx = kx_ref[...]

    def proj(c0, width):
        return jnp.dot(h, wn_ref[:, c0:c0 + width], preferred_element_type=F32)

    def proj_t(r0):
        return lax.dot_general(wt_ref[r0:r0 + BRANCH, :], h, _NT, preferred_element_type=F32)

    def head_norm(z, row):
        return z * lax.rsqrt(_group_mean_sq(z, bd) + EPS) * kgain_ref[row:row + 1, :]

    def head_norm_t(zt, idx):
        parts = []
        for g in range(N_HEADS):
            part = zt[g * HEAD_DIM:(g + 1) * HEAD_DIM]
            parts.append(part * lax.rsqrt(jnp.mean(part * part, axis=0, keepdims=True) + EPS))
        return jnp.concatenate(parts, axis=0) * qgain_ref[idx]

    def store_keys(ref, kn):
        for hp in range(BRANCH // LANES):
            ref[:, 2 * hp * LANES:(2 * hp + 1) * LANES] = kn[:, hp * LANES:(hp + 1) * LANES].astype(BF16)
            ref[:, (2 * hp + 1) * LANES:(2 * hp + 2) * LANES] = kx

    qa = head_norm_t(proj_t(0), 0)
    ka = head_norm(proj(0, BRANCH), 0)
    store_keys(ka_ref, ka)
    vat_ref[0] = proj_t(BRANCH).astype(BF16)
    ga_ref[...] = proj(BRANCH, BRANCH)

    for blk in range(BLOCKS_PER_TILE):
        km_buf[pl.ds(tile * BLOCKS_PER_TILE + blk, 1), :] = jnp.mean(
            ka[blk * MOBA_BLOCK:(blk + 1) * MOBA_BLOCK], axis=0, keepdims=True)
    km = km_buf[...]
    n_blk = km.shape[0]
    blk = lax.broadcasted_iota(jnp.int32, (n_blk, ROW_TILE), 0)
    own = (tile * BLOCKS_PER_TILE
           + lax.broadcasted_iota(jnp.int32, (n_blk, ROW_TILE), 1) // MOBA_BLOCK)
    blk_f = blk.astype(F32)
    past = blk < own
    for head, qh in enumerate(_head_operands(qa)):
        gate = jnp.dot(km[:, (head // 2) * LANES:(head // 2 + 1) * LANES], qh,
                       precision=lax.Precision.HIGHEST, preferred_element_type=F32)
        keep = (past & _top_k_rows(jnp.where(past, gate, -jnp.inf), blk_f)) | (blk == own)
        sel_ref[0, head] = jnp.where(keep, 0.0, -MASK_BIAS).astype(BF16)

    qat_ref[0] = (qa * (HEAD_DIM ** -0.5 * LOG2E)).astype(BF16)
    qbt_ref[0] = (head_norm_t(proj_t(2 * BRANCH), 1) * (HEAD_DIM ** -0.5 * LOG2E)).astype(BF16)
    store_keys(kb_ref, head_norm(proj(2 * BRANCH, BRANCH), 1))
    vbt_ref[0] = proj_t(3 * BRANCH).astype(BF16)
    gb_ref[...] = proj(3 * BRANCH, BRANCH)

    zc_ref[...] = proj(4 * BRANCH, ZC_COLS)
    zd_ref[...] = proj(4 * BRANCH + ZC_COLS, ZD_COLS)


def _inproj(x2, ng, wn_all, wt_all, layer, kgains, qgains, bd, kx, seq):
    n = x2.shape[0]
    nt = n // ROW_TILE
    per_seq = seq // ROW_TILE
    row = lambda i: (i, 0)
    const = lambda i: (0, 0)

    def nat(cols, dtype):
        return (jax.ShapeDtypeStruct((n, cols), dtype), pl.BlockSpec((ROW_TILE, cols), row))

    def tr(dtype):
        return (jax.ShapeDtypeStruct((nt, BRANCH, ROW_TILE), dtype),
                pl.BlockSpec((1, BRANCH, ROW_TILE), lambda i: (i, 0, 0)))

    n_blk = seq // MOBA_BLOCK
    choice = (jax.ShapeDtypeStruct((nt, N_HEADS, n_blk, ROW_TILE), BF16),
              pl.BlockSpec((1, N_HEADS, n_blk, ROW_TILE), lambda i: (i, 0, 0, 0)))
    outs = [tr(BF16), nat(2 * BRANCH, BF16), tr(BF16), nat(BRANCH, F32), choice,
            tr(BF16), nat(2 * BRANCH, BF16), tr(BF16), nat(BRANCH, F32),
            nat(ZC_COLS, F32), nat(ZD_COLS, F32)]
    return pl.pallas_call(
        functools.partial(_inproj_kernel, per_seq),
        grid=(nt,),
        in_specs=[pl.BlockSpec((ROW_TILE, D_MODEL), row),
                  pl.BlockSpec((1, D_MODEL), const),
                  pl.BlockSpec((None, D_MODEL, WN_COLS), lambda i: (layer, 0, 0)),
                  pl.BlockSpec((None, WT_ROWS, D_MODEL), lambda i: (layer, 0, 0)),
                  pl.BlockSpec((2, BRANCH), const),
                  pl.BlockSpec((2, BRANCH, 1), lambda i: (0, 0, 0)),
                  pl.BlockSpec((BRANCH, BRANCH), const),
                  pl.BlockSpec((ROW_TILE, LANES), lambda i: (i % per_seq, 0))],
        out_specs=[o[1] for o in outs],
        out_shape=[o[0] for o in outs],
        scratch_shapes=[pltpu.VMEM((n_blk, BRANCH), F32)],
        compiler_params=_params(1),
        name="inproj",
    )(x2, ng, wn_all, wt_all, kgains, qgains, bd, kx)


def _key_position_lanes(seq):
    pos = np.arange(seq)
    c, n = pos % MOBA_BLOCK, pos // MOBA_BLOCK
    kx = np.zeros((seq, LANES), np.float32)
    p = ALIBI_PIECES
    kx[:, 0:p] = (c // 16)[:, None]
    kx[:, p:2 * p] = (c % 16)[:, None]
    kx[:, 2 * p:3 * p] = n[:, None]
    kx[pos, SEL_LANE0 + n] = 1.0
    return jnp.asarray(kx, BF16)


def _query_alibi_rows(slopes):
    pieces, rest = [], LOG2E
    for _ in range(ALIBI_PIECES):
        piece = float(np.asarray(rest, dtype=BF16).astype(np.float64))
        pieces.append(piece)
        rest -= piece
    p = ALIBI_PIECES
    qx = np.zeros((len(slopes), LANES, ATT_TILE), np.float32)
    for h, slope in enumerate(slopes):
        for g, weight in enumerate((16.0, 1.0, float(MOBA_BLOCK))):
            qx[h, g * p:(g + 1) * p, :] = np.asarray([weight * slope * piece for piece in pieces])[:, None]
    return jnp.asarray(qx, F32)


class _SweepOps:
    def __init__(self, k_ref, vt_ref, qft_buf, s_bufs, p_bufs, a_bufs, smax_bufs, m_buf, acc_buf):
        self.k_ref, self.vt_ref, self.qft_buf = k_ref, vt_ref, qft_buf
        self.s_bufs, self.p_bufs, self.a_bufs, self.smax_bufs = s_bufs, p_bufs, a_bufs, smax_bufs
        self.m_buf, self.acc_buf = m_buf, acc_buf

    def init(self):
        self.m_buf[...] = jnp.full(self.m_buf.shape, M_INIT, F32)
        self.acc_buf[...] = jnp.zeros(self.acc_buf.shape, F32)

    def issue_scores(self, group, x, bias=None):
        rows = pl.ds(pl.multiple_of(group * ATT_TILE, ATT_TILE), ATT_TILE)
        for h in range(N_HEADS):
            pair = h // 2
            keys = self.k_ref[rows, 2 * pair * LANES:2 * (pair + 1) * LANES]
            s = jnp.dot(keys, self.qft_buf[h], preferred_element_type=F32)
            if bias is None:
                self.smax_bufs[x][h] = jnp.max(s, axis=0, keepdims=True)
            else:
                s = s + bias
            self.s_bufs[x][h, :, 0:ATT_TILE] = s

    def softmax(self, x, mask=None, issued_max=True):
        for h in range(N_HEADS):
            s = self.s_bufs[x][h, :, 0:ATT_TILE]
            if mask is not None:
                s = mask(s)
            if mask is None and issued_max:
                group_max = self.smax_bufs[x][h]
            else:
                group_max = jnp.max(s, axis=0, keepdims=True)
            m_old = self.m_buf[h]
            m_new = jnp.maximum(m_old, group_max)
            self.m_buf[h] = m_new
            self.a_bufs[x][h] = jnp.exp2(m_old - m_new)
            self.p_bufs[x][h, :, 0:ATT_TILE] = jnp.exp2(s - m_new).astype(BF16)

    def fold_values(self, group, x):
        ones = jnp.ones((SUBLANES, ATT_TILE), BF16)
        vt = self.vt_ref[group]
        for h in range(N_HEADS):
            lhs = jnp.concatenate([vt[h * HEAD_DIM:(h + 1) * HEAD_DIM, :], ones], axis=0)
            self.acc_buf[h] = (self.a_bufs[x][h] * self.acc_buf[h]
                               + jnp.dot(lhs, self.p_bufs[x][h, :, 0:ATT_TILE],
                                         preferred_element_type=F32))

    def finish(self, g_ref, o_ref):
        acc = self.acc_buf
        out_t = jnp.concatenate(
            [acc[h, 0:HEAD_DIM, :] / acc[h, HEAD_DIM:HEAD_DIM + 1, :] for h in range(N_HEADS)], axis=0)
        o_ref[...] = out_t.T * _silu(g_ref[...])


def _flash_sweep(ops, n_steps, last_mask, g_ref, o_ref):
    def regular_step(t, x):
        ops.issue_scores(t + 1, 1 - x)
        ops.fold_values(jnp.maximum(t - 1, 0), 1 - x)
        ops.softmax(x)

    def neutral_fold(x):
        ops.a_bufs[x][...] = jnp.ones(ops.a_bufs[x].shape, F32)
        ops.p_bufs[x][...] = jnp.zeros(ops.p_bufs[x].shape, BF16)

    ops.init()
    n_regular = n_steps - 1
    odd = n_regular % 2

    @pl.when(odd == 1)
    def _():
        neutral_fold(0)
        ops.issue_scores(0, 1)
        regular_step(0, 1)

    @pl.when(odd == 0)
    def _():
        neutral_fold(1)
        ops.issue_scores(0, 0)

    def step_pair(u, _):
        t = odd + 2 * u
        regular_step(t, 0)
        regular_step(t + 1, 1)
        return 0

    lax.fori_loop(0, n_regular // 2, step_pair, 0)
    ops.fold_values(jnp.maximum(n_steps - 2, 0), 1)
    ops.softmax(0, last_mask)
    ops.fold_values(n_steps - 1, 0)
    ops.finish(g_ref, o_ref)


def _head_operands(qt):
    first = lax.broadcasted_iota(jnp.int32, (LANES, qt.shape[1]), 0) < HEAD_DIM
    zero = jnp.zeros((), qt.dtype)
    out = []
    for h in range(N_HEADS):
        pair = qt[(h // 2) * LANES:(h // 2 + 1) * LANES]
        out.append(jnp.where(first, pair, zero) if h % 2 == 0 else jnp.where(first, zero, pair))
    return out


def _attn_scratch():
    stat = pltpu.VMEM((N_HEADS, 1, ATT_TILE), F32)
    return ([pltpu.VMEM((N_HEADS, 2 * LANES, ATT_TILE), BF16)]
            + [pltpu.VMEM((N_HEADS, ATT_TILE, ATT_TILE + LANES), F32)] * 2
            + [pltpu.VMEM((N_HEADS, ATT_TILE, ATT_TILE + LANES), BF16)] * 2
            + [stat, stat]
            + [stat, stat]
            + [stat]
            + [pltpu.VMEM((N_HEADS, HEAD_DIM + SUBLANES, ATT_TILE), F32)])


def _attn_specs(seq):
    nq = seq // ATT_TILE
    q_tile = pl.BlockSpec((1, BRANCH, ATT_TILE), lambda b, i: (b * nq + i, 0, 0))
    keys = pl.BlockSpec((seq, 2 * BRANCH), lambda b, i: (b, 0))
    values = pl.BlockSpec((nq, BRANCH, ATT_TILE), lambda b, i: (b, 0, 0))
    gate = pl.BlockSpec((ATT_TILE, BRANCH), lambda b, i: (b * nq + i, 0))
    qx = pl.BlockSpec((N_HEADS, LANES, ATT_TILE), lambda b, i: (0, 0, 0),
                      pipeline_mode=pl.Buffered(1))
    return nq, q_tile, keys, values, gate, qx


def _top_k_rows(gate, row_f):
    sel = jnp.zeros(gate.shape, F32)
    for _ in range(MOBA_TOPK):
        top = jnp.max(gate, axis=0, keepdims=True)
        first = jnp.min(jnp.where(gate == top, row_f, 1e9), axis=0, keepdims=True)
        pick = row_f == first
        sel = jnp.where(pick, 1.0, sel)
        gate = jnp.where(pick, -jnp.inf, gate)
    return sel > 0.5


def _moba_kernel(qt_ref, k_ref, vt_ref, sel_ref, g_ref, qx_ref, o_ref, qft_buf, *bufs):
    i = pl.program_id(1)
    n_blk = sel_ref.shape[2]
    unused = jnp.zeros((LANES - SEL_LANE0 - n_blk, ATT_TILE), BF16)
    for h, qh in enumerate(_head_operands(qt_ref[0])):
        qft_buf[h] = jnp.concatenate(
            [qh, qx_ref[h, 0:SEL_LANE0, :].astype(BF16), sel_ref[0, h], unused], axis=0)

    causal = (lax.broadcasted_iota(jnp.int32, (ATT_TILE, ATT_TILE), 0)
              <= lax.broadcasted_iota(jnp.int32, (ATT_TILE, ATT_TILE), 1))
    ops = _SweepOps(k_ref, vt_ref, qft_buf, bufs[0:2], bufs[2:4], bufs[4:6], bufs[6:8], *bufs[8:])
    _flash_sweep(ops, i + 1, lambda s: jnp.where(causal, s, NEG), g_ref, o_ref)


def _moba(qt, k, vt, sel, g, qx, batch, seq):
    nq, q_tile, keys, values, gate, qx_spec = _attn_specs(seq)
    n_blk = seq // MOBA_BLOCK
    assert SEL_LANE0 + n_blk <= LANES and 3 * ALIBI_PIECES <= SEL_LANE0
    return pl.pallas_call(
        _moba_kernel,
        grid=(batch, nq),
        in_specs=[q_tile, keys, values,
                  pl.BlockSpec((1, N_HEADS, n_blk, ATT_TILE), lambda b, i: (b * nq + i, 0, 0, 0)),
                  gate, qx_spec],
        out_specs=gate,
        out_shape=jax.ShapeDtypeStruct(g.shape, F32),
        scratch_shapes=_attn_scratch(),
        compiler_params=_params(2),
        name="moba",
    )(qt, k, vt, sel, g, qx)


def _dilated_multiplicity_table():
    idx = np.arange(ATT_TILE)
    delta = (np.arange(DIL_GROUPS_BACK + 1)[:, None, None] * ATT_TILE
             + idx[None, None, :] - idx[None, :, None])
    mult = np.zeros(delta.shape, np.float64)
    for window, dil in DIL_PATTERNS:
        mult += (delta >= 0) & (delta <= window) & (delta % dil == 0)
    table = np.where(mult > 0, np.log2(np.maximum(mult, 1.0)), NEG)
    return jnp.asarray(np.concatenate([table, np.full_like(table[:1], NEG)]), F32)


def _dilated_kernel(qt_ref, qt_next_ref, k_ref, vt_ref, t_ref, g_ref, qx_ref, o_ref, qft_buf, *bufs):
    i = pl.program_id(1)
    n_steps = DIL_GROUPS_BACK + 1
    ops = _SweepOps(k_ref, vt_ref, qft_buf, bufs[0:2], bufs[2:4], bufs[4:6], bufs[6:8], *bufs[8:])

    def group_and_table(tile, t):
        inside = t <= tile
        return jnp.where(inside, tile - t, 0), t_ref[jnp.where(inside, t, n_steps)]

    def prepare(tile, q_ref):
        for h, qh in enumerate(_head_operands(q_ref[0])):
            qft_buf[h] = jnp.concatenate([qh, qx_ref[h].astype(BF16)], axis=0)
        ops.issue_scores(tile, 0, t_ref[0])

    ops.init()

    @pl.when(i == 0)
    def _():
        prepare(i, qt_ref)

    for t in range(n_steps):
        x = t % 2
        if t + 1 < n_steps:
            group, table = group_and_table(i, t + 1)
            ops.issue_scores(group, 1 - x, table)
        if t >= 1:
            ops.fold_values(group_and_table(i, t - 1)[0], 1 - x)
        ops.softmax(x, issued_max=False)
    prepare(jnp.minimum(i + 1, pl.num_programs(1) - 1), qt_next_ref)
    ops.fold_values(group_and_table(i, n_steps - 1)[0], (n_steps - 1) % 2)
    ops.finish(g_ref, o_ref)


def _dilated(qt, k, vt, table, g, qx, batch, seq):
    nq, q_tile, keys, values, gate, qx_spec = _attn_specs(seq)
    assert DIL_GROUPS_BACK % 2 == 0
    q_next = pl.BlockSpec((1, BRANCH, ATT_TILE),
                          lambda b, i: (b * nq + jnp.minimum(i + 1, nq - 1), 0, 0))
    return pl.pallas_call(
        _dilated_kernel,
        grid=(batch, nq),
        in_specs=[q_tile, q_next, keys, values,
                  pl.BlockSpec(table.shape, lambda b, i: (0, 0, 0), pipeline_mode=pl.Buffered(1)),
                  gate, qx_spec],
        out_specs=gate,
        out_shape=jax.ShapeDtypeStruct(g.shape, F32),
        scratch_shapes=_attn_scratch(),
        compiler_params=_params(2),
        name="dilated",
    )(qt, qt, k, vt, table, g, qx)


def _conv_module(z, halo, has_history, w_ref, b_ref, lng_ref, lnb_ref, pw_ref, pwb_ref, u_buf):
    def glu(z):
        return z[:, 0:BRANCH] * jax.nn.sigmoid(z[:, BRANCH:2 * BRANCH])

    u_buf[0, 0:CONV_HALO, :] = jnp.where(has_history, glu(halo), 0.0)
    u_buf[0, CONV_HALO:, :] = glu(z)
    shifted = CONV_HALO + ROW_TILE - SUBLANES
    for phase in range(1, SUBLANES):
        u_buf[phase, 0:shifted, :] = u_buf[0, phase:phase + shifted, :]
    acc = jnp.zeros((ROW_TILE, BRANCH), F32) + b_ref[...]
    first = CONV_HALO - (CONV_WIDTH - 1)
    for tap in range(CONV_WIDTH):
        phase, start = (first + tap) % SUBLANES, (first + tap) // SUBLANES * SUBLANES
        acc = acc + w_ref[tap:tap + 1, :] * u_buf[phase, start:start + ROW_TILE, :]
    mu = jnp.mean(acc, axis=-1, keepdims=True)
    cen = acc - mu
    var = jnp.mean(cen * cen, axis=-1, keepdims=True)
    un = cen * lax.rsqrt(var + EPS) * lng_ref[...] + lnb_ref[...]
    y = jnp.dot(_silu(un).astype(BF16), pw_ref[...], preferred_element_type=F32) + pwb_ref[...]
    return y * _silu(z[:, 2 * BRANCH:3 * BRANCH])


def _gla_sum_matrices():
    c = GLA_CHUNK
    i = np.arange(c)[:, None]
    t = np.arange(c)[None, :]
    mats = [t <= i]
    for l in range(GLA_LEVELS):
        h = (c // 2) >> l
        mid = (i // (2 * h)) * (2 * h) + h
        later = (i & h) != 0
        mats.append((later & (t >= mid) & (t <= i)) | (~later & (t > i) & (t < mid)))
    return jnp.asarray(np.concatenate(mats, axis=0), BF16)


def _gla_tile(z_ref, sums_ref, wg_ref, bg_ref, gn_ref, bd_ref, o_ref, state_ref):
    c = GLA_CHUNK
    nh = GLA_HEADS
    kw = nh * GLA_DK
    vw = nh * GLA_DV

    row = lax.broadcasted_iota(jnp.int32, (c, kw), 0)
    qi = lax.broadcasted_iota(jnp.int32, (c, nh * c), 0)
    kj = lax.broadcasted_iota(jnp.int32, (c, nh * c), 1) % c
    level_mask = [(qi >> (GLA_LEVELS - l)) == (kj >> (GLA_LEVELS - l)) for l in range(GLA_LEVELS)]
    diag_mask = qi == kj
    k_head = (lax.broadcasted_iota(jnp.int32, (nh * c, kw), 0) // c
              == lax.broadcasted_iota(jnp.int32, (nh * c, kw), 1) // GLA_DK)
    v_head = (lax.broadcasted_iota(jnp.int32, (nh * c, vw), 0) // c
              == lax.broadcasted_iota(jnp.int32, (nh * c, vw), 1) // GLA_DV)
    s_head = (lax.broadcasted_iota(jnp.int32, (vw, kw), 0) // GLA_DV
              == lax.broadcasted_iota(jnp.int32, (vw, kw), 1) // GLA_DK)

    k_head_bf = jnp.where(k_head, 1.0, 0.0).astype(BF16)

    def per_head_keys(kt):
        return jnp.concatenate([kt.astype(BF16)] * nh, axis=0) * k_head_bf

    def chunk_of(b, rows):
        q = z_ref[b, rows, 0:kw] * GLA_DK ** -0.5
        k = z_ref[b, rows, kw:2 * kw]
        v = z_ref[b, rows, 2 * kw:2 * kw + vw]
        gd = z_ref[b, rows, 2 * kw + vw:2 * kw + 2 * vw]
        lr = z_ref[b, rows, 2 * kw + 2 * vw:2 * kw + 2 * vw + LR_PAD]

        g = jnp.dot(lr.astype(BF16), wg_ref[...], preferred_element_type=F32) + bg_ref[...]
        la = (jnp.minimum(g, 0.0) - jnp.log(1.0 + jnp.exp(-jnp.abs(g)))) / GLA_TAU
        a1 = la.astype(BF16)
        a2 = (la - a1.astype(F32)).astype(BF16)
        parts = jnp.dot(sums_ref[...], jnp.concatenate([a1, a2], axis=1),
                        preferred_element_type=F32)
        sums = parts[:, 0:kw] + parts[:, kw:2 * kw]
        bc = sums[0:c]

        attn = jnp.where(diag_mask,
                         lax.dot_general(q.astype(BF16), per_head_keys(k), _NT,
                                         preferred_element_type=F32), 0.0)
        for l in range(GLA_LEVELS):
            later = (row & ((c // 2) >> l)) != 0
            scaled = jnp.where(later, q, k) * jnp.exp(sums[(1 + l) * c:(2 + l) * c])
            qt = jnp.where(later, scaled, 0.0).astype(BF16)
            a = lax.dot_general(qt, per_head_keys(jnp.where(later, 0.0, scaled)), _NT,
                                preferred_element_type=F32)
            attn = attn + (a if l == 0 else jnp.where(level_mask[l], a, 0.0))

        vb = v.astype(BF16)
        v_stack = jnp.where(v_head, jnp.concatenate([vb] * nh, axis=0), jnp.zeros((), BF16))
        o = jnp.dot(attn.astype(BF16), v_stack, preferred_element_type=F32)

        state = state_ref[b]
        o = o + lax.dot_general((q * jnp.exp(bc)).astype(BF16), state.astype(BF16), _NT,
                                preferred_element_type=F32)
        b_last = bc[c - 1:c, :]
        k_dec = (k * jnp.exp(b_last - bc)).astype(BF16)
        upd = lax.dot_general(vb, k_dec, _TN, preferred_element_type=F32)
        state_ref[b] = state * jnp.exp(b_last) + jnp.where(s_head, upd, 0.0)

        on = o * lax.rsqrt(_group_mean_sq(o, bd_ref[...]) + EPS) * gn_ref[...]
        o_ref[b, rows, :] = on * _silu(gd)

    for ci in range(ROW_TILE // c):
        for b in range(z_ref.shape[0]):
            chunk_of(b, pl.ds(ci * c, c))


def _tail_kernel(x_ref, ya_ref, yb_ref, zc_ref, halo_ref, zd_ref,
                 cw_ref, cb_ref, lng_ref, lnb_ref, pw_ref, pwb_ref,
                 sums_ref, wg_ref, bg_ref, gn_ref, bd_ref, wo_ref,
                 o_ref, u_buf, yd_buf, state_ref):
    j = pl.program_id(0)

    @pl.when(j == 0)
    def _():
        state_ref[...] = jnp.zeros_like(state_ref)

    def project(y, g):
        return jnp.dot(y.astype(BF16), wo_ref[g * BRANCH:(g + 1) * BRANCH, :],
                       preferred_element_type=F32)

    _gla_tile(zd_ref, sums_ref, wg_ref, bg_ref, gn_ref, bd_ref, yd_buf, state_ref)
    for b in range(x_ref.shape[0]):
        yc = _conv_module(zc_ref[b], halo_ref[b], j > 0, cw_ref, cb_ref, lng_ref, lnb_ref,
                          pw_ref, pwb_ref, u_buf.at[b])
        o_ref[b] = (x_ref[b] + project(ya_ref[b], 0) + project(yb_ref[b], 1) + project(yc, 2)
                    + project(yd_buf[b], 3))


def _tail(x2, ya, yb, zc, zd, conv_consts, gla_consts, w_out_all, layer, batch, seq):
    per = ROW_TILE // CONV_HALO
    tile = lambda cols: pl.BlockSpec((batch, ROW_TILE, cols), lambda j: (0, j, 0))
    whole = lambda a: pl.BlockSpec(a.shape, lambda j: (0,) * a.ndim, pipeline_mode=pl.Buffered(1))
    by_seq = lambda a: a.reshape(batch, seq, a.shape[-1])
    consts = (*conv_consts, *gla_consts, w_out_all)
    w_out_spec = pl.BlockSpec((None,) + w_out_all.shape[1:], lambda j: (layer, 0, 0),
                              pipeline_mode=pl.Buffered(1))
    out = pl.pallas_call(
        _tail_kernel,
        grid=(seq // ROW_TILE,),
        in_specs=[tile(D_MODEL), tile(BRANCH), tile(BRANCH), tile(ZC_COLS),
                  pl.BlockSpec((batch, CONV_HALO, ZC_COLS),
                               lambda j: (0, jnp.maximum(j * per - 1, 0), 0)),
                  tile(ZD_COLS)] + [whole(a) for a in consts[:-1]] + [w_out_spec],
        out_specs=tile(D_MODEL),
        out_shape=jax.ShapeDtypeStruct((batch, seq, D_MODEL), F32),
        scratch_shapes=[pltpu.VMEM((batch, SUBLANES, CONV_HALO + ROW_TILE, BRANCH), F32),
                        pltpu.VMEM((batch, ROW_TILE, BRANCH), F32),
                        pltpu.VMEM((batch, GLA_HEADS * GLA_DV, GLA_HEADS * GLA_DK), F32)],
        compiler_params=_params(1),
        name="tail",
    )(by_seq(x2), by_seq(ya), by_seq(yb), by_seq(zc), by_seq(zc), by_seq(zd), *consts)
    return out.reshape(batch * seq, D_MODEL)


def _pack_w_in(w_in):
    col = lambda j: w_in[..., j * BRANCH:(j + 1) * BRANCH]
    gla0 = 11 * BRANCH
    qkv = 2 * GLA_HEADS * GLA_DK + BRANCH
    pad = jnp.zeros(w_in.shape[:-1] + (LR_PAD - GLA_RANK,), w_in.dtype)
    wn = jnp.concatenate([col(1), col(3), col(5), col(7), w_in[..., 8 * BRANCH:gla0 + qkv],
                          w_in[..., gla0 + qkv + GLA_RANK:], w_in[..., gla0 + qkv:gla0 + qkv + GLA_RANK],
                          pad], axis=-1)
    wt = jnp.swapaxes(jnp.concatenate([col(0), col(2), col(4), col(6)], axis=-1), -1, -2)
    return wn.astype(BF16), wt.astype(BF16)


def _layer(x2, batch, seq, consts, layer, big_weights, norm_g, q_gain_a, k_gain_a, q_gain_b, k_gain_b,
           conv_w, conv_b, conv_ln_g, conv_ln_b, conv_pw_w, conv_pw_b, gla_gate_w, gla_gate_b, gla_norm_g):
    bd, kx, qx_moba, qx_dil, dil_table, gla_sums = consts
    wn_all, wt_all, w_out_all = big_weights
    kgains = jnp.stack([jnp.tile(k_gain_a, N_HEADS), jnp.tile(k_gain_b, N_HEADS)])
    qgains = jnp.stack([jnp.tile(q_gain_a, N_HEADS), jnp.tile(q_gain_b, N_HEADS)])[:, :, None]
    (qat, ka, vat, ga, sel, qbt, kb, vbt, gb, zc, zd) = _inproj(
        x2, norm_g[None, :], wn_all, wt_all, layer, kgains, qgains, bd, kx, seq)
    ya = _moba(qat, ka, vat, sel, ga, qx_moba, batch, seq)
    yb = _dilated(qbt, kb, vbt, dil_table, gb, qx_dil, batch, seq)
    wg = jnp.concatenate([gla_gate_w, jnp.zeros((LR_PAD - GLA_RANK, gla_gate_w.shape[1]), F32)],
                         axis=0).astype(BF16)
    conv_consts = (conv_w, conv_b[None, :], conv_ln_g[None, :], conv_ln_b[None, :],
                   conv_pw_w.astype(BF16), conv_pw_b[None, :])
    gla_consts = (gla_sums, wg, gla_gate_b[None, :], jnp.tile(gla_norm_g, GLA_HEADS)[None, :], bd)
    return _tail(x2, ya, yb, zc, zd, conv_consts, gla_consts, w_out_all, layer, batch, seq)


def kernel(x, norm_g, w_in, q_gain_a, k_gain_a, q_gain_b, k_gain_b, conv_w, conv_b, conv_ln_g, conv_ln_b,
           conv_pw_w, conv_pw_b, gla_gate_w, gla_gate_b, gla_norm_g, w_out):
    batch, seq, d = x.shape
    assert d == D_MODEL and seq % ROW_TILE == 0 and ROW_TILE == ATT_TILE
    group = np.arange(BRANCH) // HEAD_DIM
    bd = jnp.asarray((group[:, None] == group[None, :]) / HEAD_DIM, BF16)
    heads = np.arange(N_HEADS)
    consts = (bd, _key_position_lanes(seq),
              _query_alibi_rows(2.0 ** -(1.0 + 2 * heads)),
              _query_alibi_rows(2.0 ** -(2.0 + 2 * heads)),
              _dilated_multiplicity_table(), _gla_sum_matrices())
    x2 = x.reshape(batch * seq, d)
    big_weights = (*_pack_w_in(w_in), w_out.astype(BF16))
    params = (norm_g, q_gain_a, k_gain_a, q_gain_b, k_gain_b, conv_w, conv_b, conv_ln_g,
              conv_ln_b, conv_pw_w, conv_pw_b, gla_gate_w, gla_gate_b, gla_norm_g)
    for layer in range(norm_g.shape[0]):
        x2 = _layer(x2, batch, seq, consts, layer, big_weights, *(p[layer] for p in params))
    return x2.reshape(batch, seq, d)
```

```python
import functools

import numpy as np
import jax
import jax.numpy as jnp
from jax import lax
from jax.experimental import pallas as pl
from jax.experimental.pallas import tpu as pltpu

F32 = jnp.float32
BF16 = jnp.bfloat16

D_MODEL = 1024
BRANCH = 256
HEAD_DIM = 64
N_HEADS = BRANCH // HEAD_DIM
MOBA_BLOCK = 256
MOBA_TOPK = 3
DIL_PATTERNS = ((128, 1), (512, 4), (2048, 16))
CONV_WIDTH = 31
GLA_HEADS = 4
GLA_DK = 32
GLA_DV = 64
GLA_RANK = 16
GLA_TAU = 16.0
EPS = 1e-6
NEG = -1e30
LOG2E = 1.4426950408889634

LANES = 128
SUBLANES = 8
ROW_TILE = 512
ATT_TILE = 512
BLOCKS_PER_TILE = ATT_TILE // MOBA_BLOCK
DIL_GROUPS_BACK = max(w for w, _ in DIL_PATTERNS) // ATT_TILE
ALIBI_PIECES = 4
SEL_LANE0 = 16
MASK_BIAS = 2.0 ** 100
M_INIT = -1e29
GLA_CHUNK = 128
GLA_LEVELS = 7
CONV_HALO = 32
LR_PAD = 128
VMEM_LIMIT = 56 * 1024 * 1024

WT_ROWS = 4 * BRANCH
ZC_COLS = 3 * BRANCH
ZD_COLS = 2 * GLA_HEADS * GLA_DK + 2 * BRANCH + LR_PAD
WN_COLS = 4 * BRANCH + ZC_COLS + ZD_COLS

_NT = (((1,), (1,)), ((), ()))
_TN = (((0,), (0,)), ((), ()))


def _params(n_grid):
    return pltpu.CompilerParams(dimension_semantics=("arbitrary",) * n_grid,
                                vmem_limit_bytes=VMEM_LIMIT)


def _silu(x):
    return x * jax.nn.sigmoid(x)


def _group_mean_sq(z, bd):
    z2 = z * z
    hi = z2.astype(BF16)
    lo = (z2 - hi.astype(F32)).astype(BF16)
    return (jnp.dot(hi, bd, preferred_element_type=F32)
            + jnp.dot(lo, bd, preferred_element_type=F32))


def _inproj_kernel(tiles_per_seq, x_ref, ng_ref, wn_ref, wt_ref, kgain_ref, qgain_ref, bd_ref, kx_ref,
                   qat_ref, ka_ref, vat_ref, ga_ref, sel_ref,
                   qbt_ref, kb_ref, vbt_ref, gb_ref, zc_ref, zd_ref, km_buf):
    tile = pl.program_id(0) % tiles_per_seq

    @pl.when(tile == 0)
    def _():
        km_buf[...] = jnp.zeros_like(km_buf)

    x = x_ref[...]
    ms = jnp.mean(x * x, axis=-1, keepdims=True)
    h = (x * lax.rsqrt(ms + EPS) * ng_ref[...]).astype(BF16)
    bd = bd_ref[...]
    kx = kx_ref[...]

    def proj(c0, width):
        return jnp.dot(h, wn_ref[:, c0:c0 + width], preferred_element_type=F32)

    def proj_t(r0):
        return lax.dot_general(wt_ref[r0:r0 + BRANCH, :], h, _NT, preferred_element_type=F32)

    def head_norm(z, row):
        return z * lax.rsqrt(_group_mean_sq(z, bd) + EPS) * kgain_ref[row:row + 1, :]

    def head_norm_t(zt, idx):
        parts = []
        for g in range(N_HEADS):
            part = zt[g * HEAD_DIM:(g + 1) * HEAD_DIM]
            parts.append(part * lax.rsqrt(jnp.mean(part * part, axis=0, keepdims=True) + EPS))
        return jnp.concatenate(parts, axis=0) * qgain_ref[idx]

    def store_keys(ref, kn):
        for hp in range(BRANCH // LANES):
            ref[:, 2 * hp * LANES:(2 * hp + 1) * LANES] = kn[:, hp * LANES:(hp + 1) * LANES].astype(BF16)
            ref[:, (2 * hp + 1) * LANES:(2 * hp + 2) * LANES] = kx

    qa = head_norm_t(proj_t(0), 0)
    ka = head_norm(proj(0, BRANCH), 0)
    store_keys(ka_ref, ka)
    vat_ref[0] = proj_t(BRANCH).astype(BF16)
    ga_ref[...] = proj(BRANCH, BRANCH)

    for blk in range(BLOCKS_PER_TILE):
        km_buf[pl.ds(tile * BLOCKS_PER_TILE + blk, 1), :] = jnp.mean(
            ka[blk * MOBA_BLOCK:(blk + 1) * MOBA_BLOCK], axis=0, keepdims=True)
    km = km_buf[...]
    n_blk = km.shape[0]
    blk = lax.broadcasted_iota(jnp.int32, (n_blk, ROW_TILE), 0)
    own = (tile * BLOCKS_PER_TILE
           + lax.broadcasted_iota(jnp.int32, (n_blk, ROW_TILE), 1) // MOBA_BLOCK)
    blk_f = blk.astype(F32)
    past = blk < own
    for head, qh in enumerate(_head_operands(qa)):
        gate = jnp.dot(km[:, (head // 2) * LANES:(head // 2 + 1) * LANES], qh,
                       precision=lax.Precision.HIGHEST, preferred_element_type=F32)
        keep = (past & _top_k_rows(jnp.where(past, gate, -jnp.inf), blk_f)) | (blk == own)
        sel_ref[0, head] = jnp.where(keep, 0.0, -MASK_BIAS).astype(BF16)

    qat_ref[0] = (qa * (HEAD_DIM ** -0.5 * LOG2E)).astype(BF16)
    qbt_ref[0] = (head_norm_t(proj_t(2 * BRANCH), 1) * (HEAD_DIM ** -0.5 * LOG2E)).astype(BF16)
    store_keys(kb_ref, head_norm(proj(2 * BRANCH, BRANCH), 1))
    vbt_ref[0] = proj_t(3 * BRANCH).astype(BF16)
    gb_ref[...] = proj(3 * BRANCH, BRANCH)

    zc_ref[...] = proj(4 * BRANCH, ZC_COLS)
    zd_ref[...] = proj(4 * BRANCH + ZC_COLS, ZD_COLS)


def _inproj(x2, ng, wn_all, wt_all, layer, kgains, qgains, bd, kx, seq):
    n = x2.shape[0]
    nt = n // ROW_TILE
    per_seq = seq // ROW_TILE
    row = lambda i: (i, 0)
    const = lambda i: (0, 0)

    def nat(cols, dtype):
        return (jax.ShapeDtypeStruct((n, cols), dtype), pl.BlockSpec((ROW_TILE, cols), row))

    def tr(dtype):
        return (jax.ShapeDtypeStruct((nt, BRANCH, ROW_TILE), dtype),
                pl.BlockSpec((1, BRANCH, ROW_TILE), lambda i: (i, 0, 0)))

    n_blk = seq // MOBA_BLOCK
    choice = (jax.ShapeDtypeStruct((nt, N_HEADS, n_blk, ROW_TILE), BF16),
              pl.BlockSpec((1, N_HEADS, n_blk, ROW_TILE), lambda i: (i, 0, 0, 0)))
    outs = [tr(BF16), nat(2 * BRANCH, BF16), tr(BF16), nat(BRANCH, F32), choice,
            tr(BF16), nat(2 * BRANCH, BF16), tr(BF16), nat(BRANCH, F32),
            nat(ZC_COLS, F32), nat(ZD_COLS, F32)]
    return pl.pallas_call(
        functools.partial(_inproj_kernel, per_seq),
        grid=(nt,),
        in_specs=[pl.BlockSpec((ROW_TILE, D_MODEL), row),
                  pl.BlockSpec((1, D_MODEL), const),
                  pl.BlockSpec((None, D_MODEL, WN_COLS), lambda i: (layer, 0, 0)),
                  pl.BlockSpec((None, WT_ROWS, D_MODEL), lambda i: (layer, 0, 0)),
                  pl.BlockSpec((2, BRANCH), const),
                  pl.BlockSpec((2, BRANCH, 1), lambda i: (0, 0, 0)),
                  pl.BlockSpec((BRANCH, BRANCH), const),
                  pl.BlockSpec((ROW_TILE, LANES), lambda i: (i % per_seq, 0))],
        out_specs=[o[1] for o in outs],
        out_shape=[o[0] for o in outs],
        scratch_shapes=[pltpu.VMEM((n_blk, BRANCH), F32)],
        compiler_params=_params(1),
        name="inproj",
    )(x2, ng, wn_all, wt_all, kgains, qgains, bd, kx)


def _key_position_lanes(seq):
    pos = np.arange(seq)
    c, n = pos % MOBA_BLOCK, pos // MOBA_BLOCK
    kx = np.zeros((seq, LANES), np.float32)
    p = ALIBI_PIECES
    kx[:, 0:p] = (c // 16)[:, None]
    kx[:, p:2 * p] = (c % 16)[:, None]
    kx[:, 2 * p:3 * p] = n[:, None]
    kx[pos, SEL_LANE0 + n] = 1.0
    return jnp.asarray(kx, BF16)


def _query_alibi_rows(slopes):
    pieces, rest = [], LOG2E
    for _ in range(ALIBI_PIECES):
        piece = float(np.asarray(rest, dtype=BF16).astype(np.float64))
        pieces.append(piece)
        rest -= piece
    p = ALIBI_PIECES
    qx = np.zeros((len(slopes), LANES, ATT_TILE), np.float32)
    for h, slope in enumerate(slopes):
        for g, weight in enumerate((16.0, 1.0, float(MOBA_BLOCK))):
            qx[h, g * p:(g + 1) * p, :] = np.asarray([weight * slope * piece for piece in pieces])[:, None]
    return jnp.asarray(qx, F32)


class _SweepOps:
    def __init__(self, k_ref, vt_ref, qft_buf, s_bufs, p_bufs, a_bufs, smax_bufs, m_buf, acc_buf):
        self.k_ref, self.vt_ref, self.qft_buf = k_ref, vt_ref, qft_buf
        self.s_bufs, self.p_bufs, self.a_bufs, self.smax_bufs = s_bufs, p_bufs, a_bufs, smax_bufs
        self.m_buf, self.acc_buf = m_buf, acc_buf

    def init(self):
        self.m_buf[...] = jnp.full(self.m_buf.shape, M_INIT, F32)
        self.acc_buf[...] = jnp.zeros(self.acc_buf.shape, F32)

    def issue_scores(self, group, x, bias=None):
        rows = pl.ds(pl.multiple_of(group * ATT_TILE, ATT_TILE), ATT_TILE)
        keys = [self.k_ref[rows, 2 * pair * LANES:2 * (pair + 1) * LANES]
                for pair in range(N_HEADS // 2)]
        for h in range(N_HEADS):
            s = jnp.dot(keys[h // 2], self.qft_buf[h], preferred_element_type=F32)
            if bias is None:
                self.smax_bufs[x][h] = jnp.max(s, axis=0, keepdims=True)
            else:
                s = s + bias
            self.s_bufs[x][h] = s

    def softmax(self, x, mask=None, issued_max=True):
        for h in range(N_HEADS):
            s = self.s_bufs[x][h]
            if mask is not None:
                s = mask(s)
            if mask is None and issued_max:
                group_max = self.smax_bufs[x][h]
            else:
                group_max = jnp.max(s, axis=0, keepdims=True)
            m_old = self.m_buf[h]
            m_new = jnp.maximum(m_old, group_max)
            self.m_buf[h] = m_new
            self.a_bufs[x][h] = jnp.exp2(m_old - m_new)
            self.p_bufs[x][h] = jnp.exp2(s - m_new).astype(BF16)

    def fold_values(self, group, x):
        ones = jnp.ones((SUBLANES, ATT_TILE), BF16)
        vt = self.vt_ref[group]
        for h in range(N_HEADS):
            lhs = jnp.concatenate([vt[h * HEAD_DIM:(h + 1) * HEAD_DIM, :], ones], axis=0)
            self.acc_buf[h] = (self.a_bufs[x][h] * self.acc_buf[h]
                               + jnp.dot(lhs, self.p_bufs[x][h], preferred_element_type=F32))

    def finish(self, g_ref, o_ref):
        acc = self.acc_buf
        out_t = jnp.concatenate(
            [acc[h, 0:HEAD_DIM, :] / acc[h, HEAD_DIM:HEAD_DIM + 1, :] for h in range(N_HEADS)], axis=0)
        o_ref[...] = out_t.T * _silu(g_ref[...])


def _flash_sweep(ops, n_steps, last_mask, g_ref, o_ref):
    def regular_step(t, x):
        ops.issue_scores(t + 1, 1 - x)
        ops.fold_values(jnp.maximum(t - 1, 0), 1 - x)
        ops.softmax(x)

    def neutral_fold(x):
        ops.a_bufs[x][...] = jnp.ones(ops.a_bufs[x].shape, F32)
        ops.p_bufs[x][...] = jnp.zeros(ops.p_bufs[x].shape, BF16)

    ops.init()
    n_regular = n_steps - 1
    odd = n_regular % 2

    @pl.when(odd == 1)
    def _():
        neutral_fold(0)
        ops.issue_scores(0, 1)
        regular_step(0, 1)

    @pl.when(odd == 0)
    def _():
        neutral_fold(1)
        ops.issue_scores(0, 0)

    def step_pair(u, _):
        t = odd + 2 * u
        regular_step(t, 0)
        regular_step(t + 1, 1)
        return 0

    lax.fori_loop(0, n_regular // 2, step_pair, 0)
    ops.fold_values(jnp.maximum(n_steps - 2, 0), 1)
    ops.softmax(0, last_mask)
    ops.fold_values(n_steps - 1, 0)
    ops.finish(g_ref, o_ref)


def _head_operands(qt):
    first = lax.broadcasted_iota(jnp.int32, (LANES, qt.shape[1]), 0) < HEAD_DIM
    zero = jnp.zeros((), qt.dtype)
    out = []
    for h in range(N_HEADS):
        pair = qt[(h // 2) * LANES:(h // 2 + 1) * LANES]
        out.append(jnp.where(first, pair, zero) if h % 2 == 0 else jnp.where(first, zero, pair))
    return out


def _attn_scratch():
    stat = pltpu.VMEM((N_HEADS, 1, ATT_TILE), F32)
    return ([pltpu.VMEM((N_HEADS, 2 * LANES, ATT_TILE), BF16)]
            + [pltpu.VMEM((N_HEADS, ATT_TILE, ATT_TILE), F32)] * 2
            + [pltpu.VMEM((N_HEADS, ATT_TILE, ATT_TILE), BF16)] * 2
            + [stat, stat]
            + [stat, stat]
            + [stat]
            + [pltpu.VMEM((N_HEADS, HEAD_DIM + SUBLANES, ATT_TILE), F32)])


def _attn_specs(seq):
    nq = seq // ATT_TILE
    q_tile = pl.BlockSpec((1, BRANCH, ATT_TILE), lambda b, i: (b * nq + i, 0, 0))
    keys = pl.BlockSpec((seq, 2 * BRANCH), lambda b, i: (b, 0))
    values = pl.BlockSpec((nq, BRANCH, ATT_TILE), lambda b, i: (b, 0, 0))
    gate = pl.BlockSpec((ATT_TILE, BRANCH), lambda b, i: (b * nq + i, 0))
    qx = pl.BlockSpec((N_HEADS, LANES, ATT_TILE), lambda b, i: (0, 0, 0),
                      pipeline_mode=pl.Buffered(1))
    return nq, q_tile, keys, values, gate, qx


def _top_k_rows(gate, row_f):
    sel = jnp.zeros(gate.shape, F32)
    for _ in range(MOBA_TOPK):
        top = jnp.max(gate, axis=0, keepdims=True)
        first = jnp.min(jnp.where(gate == top, row_f, 1e9), axis=0, keepdims=True)
        pick = row_f == first
        sel = jnp.where(pick, 1.0, sel)
        gate = jnp.where(pick, -jnp.inf, gate)
    return sel > 0.5


def _moba_kernel(qt_ref, k_ref, vt_ref, sel_ref, g_ref, qx_ref, o_ref, qft_buf, *bufs):
    i = pl.program_id(1)
    n_blk = sel_ref.shape[2]
    unused = jnp.zeros((LANES - SEL_LANE0 - n_blk, ATT_TILE), BF16)
    for h, qh in enumerate(_head_operands(qt_ref[0])):
        qft_buf[h] = jnp.concatenate(
            [qh, qx_ref[h, 0:SEL_LANE0, :].astype(BF16), sel_ref[0, h], unused], axis=0)

    causal = (lax.broadcasted_iota(jnp.int32, (ATT_TILE, ATT_TILE), 0)
              <= lax.broadcasted_iota(jnp.int32, (ATT_TILE, ATT_TILE), 1))
    ops = _SweepOps(k_ref, vt_ref, qft_buf, bufs[0:2], bufs[2:4], bufs[4:6], bufs[6:8], *bufs[8:])
    _flash_sweep(ops, i + 1, lambda s: jnp.where(causal, s, NEG), g_ref, o_ref)


def _moba(qt, k, vt, sel, g, qx, batch, seq):
    nq, q_tile, keys, values, gate, qx_spec = _attn_specs(seq)
    n_blk = seq // MOBA_BLOCK
    assert SEL_LANE0 + n_blk <= LANES and 3 * ALIBI_PIECES <= SEL_LANE0
    return pl.pallas_call(
        _moba_kernel,
        grid=(batch, nq),
        in_specs=[q_tile, keys, values,
                  pl.BlockSpec((1, N_HEADS, n_blk, ATT_TILE), lambda b, i: (b * nq + i, 0, 0, 0)),
                  gate, qx_spec],
        out_specs=gate,
        out_shape=jax.ShapeDtypeStruct(g.shape, F32),
        scratch_shapes=_attn_scratch(),
        compiler_params=_params(2),
        name="moba",
    )(qt, k, vt, sel, g, qx)


def _dilated_multiplicity_table():
    idx = np.arange(ATT_TILE)
    delta = (np.arange(DIL_GROUPS_BACK + 1)[:, None, None] * ATT_TILE
             + idx[None, None, :] - idx[None, :, None])
    mult = np.zeros(delta.shape, np.float64)
    for window, dil in DIL_PATTERNS:
        mult += (delta >= 0) & (delta <= window) & (delta % dil == 0)
    table = np.where(mult > 0, np.log2(np.maximum(mult, 1.0)), NEG)
    return jnp.asarray(np.concatenate([table, np.full_like(table[:1], NEG)]), F32)


def _dilated_kernel(qt_ref, qt_next_ref, k_ref, vt_ref, t_ref, g_ref, qx_ref, o_ref, qft_buf, *bufs):
    i = pl.program_id(1)
    n_steps = DIL_GROUPS_BACK + 1
    ops = _SweepOps(k_ref, vt_ref, qft_buf, bufs[0:2], bufs[2:4], bufs[4:6], bufs[6:8], *bufs[8:])

    def group_of(tile, t):
        return jnp.where(t <= tile, tile - t, 0)

    def table_of(tile, t):
        return t_ref[jnp.where(t <= tile, t, n_steps)]

    def prepare(tile, q_ref):
        for h, qh in enumerate(_head_operands(q_ref[0])):
            qft_buf[h] = jnp.concatenate([qh, qx_ref[h].astype(BF16)], axis=0)
        ops.issue_scores(tile, 0, table_of(tile, 0))

    ops.init()

    @pl.when(i == 0)
    def _():
        prepare(i, qt_ref)

    for t in range(n_steps):
        x = t % 2
        if t + 1 < n_steps:
            ops.issue_scores(group_of(i, t + 1), 1 - x, table_of(i, t + 1))
        if t >= 1:
            ops.fold_values(group_of(i, t - 1), 1 - x)
        ops.softmax(x, issued_max=False)
    prepare(jnp.minimum(i + 1, pl.num_programs(1) - 1), qt_next_ref)
    ops.fold_values(group_of(i, n_steps - 1), (n_steps - 1) % 2)
    ops.finish(g_ref, o_ref)


def _dilated(qt, k, vt, table, g, qx, batch, seq):
    nq, q_tile, keys, values, gate, qx_spec = _attn_specs(seq)
    assert DIL_GROUPS_BACK % 2 == 0
    q_next = pl.BlockSpec((1, BRANCH, ATT_TILE),
                          lambda b, i: (b * nq + jnp.minimum(i + 1, nq - 1), 0, 0))
    return pl.pallas_call(
        _dilated_kernel,
        grid=(batch, nq),
        in_specs=[q_tile, q_next, keys, values,
                  pl.BlockSpec(table.shape, lambda b, i: (0, 0, 0), pipeline_mode=pl.Buffered(1)),
                  gate, qx_spec],
        out_specs=gate,
        out_shape=jax.ShapeDtypeStruct(g.shape, F32),
        scratch_shapes=_attn_scratch(),
        compiler_params=_params(2),
        name="dilated",
    )(qt, qt, k, vt, table, g, qx)


def _conv_module(z, halo, has_history, w_ref, b_ref, lng_ref, lnb_ref, pw_ref, pwb_ref, u_buf):
    def glu(z):
        return z[:, 0:BRANCH] * jax.nn.sigmoid(z[:, BRANCH:2 * BRANCH])

    u_buf[0, 0:CONV_HALO, :] = jnp.where(has_history, glu(halo), 0.0)
    u_buf[0, CONV_HALO:, :] = glu(z)
    shifted = CONV_HALO + ROW_TILE - SUBLANES
    for phase in range(1, SUBLANES):
        u_buf[phase, 0:shifted, :] = u_buf[0, phase:phase + shifted, :]
    acc = jnp.zeros((ROW_TILE, BRANCH), F32) + b_ref[...]
    first = CONV_HALO - (CONV_WIDTH - 1)
    for tap in range(CONV_WIDTH):
        phase, start = (first + tap) % SUBLANES, (first + tap) // SUBLANES * SUBLANES
        acc = acc + w_ref[tap:tap + 1, :] * u_buf[phase, start:start + ROW_TILE, :]
    mu = jnp.mean(acc, axis=-1, keepdims=True)
    cen = acc - mu
    var = jnp.mean(cen * cen, axis=-1, keepdims=True)
    un = cen * lax.rsqrt(var + EPS) * lng_ref[...] + lnb_ref[...]
    y = jnp.dot(_silu(un).astype(BF16), pw_ref[...], preferred_element_type=F32) + pwb_ref[...]
    return y * _silu(z[:, 2 * BRANCH:3 * BRANCH])


def _gla_sum_matrices():
    c = GLA_CHUNK
    i = np.arange(c)[:, None]
    t = np.arange(c)[None, :]
    mats = [t <= i]
    for l in range(GLA_LEVELS):
        h = (c // 2) >> l
        mid = (i // (2 * h)) * (2 * h) + h
        later = (i & h) != 0
        mats.append((later & (t >= mid) & (t <= i)) | (~later & (t > i) & (t < mid)))
    return jnp.asarray(np.concatenate(mats, axis=0), BF16)


def _gla_tile(z_ref, sums_ref, wg_ref, bg_ref, gn_ref, bd_ref, o_ref, state_ref):
    c = GLA_CHUNK
    nh = GLA_HEADS
    kw = nh * GLA_DK
    vw = nh * GLA_DV

    row = lax.broadcasted_iota(jnp.int32, (c, kw), 0)
    qi = lax.broadcasted_iota(jnp.int32, (c, nh * c), 0)
    kj = lax.broadcasted_iota(jnp.int32, (c, nh * c), 1) % c
    level_mask = [(qi >> (GLA_LEVELS - l)) == (kj >> (GLA_LEVELS - l)) for l in range(GLA_LEVELS)]
    diag_mask = qi == kj
    k_head = (lax.broadcasted_iota(jnp.int32, (nh * c, kw), 0) // c
              == lax.broadcasted_iota(jnp.int32, (nh * c, kw), 1) // GLA_DK)
    v_head = (lax.broadcasted_iota(jnp.int32, (nh * c, vw), 0) // c
              == lax.broadcasted_iota(jnp.int32, (nh * c, vw), 1) // GLA_DV)
    s_head = (lax.broadcasted_iota(jnp.int32, (vw, kw), 0) // GLA_DV
              == lax.broadcasted_iota(jnp.int32, (vw, kw), 1) // GLA_DK)

    k_head_bf = jnp.where(k_head, 1.0, 0.0).astype(BF16)

    def per_head_keys(kt):
        return jnp.concatenate([kt.astype(BF16)] * nh, axis=0) * k_head_bf

    def chunk_of(b, rows):
        q = z_ref[b, rows, 0:kw] * GLA_DK ** -0.5
        k = z_ref[b, rows, kw:2 * kw]
        v = z_ref[b, rows, 2 * kw:2 * kw + vw]
        gd = z_ref[b, rows, 2 * kw + vw:2 * kw + 2 * vw]
        lr = z_ref[b, rows, 2 * kw + 2 * vw:2 * kw + 2 * vw + LR_PAD]

        g = jnp.dot(lr.astype(BF16), wg_ref[...], preferred_element_type=F32) + bg_ref[...]
        la = (jnp.minimum(g, 0.0) - jnp.log(1.0 + jnp.exp(-jnp.abs(g)))) / GLA_TAU
        a1 = la.astype(BF16)
        a2 = (la - a1.astype(F32)).astype(BF16)
        parts = jnp.dot(sums_ref[...], jnp.concatenate([a1, a2], axis=1),
                        preferred_element_type=F32)
        sums = parts[:, 0:kw] + parts[:, kw:2 * kw]
        bc = sums[0:c]

        attn = jnp.where(diag_mask,
                         lax.dot_general(q.astype(BF16), per_head_keys(k), _NT,
                                         preferred_element_type=F32), 0.0)
        for l in range(GLA_LEVELS):
            later = (row & ((c // 2) >> l)) != 0
            scaled = jnp.where(later, q, k) * jnp.exp(sums[(1 + l) * c:(2 + l) * c])
            qt = jnp.where(later, scaled, 0.0).astype(BF16)
            a = lax.dot_general(qt, per_head_keys(jnp.where(later, 0.0, scaled)), _NT,
                                preferred_element_type=F32)
            attn = attn + (a if l == 0 else jnp.where(level_mask[l], a, 0.0))

        vb = v.astype(BF16)
        v_stack = jnp.where(v_head, jnp.concatenate([vb] * nh, axis=0), jnp.zeros((), BF16))
        o = jnp.dot(attn.astype(BF16), v_stack, preferred_element_type=F32)

        state = state_ref[b]
        o = o + lax.dot_general((q * jnp.exp(bc)).astype(BF16), state.astype(BF16), _NT,
                                preferred_element_type=F32)
        b_last = bc[c - 1:c, :]
        k_dec = (k * jnp.exp(b_last - bc)).astype(BF16)
        upd = lax.dot_general(vb, k_dec, _TN, preferred_element_type=F32)
        state_ref[b] = state * jnp.exp(b_last) + jnp.where(s_head, upd, 0.0)

        on = o * lax.rsqrt(_group_mean_sq(o, bd_ref[...]) + EPS) * gn_ref[...]
        o_ref[b, rows, :] = on * _silu(gd)

    for ci in range(ROW_TILE // c):
        for b in range(z_ref.shape[0]):
            chunk_of(b, pl.ds(ci * c, c))


def _tail_kernel(x_ref, ya_ref, yb_ref, zc_ref, halo_ref, zd_ref,
                 cw_ref, cb_ref, lng_ref, lnb_ref, pw_ref, pwb_ref,
                 sums_ref, wg_ref, bg_ref, gn_ref, bd_ref, wo_ref,
                 o_ref, u_buf, yd_buf, state_ref):
    j = pl.program_id(0)

    @pl.when(j == 0)
    def _():
        state_ref[...] = jnp.zeros_like(state_ref)

    def project(y, g):
        return jnp.dot(y.astype(BF16), wo_ref[g * BRANCH:(g + 1) * BRANCH, :],
                       preferred_element_type=F32)

    _gla_tile(zd_ref, sums_ref, wg_ref, bg_ref, gn_ref, bd_ref, yd_buf, state_ref)
    for b in range(x_ref.shape[0]):
        yc = _conv_module(zc_ref[b], halo_ref[b], j > 0, cw_ref, cb_ref, lng_ref, lnb_ref,
                          pw_ref, pwb_ref, u_buf.at[b])
        o_ref[b] = (x_ref[b] + project(ya_ref[b], 0) + project(yb_ref[b], 1) + project(yc, 2)
                    + project(yd_buf[b], 3))


def _tail(x2, ya, yb, zc, zd, conv_consts, gla_consts, w_out_all, layer, batch, seq):
    per = ROW_TILE // CONV_HALO
    tile = lambda cols: pl.BlockSpec((batch, ROW_TILE, cols), lambda j: (0, j, 0))
    whole = lambda a: pl.BlockSpec(a.shape, lambda j: (0,) * a.ndim, pipeline_mode=pl.Buffered(1))
    by_seq = lambda a: a.reshape(batch, seq, a.shape[-1])
    consts = (*conv_consts, *gla_consts, w_out_all)
    w_out_spec = pl.BlockSpec((None,) + w_out_all.shape[1:], lambda j: (layer, 0, 0),
                              pipeline_mode=pl.Buffered(1))
    out = pl.pallas_call(
        _tail_kernel,
        grid=(seq // ROW_TILE,),
        in_specs=[tile(D_MODEL), tile(BRANCH), tile(BRANCH), tile(ZC_COLS),
                  pl.BlockSpec((batch, CONV_HALO, ZC_COLS),
                               lambda j: (0, jnp.maximum(j * per - 1, 0), 0)),
                  tile(ZD_COLS)] + [whole(a) for a in consts[:-1]] + [w_out_spec],
        out_specs=tile(D_MODEL),
        out_shape=jax.ShapeDtypeStruct((batch, seq, D_MODEL), F32),
        scratch_shapes=[pltpu.VMEM((batch, SUBLANES, CONV_HALO + ROW_TILE, BRANCH), F32),
                        pltpu.VMEM((batch, ROW_TILE, BRANCH), F32),
                        pltpu.VMEM((batch, GLA_HEADS * GLA_DV, GLA_HEADS * GLA_DK), F32)],
        compiler_params=_params(1),
        name="tail",
    )(by_seq(x2), by_seq(ya), by_seq(yb), by_seq(zc), by_seq(zc), by_seq(zd), *consts)
    return out.reshape(batch * seq, D_MODEL)


def _pack_w_in(w_in):
    col = lambda j: w_in[..., j * BRANCH:(j + 1) * BRANCH]
    gla0 = 11 * BRANCH
    qkv = 2 * GLA_HEADS * GLA_DK + BRANCH
    pad = jnp.zeros(w_in.shape[:-1] + (LR_PAD - GLA_RANK,), w_in.dtype)
    wn = jnp.concatenate([col(1), col(3), col(5), col(7), w_in[..., 8 * BRANCH:gla0 + qkv],
                          w_in[..., gla0 + qkv + GLA_RANK:], w_in[..., gla0 + qkv:gla0 + qkv + GLA_RANK],
                          pad], axis=-1)
    wt = jnp.swapaxes(jnp.concatenate([col(0), col(2), col(4), col(6)], axis=-1), -1, -2)
    return wn.astype(BF16), wt.astype(BF16)


def _layer(x2, batch, seq, consts, layer, big_weights, norm_g, q_gain_a, k_gain_a, q_gain_b, k_gain_b,
           conv_w, conv_b, conv_ln_g, conv_ln_b, conv_pw_w, conv_pw_b, gla_gate_w, gla_gate_b, gla_norm_g):
    bd, kx, qx_moba, qx_dil, dil_table, gla_sums = consts
    wn_all, wt_all, w_out_all = big_weights
    kgains = jnp.stack([jnp.tile(k_gain_a, N_HEADS), jnp.tile(k_gain_b, N_HEADS)])
    qgains = jnp.stack([jnp.tile(q_gain_a, N_HEADS), jnp.tile(q_gain_b, N_HEADS)])[:, :, None]
    (qat, ka, vat, ga, sel, qbt, kb, vbt, gb, zc, zd) = _inproj(
        x2, norm_g[None, :], wn_all, wt_all, layer, kgains, qgains, bd, kx, seq)
    ya = _moba(qat, ka, vat, sel, ga, qx_moba, batch, seq)
    yb = _dilated(qbt, kb, vbt, dil_table, gb, qx_dil, batch, seq)
    wg = jnp.concatenate([gla_gate_w, jnp.zeros((LR_PAD - GLA_RANK, gla_gate_w.shape[1]), F32)],
                         axis=0).astype(BF16)
    conv_consts = (conv_w, conv_b[None, :], conv_ln_g[None, :], conv_ln_b[None, :],
                   conv_pw_w.astype(BF16), conv_pw_b[None, :])
    gla_consts = (gla_sums, wg, gla_gate_b[None, :], jnp.tile(gla_norm_g, GLA_HEADS)[None, :], bd)
    return _tail(x2, ya, yb, zc, zd, conv_consts, gla_consts, w_out_all, layer, batch, seq)


def kernel(x, norm_g, w_in, q_gain_a, k_gain_a, q_gain_b, k_gain_b, conv_w, conv_b, conv_ln_g, conv_ln_b,
           conv_pw_w, conv_pw_b, gla_gate_w, gla_gate_b, gla_norm_g, w_out):
    batch, seq, d = x.shape
    assert d == D_MODEL and seq % ROW_TILE == 0 and ROW_TILE == ATT_TILE
    group = np.arange(BRANCH) // HEAD_DIM
    bd = jnp.asarray((group[:, None] == group[None, :]) / HEAD_DIM, BF16)
    heads = np.arange(N_HEADS)
    consts = (bd, _key_position_lanes(seq),
              _query_alibi_rows(2.0 ** -(1.0 + 2 * heads)),
              _query_alibi_rows(2.0 ** -(2.0 + 2 * heads)),
              _dilated_multiplicity_table(), _gla_sum_matrices())
    x2 = x.reshape(batch * seq, d)
    big_weights = (*_pack_w_in(w_in), w_out.astype(BF16))
    params = (norm_g, q_gain_a, k_gain_a, q_gain_b, k_gain_b, conv_w, conv_b, conv_ln_g,
              conv_ln_b, conv_pw_w, conv_pw_b, gla_gate_w, gla_gate_b, gla_norm_g)
    for layer in range(norm_g.shape[0]):
        x2 = _layer(x2, batch, seq, consts, layer, big_weights, *(p[layer] for p in params))
    return x2.reshape(batch, seq, d)
```

```python
import functools

import numpy as np
import jax
import jax.numpy as jnp
from jax import lax
from jax.experimental import pallas as pl
from jax.experimental.pallas import tpu as pltpu

F32 = jnp.float32
BF16 = jnp.bfloat16

D_MODEL = 1024
BRANCH = 256
HEAD_DIM = 64
N_HEADS = BRANCH // HEAD_DIM
MOBA_BLOCK = 256
MOBA_TOPK = 3
DIL_PATTERNS = ((128, 1), (512, 4), (2048, 16))
CONV_WIDTH = 31
GLA_HEADS = 4
GLA_DK = 32
GLA_DV = 64
GLA_RANK = 16
GLA_TAU = 16.0
EPS = 1e-6
NEG = -1e30
LOG2E = 1.4426950408889634

LANES = 128
SUBLANES = 8
ROW_TILE = 512
ATT_TILE = 512
BLOCKS_PER_TILE = ATT_TILE // MOBA_BLOCK
DIL_TILES_PER_STEP = 2
DIL_GROUPS_BACK = max(w for w, _ in DIL_PATTERNS) // ATT_TILE
ALIBI_PIECES = 4
SEL_LANE0 = 16
MASK_BIAS = 2.0 ** 100
M_INIT = -1e29
GLA_CHUNK = 128
GLA_LEVELS = 7
CONV_HALO = 32
LR_PAD = 128
VMEM_LIMIT = 56 * 1024 * 1024

WT_ROWS = 4 * BRANCH
ZC_COLS = 3 * BRANCH
ZD_COLS = 2 * GLA_HEADS * GLA_DK + 2 * BRANCH + LR_PAD
WN_COLS = 4 * BRANCH + ZC_COLS + ZD_COLS

_NT = (((1,), (1,)), ((), ()))
_TN = (((0,), (0,)), ((), ()))


def _params(n_grid):
    return pltpu.CompilerParams(dimension_semantics=("arbitrary",) * n_grid,
                                vmem_limit_bytes=VMEM_LIMIT)


def _silu(x):
    return x * jax.nn.sigmoid(x)


def _group_mean_sq(z, bd):
    z2 = z * z
    hi = z2.astype(BF16)
    lo = (z2 - hi.astype(F32)).astype(BF16)
    return (jnp.dot(hi, bd, preferred_element_type=F32)
            + jnp.dot(lo, bd, preferred_element_type=F32))


def _inproj_kernel(tiles_per_seq, x_ref, ng_ref, wn_ref, wt_ref, kgain_ref, qgain_ref, bd_ref, kx_ref,
                   qat_ref, ka_ref, vat_ref, ga_ref, sel_ref,
                   qbt_ref, kb_ref, vbt_ref, gb_ref, zc_ref, zd_ref, km_buf):
    tile = pl.program_id(0) % tiles_per_seq

    @pl.when(tile == 0)
    def _():
        km_buf[...] = jnp.zeros_like(km_buf)

    x = x_ref[...]
    ms = jnp.mean(x * x, axis=-1, keepdims=True)
    h = (x * lax.rsqrt(ms + EPS) * ng_ref[...]).astype(BF16)
    bd = bd_ref[...]
    kx = kx_ref[...]

    def proj(c0, width):
        return jnp.dot(h, wn_ref[:, c0:c0 + width], preferred_element_type=F32)

    def proj_t(r0):
        return lax.dot_general(wt_ref[r0:r0 + BRANCH, :], h, _NT, preferred_element_type=F32)

    def head_norm(z, row):
        return z * lax.rsqrt(_group_mean_sq(z, bd) + EPS) * kgain_ref[row:row + 1, :]

    def head_norm_t(zt, idx):
        parts = []
        for g in range(N_HEADS):
            part = zt[g * HEAD_DIM:(g + 1) * HEAD_DIM]
            parts.append(part * lax.rsqrt(jnp.mean(part * part, axis=0, keepdims=True) + EPS))
        return jnp.concatenate(parts, axis=0) * qgain_ref[idx]

    def store_keys(ref, kn):
        for hp in range(BRANCH // LANES):
            ref[:, 2 * hp * LANES:(2 * hp + 1) * LANES] = kn[:, hp * LANES:(hp + 1) * LANES].astype(BF16)
            ref[:, (2 * hp + 1) * LANES:(2 * hp + 2) * LANES] = kx

    qa = head_norm_t(proj_t(0), 0)
    ka = head_norm(proj(0, BRANCH), 0)
    store_keys(ka_ref, ka)
    vat_ref[0] = proj_t(BRANCH).astype(BF16)
    ga_ref[...] = proj(BRANCH, BRANCH)

    for blk in range(BLOCKS_PER_TILE):
        km_buf[pl.ds(tile * BLOCKS_PER_TILE + blk, 1), :] = jnp.mean(
            ka[blk * MOBA_BLOCK:(blk + 1) * MOBA_BLOCK], axis=0, keepdims=True)
    km = km_buf[...]
    n_blk = km.shape[0]
    blk = lax.broadcasted_iota(jnp.int32, (n_blk, ROW_TILE), 0)
    own = (tile * BLOCKS_PER_TILE
           + lax.broadcasted_iota(jnp.int32, (n_blk, ROW_TILE), 1) // MOBA_BLOCK)
    blk_f = blk.astype(F32)
    past = blk < own
    for head, qh in enumerate(_head_operands(qa)):
        gate = jnp.dot(km[:, (head // 2) * LANES:(head // 2 + 1) * LANES], qh,
                       precision=lax.Precision.HIGHEST, preferred_element_type=F32)
        keep = (past & _top_k_rows(jnp.where(past, gate, -jnp.inf), blk_f)) | (blk == own)
        sel_ref[0, head] = jnp.where(keep, 0.0, -MASK_BIAS).astype(BF16)

    qat_ref[0] = (qa * (HEAD_DIM ** -0.5 * LOG2E)).astype(BF16)
    qbt_ref[0] = (head_norm_t(proj_t(2 * BRANCH), 1) * (HEAD_DIM ** -0.5 * LOG2E)).astype(BF16)
    store_keys(kb_ref, head_norm(proj(2 * BRANCH, BRANCH), 1))
    vbt_ref[0] = proj_t(3 * BRANCH).astype(BF16)
    gb_ref[...] = proj(3 * BRANCH, BRANCH)

    zc_ref[...] = proj(4 * BRANCH, ZC_COLS)
    zd_ref[...] = proj(4 * BRANCH + ZC_COLS, ZD_COLS)


def _inproj(x2, ng, wn_all, wt_all, layer, kgains, qgains, bd, kx, seq):
    n = x2.shape[0]
    nt = n // ROW_TILE
    per_seq = seq // ROW_TILE
    row = lambda i: (i, 0)
    const = lambda i: (0, 0)

    def nat(cols, dtype):
        return (jax.ShapeDtypeStruct((n, cols), dtype), pl.BlockSpec((ROW_TILE, cols), row))

    def tr(dtype):
        return (jax.ShapeDtypeStruct((nt, BRANCH, ROW_TILE), dtype),
                pl.BlockSpec((1, BRANCH, ROW_TILE), lambda i: (i, 0, 0)))

    n_blk = seq // MOBA_BLOCK
    choice = (jax.ShapeDtypeStruct((nt, N_HEADS, n_blk, ROW_TILE), BF16),
              pl.BlockSpec((1, N_HEADS, n_blk, ROW_TILE), lambda i: (i, 0, 0, 0)))
    outs = [tr(BF16), nat(2 * BRANCH, BF16), tr(BF16), nat(BRANCH, F32), choice,
            tr(BF16), nat(2 * BRANCH, BF16), tr(BF16), nat(BRANCH, F32),
            nat(ZC_COLS, F32), nat(ZD_COLS, F32)]
    return pl.pallas_call(
        functools.partial(_inproj_kernel, per_seq),
        grid=(nt,),
        in_specs=[pl.BlockSpec((ROW_TILE, D_MODEL), row),
                  pl.BlockSpec((1, D_MODEL), const),
                  pl.BlockSpec((None, D_MODEL, WN_COLS), lambda i: (layer, 0, 0)),
                  pl.BlockSpec((None, WT_ROWS, D_MODEL), lambda i: (layer, 0, 0)),
                  pl.BlockSpec((2, BRANCH), const),
                  pl.BlockSpec((2, BRANCH, 1), lambda i: (0, 0, 0)),
                  pl.BlockSpec((BRANCH, BRANCH), const),
                  pl.BlockSpec((ROW_TILE, LANES), lambda i: (i % per_seq, 0))],
        out_specs=[o[1] for o in outs],
        out_shape=[o[0] for o in outs],
        scratch_shapes=[pltpu.VMEM((n_blk, BRANCH), F32)],
        compiler_params=_params(1),
        name="inproj",
    )(x2, ng, wn_all, wt_all, kgains, qgains, bd, kx)


def _key_position_lanes(seq):
    pos = np.arange(seq)
    c, n = pos % MOBA_BLOCK, pos // MOBA_BLOCK
    kx = np.zeros((seq, LANES), np.float32)
    p = ALIBI_PIECES
    kx[:, 0:p] = (c // 16)[:, None]
    kx[:, p:2 * p] = (c % 16)[:, None]
    kx[:, 2 * p:3 * p] = n[:, None]
    kx[pos, SEL_LANE0 + n] = 1.0
    return jnp.asarray(kx, BF16)


def _query_alibi_rows(slopes):
    pieces, rest = [], LOG2E
    for _ in range(ALIBI_PIECES):
        piece = float(np.asarray(rest, dtype=BF16).astype(np.float64))
        pieces.append(piece)
        rest -= piece
    p = ALIBI_PIECES
    qx = np.zeros((len(slopes), LANES, ATT_TILE), np.float32)
    for h, slope in enumerate(slopes):
        for g, weight in enumerate((16.0, 1.0, float(MOBA_BLOCK))):
            qx[h, g * p:(g + 1) * p, :] = np.asarray([weight * slope * piece for piece in pieces])[:, None]
    return jnp.asarray(qx, F32)


class _SweepOps:
    def __init__(self, k_ref, vt_ref, qft_buf, s_bufs, p_bufs, a_bufs, smax_bufs, m_buf, acc_buf):
        self.k_ref, self.vt_ref, self.qft_buf = k_ref, vt_ref, qft_buf
        self.s_bufs, self.p_bufs, self.a_bufs, self.smax_bufs = s_bufs, p_bufs, a_bufs, smax_bufs
        self.m_buf, self.acc_buf = m_buf, acc_buf

    def init(self):
        self.m_buf[...] = jnp.full(self.m_buf.shape, M_INIT, F32)
        self.acc_buf[...] = jnp.zeros(self.acc_buf.shape, F32)

    def issue_scores(self, group, x, bias=None):
        rows = pl.ds(pl.multiple_of(group * ATT_TILE, ATT_TILE), ATT_TILE)
        keys = [self.k_ref[rows, 2 * pair * LANES:2 * (pair + 1) * LANES]
                for pair in range(N_HEADS // 2)]
        for h in range(N_HEADS):
            s = jnp.dot(keys[h // 2], self.qft_buf[h], preferred_element_type=F32)
            if bias is None:
                self.smax_bufs[x][h] = jnp.max(s, axis=0, keepdims=True)
            else:
                s = s + bias
            self.s_bufs[x][h] = s

    def softmax(self, x, mask=None, issued_max=True):
        for h in range(N_HEADS):
            s = self.s_bufs[x][h]
            if mask is not None:
                s = mask(s)
            if mask is None and issued_max:
                group_max = self.smax_bufs[x][h]
            else:
                group_max = jnp.max(s, axis=0, keepdims=True)
            m_old = self.m_buf[h]
            m_new = jnp.maximum(m_old, group_max)
            self.m_buf[h] = m_new
            self.a_bufs[x][h] = jnp.exp2(m_old - m_new)
            self.p_bufs[x][h] = jnp.exp2(s - m_new).astype(BF16)

    def fold_values(self, group, x):
        ones = jnp.ones((SUBLANES, ATT_TILE), BF16)
        vt = self.vt_ref[group]
        for h in range(N_HEADS):
            lhs = jnp.concatenate([vt[h * HEAD_DIM:(h + 1) * HEAD_DIM, :], ones], axis=0)
            self.acc_buf[h] = (self.a_bufs[x][h] * self.acc_buf[h]
                               + jnp.dot(lhs, self.p_bufs[x][h], preferred_element_type=F32))

    def finish(self, g_ref, o_ref):
        acc = self.acc_buf
        out_t = jnp.concatenate(
            [acc[h, 0:HEAD_DIM, :] / acc[h, HEAD_DIM:HEAD_DIM + 1, :] for h in range(N_HEADS)], axis=0)
        o_ref[...] = out_t.T * _silu(g_ref[...])


def _flash_sweep(ops, n_steps, last_mask, g_ref, o_ref):
    def regular_step(t, x):
        ops.issue_scores(t + 1, 1 - x)
        ops.fold_values(jnp.maximum(t - 1, 0), 1 - x)
        ops.softmax(x)

    def neutral_fold(x):
        ops.a_bufs[x][...] = jnp.ones(ops.a_bufs[x].shape, F32)
        ops.p_bufs[x][...] = jnp.zeros(ops.p_bufs[x].shape, BF16)

    ops.init()
    n_regular = n_steps - 1
    odd = n_regular % 2

    @pl.when(odd == 1)
    def _():
        neutral_fold(0)
        ops.issue_scores(0, 1)
        regular_step(0, 1)

    @pl.when(odd == 0)
    def _():
        neutral_fold(1)
        ops.issue_scores(0, 0)

    def step_pair(u, _):
        t = odd + 2 * u
        regular_step(t, 0)
        regular_step(t + 1, 1)
        return 0

    lax.fori_loop(0, n_regular // 2, step_pair, 0)
    ops.fold_values(jnp.maximum(n_steps - 2, 0), 1)
    ops.softmax(0, last_mask)
    ops.fold_values(n_steps - 1, 0)
    ops.finish(g_ref, o_ref)


def _head_operands(qt):
    first = lax.broadcasted_iota(jnp.int32, (LANES, qt.shape[1]), 0) < HEAD_DIM
    zero = jnp.zeros((), qt.dtype)
    out = []
    for h in range(N_HEADS):
        pair = qt[(h // 2) * LANES:(h // 2 + 1) * LANES]
        out.append(jnp.where(first, pair, zero) if h % 2 == 0 else jnp.where(first, zero, pair))
    return out


def _attn_scratch():
    stat = pltpu.VMEM((N_HEADS, 1, ATT_TILE), F32)
    return ([pltpu.VMEM((N_HEADS, 2 * LANES, ATT_TILE), BF16)]
            + [pltpu.VMEM((N_HEADS, ATT_TILE, ATT_TILE), F32)] * 2
            + [pltpu.VMEM((N_HEADS, ATT_TILE, ATT_TILE), BF16)] * 2
            + [stat, stat]
            + [stat, stat]
            + [stat]
            + [pltpu.VMEM((N_HEADS, HEAD_DIM + SUBLANES, ATT_TILE), F32)])


def _attn_specs(seq):
    nq = seq // ATT_TILE
    q_tile = pl.BlockSpec((1, BRANCH, ATT_TILE), lambda b, i: (b * nq + i, 0, 0))
    keys = pl.BlockSpec((seq, 2 * BRANCH), lambda b, i: (b, 0))
    values = pl.BlockSpec((nq, BRANCH, ATT_TILE), lambda b, i: (b, 0, 0))
    gate = pl.BlockSpec((ATT_TILE, BRANCH), lambda b, i: (b * nq + i, 0))
    qx = pl.BlockSpec((N_HEADS, LANES, ATT_TILE), lambda b, i: (0, 0, 0),
                      pipeline_mode=pl.Buffered(1))
    return nq, q_tile, keys, values, gate, qx


def _top_k_rows(gate, row_f):
    sel = jnp.zeros(gate.shape, F32)
    for _ in range(MOBA_TOPK):
        top = jnp.max(gate, axis=0, keepdims=True)
        first = jnp.min(jnp.where(gate == top, row_f, 1e9), axis=0, keepdims=True)
        pick = row_f == first
        sel = jnp.where(pick, 1.0, sel)
        gate = jnp.where(pick, -jnp.inf, gate)
    return sel > 0.5


def _moba_kernel(qt_ref, k_ref, vt_ref, sel_ref, g_ref, qx_ref, o_ref, qft_buf, *bufs):
    i = pl.program_id(1)
    n_blk = sel_ref.shape[2]
    unused = jnp.zeros((LANES - SEL_LANE0 - n_blk, ATT_TILE), BF16)
    for h, qh in enumerate(_head_operands(qt_ref[0])):
        qft_buf[h] = jnp.concatenate(
            [qh, qx_ref[h, 0:SEL_LANE0, :].astype(BF16), sel_ref[0, h], unused], axis=0)

    causal = (lax.broadcasted_iota(jnp.int32, (ATT_TILE, ATT_TILE), 0)
              <= lax.broadcasted_iota(jnp.int32, (ATT_TILE, ATT_TILE), 1))
    ops = _SweepOps(k_ref, vt_ref, qft_buf, bufs[0:2], bufs[2:4], bufs[4:6], bufs[6:8], *bufs[8:])
    _flash_sweep(ops, i + 1, lambda s: jnp.where(causal, s, NEG), g_ref, o_ref)


def _moba(qt, k, vt, sel, g, qx, batch, seq):
    nq, q_tile, keys, values, gate, qx_spec = _attn_specs(seq)
    n_blk = seq // MOBA_BLOCK
    assert SEL_LANE0 + n_blk <= LANES and 3 * ALIBI_PIECES <= SEL_LANE0
    return pl.pallas_call(
        _moba_kernel,
        grid=(batch, nq),
        in_specs=[q_tile, keys, values,
                  pl.BlockSpec((1, N_HEADS, n_blk, ATT_TILE), lambda b, i: (b * nq + i, 0, 0, 0)),
                  gate, qx_spec],
        out_specs=gate,
        out_shape=jax.ShapeDtypeStruct(g.shape, F32),
        scratch_shapes=_attn_scratch(),
        compiler_params=_params(2),
        name="moba",
    )(qt, k, vt, sel, g, qx)


def _dilated_multiplicity_table():
    idx = np.arange(ATT_TILE)
    delta = (np.arange(DIL_GROUPS_BACK + 1)[:, None, None] * ATT_TILE
             + idx[None, None, :] - idx[None, :, None])
    mult = np.zeros(delta.shape, np.float64)
    for window, dil in DIL_PATTERNS:
        mult += (delta >= 0) & (delta <= window) & (delta % dil == 0)
    table = np.where(mult > 0, np.log2(np.maximum(mult, 1.0)), NEG)
    return jnp.asarray(np.concatenate([table, np.full_like(table[:1], NEG)]), F32)


def _dilated_kernel(qt_ref, qt_next_ref, k_ref, vt_ref, t_ref, g_ref, qx_ref, o_ref, qft_buf, *bufs):
    i = pl.program_id(1)
    n_steps = DIL_GROUPS_BACK + 1
    ops = _SweepOps(k_ref, vt_ref, qft_buf, bufs[0:2], bufs[2:4], bufs[4:6], bufs[6:8], *bufs[8:])

    def group_of(tile, t):
        return jnp.where(t <= tile, tile - t, 0)

    def table_of(tile, t):
        return t_ref[jnp.where(t <= tile, t, n_steps)]

    def prepare(tile, q):
        for h, qh in enumerate(_head_operands(q)):
            qft_buf[h] = jnp.concatenate([qh, qx_ref[h].astype(BF16)], axis=0)
        ops.issue_scores(tile, 0, table_of(tile, 0))

    @pl.when(i == 0)
    def _():
        prepare(0, qt_ref[0])

    n_tiles = pl.num_programs(1) * DIL_TILES_PER_STEP
    for sub in range(DIL_TILES_PER_STEP):
        tile = i * DIL_TILES_PER_STEP + sub
        rows = pl.ds(sub * ATT_TILE, ATT_TILE)
        ops.init()
        for t in range(n_steps):
            x = t % 2
            if t + 1 < n_steps:
                ops.issue_scores(group_of(tile, t + 1), 1 - x, table_of(tile, t + 1))
            if t >= 1:
                ops.fold_values(group_of(tile, t - 1), 1 - x)
            ops.softmax(x, issued_max=False)
        q_after = qt_ref[sub + 1] if sub + 1 < DIL_TILES_PER_STEP else qt_next_ref[0]
        prepare(jnp.minimum(tile + 1, n_tiles - 1), q_after)
        ops.fold_values(group_of(tile, n_steps - 1), (n_steps - 1) % 2)
        ops.finish(g_ref.at[rows], o_ref.at[rows])


def _dilated(qt, k, vt, table, g, qx, batch, seq):
    nq, _, keys, values, _, qx_spec = _attn_specs(seq)
    assert DIL_GROUPS_BACK % 2 == 0
    per = DIL_TILES_PER_STEP
    assert nq % per == 0
    ns = nq // per
    q_tiles = pl.BlockSpec((per, BRANCH, ATT_TILE), lambda b, i: (b * ns + i, 0, 0))
    q_next = pl.BlockSpec((1, BRANCH, ATT_TILE),
                          lambda b, i: (b * nq + jnp.minimum((i + 1) * per, nq - 1), 0, 0))
    gate = pl.BlockSpec((per * ATT_TILE, BRANCH), lambda b, i: (b * ns + i, 0))
    return pl.pallas_call(
        _dilated_kernel,
        grid=(batch, ns),
        in_specs=[q_tiles, q_next, keys, values,
                  pl.BlockSpec(table.shape, lambda b, i: (0, 0, 0), pipeline_mode=pl.Buffered(1)),
                  gate, qx_spec],
        out_specs=gate,
        out_shape=jax.ShapeDtypeStruct(g.shape, F32),
        scratch_shapes=_attn_scratch(),
        compiler_params=_params(2),
        name="dilated",
    )(qt, qt, k, vt, table, g, qx)


def _conv_module(z, halo, has_history, w_ref, b_ref, lng_ref, lnb_ref, pw_ref, pwb_ref, u_buf):
    def glu(z):
        return z[:, 0:BRANCH] * jax.nn.sigmoid(z[:, BRANCH:2 * BRANCH])

    u_buf[0, 0:CONV_HALO, :] = jnp.where(has_history, glu(halo), 0.0)
    u_buf[0, CONV_HALO:, :] = glu(z)
    shifted = CONV_HALO + ROW_TILE - SUBLANES
    for phase in range(1, SUBLANES):
        u_buf[phase, 0:shifted, :] = u_buf[0, phase:phase + shifted, :]
    acc = jnp.zeros((ROW_TILE, BRANCH), F32) + b_ref[...]
    first = CONV_HALO - (CONV_WIDTH - 1)
    for tap in range(CONV_WIDTH):
        phase, start = (first + tap) % SUBLANES, (first + tap) // SUBLANES * SUBLANES
        acc = acc + w_ref[tap:tap + 1, :] * u_buf[phase, start:start + ROW_TILE, :]
    mu = jnp.mean(acc, axis=-1, keepdims=True)
    cen = acc - mu
    var = jnp.mean(cen * cen, axis=-1, keepdims=True)
    un = cen * lax.rsqrt(var + EPS) * lng_ref[...] + lnb_ref[...]
    y = jnp.dot(_silu(un).astype(BF16), pw_ref[...], preferred_element_type=F32) + pwb_ref[...]
    return y * _silu(z[:, 2 * BRANCH:3 * BRANCH])


def _gla_sum_matrices():
    c = GLA_CHUNK
    i = np.arange(c)[:, None]
    t = np.arange(c)[None, :]
    mats = [t <= i]
    for l in range(GLA_LEVELS):
        h = (c // 2) >> l
        mid = (i // (2 * h)) * (2 * h) + h
        later = (i & h) != 0
        mats.append((later & (t >= mid) & (t <= i)) | (~later & (t > i) & (t < mid)))
    return jnp.asarray(np.concatenate(mats, axis=0), BF16)


def _gla_tile(z_ref, sums_ref, wg_ref, bg_ref, gn_ref, bd_ref, o_ref, state_ref):
    c = GLA_CHUNK
    nh = GLA_HEADS
    kw = nh * GLA_DK
    vw = nh * GLA_DV

    row = lax.broadcasted_iota(jnp.int32, (c, kw), 0)
    qi = lax.broadcasted_iota(jnp.int32, (c, nh * c), 0)
    kj = lax.broadcasted_iota(jnp.int32, (c, nh * c), 1) % c
    level_mask = [(qi >> (GLA_LEVELS - l)) == (kj >> (GLA_LEVELS - l)) for l in range(GLA_LEVELS)]
    diag_mask = qi == kj
    k_head = (lax.broadcasted_iota(jnp.int32, (nh * c, kw), 0) // c
              == lax.broadcasted_iota(jnp.int32, (nh * c, kw), 1) // GLA_DK)
    v_head = (lax.broadcasted_iota(jnp.int32, (nh * c, vw), 0) // c
              == lax.broadcasted_iota(jnp.int32, (nh * c, vw), 1) // GLA_DV)
    s_head = (lax.broadcasted_iota(jnp.int32, (vw, kw), 0) // GLA_DV
              == lax.broadcasted_iota(jnp.int32, (vw, kw), 1) // GLA_DK)

    k_head_bf = jnp.where(k_head, 1.0, 0.0).astype(BF16)

    def per_head_keys(kt):
        return jnp.concatenate([kt.astype(BF16)] * nh, axis=0) * k_head_bf

    def chunk_of(b, rows):
        q = z_ref[b, rows, 0:kw] * GLA_DK ** -0.5
        k = z_ref[b, rows, kw:2 * kw]
        v = z_ref[b, rows, 2 * kw:2 * kw + vw]
        gd = z_ref[b, rows, 2 * kw + vw:2 * kw + 2 * vw]
        lr = z_ref[b, rows, 2 * kw + 2 * vw:2 * kw + 2 * vw + LR_PAD]

        g = jnp.dot(lr.astype(BF16), wg_ref[...], preferred_element_type=F32) + bg_ref[...]
        la = (jnp.minimum(g, 0.0) - jnp.log(1.0 + jnp.exp(-jnp.abs(g)))) / GLA_TAU
        a1 = la.astype(BF16)
        a2 = (la - a1.astype(F32)).astype(BF16)
        parts = jnp.dot(sums_ref[...], jnp.concatenate([a1, a2], axis=1),
                        preferred_element_type=F32)
        sums = parts[:, 0:kw] + parts[:, kw:2 * kw]
        bc = sums[0:c]

        attn = jnp.where(diag_mask,
                         lax.dot_general(q.astype(BF16), per_head_keys(k), _NT,
                                         preferred_element_type=F32), 0.0)
        for l in range(GLA_LEVELS):
            later = (row & ((c // 2) >> l)) != 0
            scaled = jnp.where(later, q, k) * jnp.exp(sums[(1 + l) * c:(2 + l) * c])
            qt = jnp.where(later, scaled, 0.0).astype(BF16)
            a = lax.dot_general(qt, per_head_keys(jnp.where(later, 0.0, scaled)), _NT,
                                preferred_element_type=F32)
            attn = attn + (a if l == 0 else jnp.where(level_mask[l], a, 0.0))

        vb = v.astype(BF16)
        v_stack = jnp.where(v_head, jnp.concatenate([vb] * nh, axis=0), jnp.zeros((), BF16))
        o = jnp.dot(attn.astype(BF16), v_stack, preferred_element_type=F32)

        state = state_ref[b]
        o = o + lax.dot_general((q * jnp.exp(bc)).astype(BF16), state.astype(BF16), _NT,
                                preferred_element_type=F32)
        b_last = bc[c - 1:c, :]
        k_dec = (k * jnp.exp(b_last - bc)).astype(BF16)
        upd = lax.dot_general(vb, k_dec, _TN, preferred_element_type=F32)
        state_ref[b] = state * jnp.exp(b_last) + jnp.where(s_head, upd, 0.0)

        on = o * lax.rsqrt(_group_mean_sq(o, bd_ref[...]) + EPS) * gn_ref[...]
        o_ref[b, rows, :] = on * _silu(gd)

    for ci in range(ROW_TILE // c):
        for b in range(z_ref.shape[0]):
            chunk_of(b, pl.ds(ci * c, c))


def _tail_kernel(x_ref, ya_ref, yb_ref, zc_ref, halo_ref, zd_ref,
                 cw_ref, cb_ref, lng_ref, lnb_ref, pw_ref, pwb_ref,
                 sums_ref, wg_ref, bg_ref, gn_ref, bd_ref, wo_ref,
                 o_ref, u_buf, yd_buf, state_ref):
    j = pl.program_id(0)

    @pl.when(j == 0)
    def _():
        state_ref[...] = jnp.zeros_like(state_ref)

    def project(y, g):
        return jnp.dot(y.astype(BF16), wo_ref[g * BRANCH:(g + 1) * BRANCH, :],
                       preferred_element_type=F32)

    _gla_tile(zd_ref, sums_ref, wg_ref, bg_ref, gn_ref, bd_ref, yd_buf, state_ref)
    for b in range(x_ref.shape[0]):
        yc = _conv_module(zc_ref[b], halo_ref[b], j > 0, cw_ref, cb_ref, lng_ref, lnb_ref,
                          pw_ref, pwb_ref, u_buf.at[b])
        o_ref[b] = (x_ref[b] + project(ya_ref[b], 0) + project(yb_ref[b], 1) + project(yc, 2)
                    + project(yd_buf[b], 3))


def _tail(x2, ya, yb, zc, zd, conv_consts, gla_consts, w_out_all, layer, batch, seq):
    per = ROW_TILE // CONV_HALO
    tile = lambda cols: pl.BlockSpec((batch, ROW_TILE, cols), lambda j: (0, j, 0))
    whole = lambda a: pl.BlockSpec(a.shape, lambda j: (0,) * a.ndim, pipeline_mode=pl.Buffered(1))
    by_seq = lambda a: a.reshape(batch, seq, a.shape[-1])
    consts = (*conv_consts, *gla_consts, w_out_all)
    w_out_spec = pl.BlockSpec((None,) + w_out_all.shape[1:], lambda j: (layer, 0, 0),
                              pipeline_mode=pl.Buffered(1))
    out = pl.pallas_call(
        _tail_kernel,
        grid=(seq // ROW_TILE,),
        in_specs=[tile(D_MODEL), tile(BRANCH), tile(BRANCH), tile(ZC_COLS),
                  pl.BlockSpec((batch, CONV_HALO, ZC_COLS),
                               lambda j: (0, jnp.maximum(j * per - 1, 0), 0)),
                  tile(ZD_COLS)] + [whole(a) for a in consts[:-1]] + [w_out_spec],
        out_specs=tile(D_MODEL),
        out_shape=jax.ShapeDtypeStruct((batch, seq, D_MODEL), F32),
        scratch_shapes=[pltpu.VMEM((batch, SUBLANES, CONV_HALO + ROW_TILE, BRANCH), F32),
                        pltpu.VMEM((batch, ROW_TILE, BRANCH), F32),
                        pltpu.VMEM((batch, GLA_HEADS * GLA_DV, GLA_HEADS * GLA_DK), F32)],
        compiler_params=_params(1),
        name="tail",
    )(by_seq(x2), by_seq(ya), by_seq(yb), by_seq(zc), by_seq(zc), by_seq(zd), *consts)
    return out.reshape(batch * seq, D_MODEL)


def _pack_w_in(w_in):
    col = lambda j: w_in[..., j * BRANCH:(j + 1) * BRANCH]
    gla0 = 11 * BRANCH
    qkv = 2 * GLA_HEADS * GLA_DK + BRANCH
    pad = jnp.zeros(w_in.shape[:-1] + (LR_PAD - GLA_RANK,), w_in.dtype)
    wn = jnp.concatenate([col(1), col(3), col(5), col(7), w_in[..., 8 * BRANCH:gla0 + qkv],
                          w_in[..., gla0 + qkv + GLA_RANK:], w_in[..., gla0 + qkv:gla0 + qkv + GLA_RANK],
                          pad], axis=-1)
    wt = jnp.swapaxes(jnp.concatenate([col(0), col(2), col(4), col(6)], axis=-1), -1, -2)
    return wn.astype(BF16), wt.astype(BF16)


def _layer(x2, batch, seq, consts, layer, big_weights, norm_g, q_gain_a, k_gain_a, q_gain_b, k_gain_b,
           conv_w, conv_b, conv_ln_g, conv_ln_b, conv_pw_w, conv_pw_b, gla_gate_w, gla_gate_b, gla_norm_g):
    bd, kx, qx_moba, qx_dil, dil_table, gla_sums = consts
    wn_all, wt_all, w_out_all = big_weights
    kgains = jnp.stack([jnp.tile(k_gain_a, N_HEADS), jnp.tile(k_gain_b, N_HEADS)])
    qgains = jnp.stack([jnp.tile(q_gain_a, N_HEADS), jnp.tile(q_gain_b, N_HEADS)])[:, :, None]
    (qat, ka, vat, ga, sel, qbt, kb, vbt, gb, zc, zd) = _inproj(
        x2, norm_g[None, :], wn_all, wt_all, layer, kgains, qgains, bd, kx, seq)
    ya = _moba(qat, ka, vat, sel, ga, qx_moba, batch, seq)
    yb = _dilated(qbt, kb, vbt, dil_table, gb, qx_dil, batch, seq)
    wg = jnp.concatenate([gla_gate_w, jnp.zeros((LR_PAD - GLA_RANK, gla_gate_w.shape[1]), F32)],
                         axis=0).astype(BF16)
    conv_consts = (conv_w, conv_b[None, :], conv_ln_g[None, :], conv_ln_b[None, :],
                   conv_pw_w.astype(BF16), conv_pw_b[None, :])
    gla_consts = (gla_sums, wg, gla_gate_b[None, :], jnp.tile(gla_norm_g, GLA_HEADS)[None, :], bd)
    return _tail(x2, ya, yb, zc, zd, conv_consts, gla_consts, w_out_all, layer, batch, seq)


def kernel(x, norm_g, w_in, q_gain_a, k_gain_a, q_gain_b, k_gain_b, conv_w, conv_b, conv_ln_g, conv_ln_b,
           conv_pw_w, conv_pw_b, gla_gate_w, gla_gate_b, gla_norm_g, w_out):
    batch, seq, d = x.shape
    assert d == D_MODEL and seq % ROW_TILE == 0 and ROW_TILE == ATT_TILE
    group = np.arange(BRANCH) // HEAD_DIM
    bd = jnp.asarray((group[:, None] == group[None, :]) / HEAD_DIM, BF16)
    heads = np.arange(N_HEADS)
    consts = (bd, _key_position_lanes(seq),
              _query_alibi_rows(2.0 ** -(1.0 + 2 * heads)),
              _query_alibi_rows(2.0 ** -(2.0 + 2 * heads)),
              _dilated_multiplicity_table(), _gla_sum_matrices())
    x2 = x.reshape(batch * seq, d)
    big_weights = (*_pack_w_in(w_in), w_out.astype(BF16))
    params = (norm_g, q_gain_a, k_gain_a, q_gain_b, k_gain_b, conv_w, conv_b, conv_ln_g,
              conv_ln_b, conv_pw_w, conv_pw_b, gla_gate_w, gla_gate_b, gla_norm_g)
    for layer in range(norm_g.shape[0]):
        x2 = _layer(x2, batch, seq, consts, layer, big_weights, *(p[layer] for p in params))
    return x2.reshape(batch, seq, d)
```

```python
import functools

import numpy as np
import jax
import jax.numpy as jnp
from jax import lax
from jax.experimental import pallas as pl
from jax.experimental.pallas import tpu as pltpu

F32 = jnp.float32
BF16 = jnp.bfloat16

D_MODEL = 1024
BRANCH = 256
HEAD_DIM = 64
N_HEADS = BRANCH // HEAD_DIM
MOBA_BLOCK = 256
MOBA_TOPK = 3
DIL_PATTERNS = ((128, 1), (512, 4), (2048, 16))
CONV_WIDTH = 31
GLA_HEADS = 4
GLA_DK = 32
GLA_DV = 64
GLA_RANK = 16
GLA_TAU = 16.0
EPS = 1e-6
NEG = -1e30
LOG2E = 1.4426950408889634

LANES = 128
SUBLANES = 8
ROW_TILE = 512
ATT_TILE = 512
BLOCKS_PER_TILE = ATT_TILE // MOBA_BLOCK
DIL_TILES_PER_STEP = 1
DIL_RESIDUES = max(d for _, d in DIL_PATTERNS)
RES_LANE0 = 48
DEAD_LANE = RES_LANE0 + DIL_RESIDUES
DIL_GROUPS_BACK = max(w for w, _ in DIL_PATTERNS) // ATT_TILE
ALIBI_PIECES = 4
SEL_LANE0 = 16
MASK_BIAS = 2.0 ** 100
M_INIT = -1e29
GLA_CHUNK = 128
GLA_LEVELS = 7
CONV_HALO = 32
LR_PAD = 128
VMEM_LIMIT = 56 * 1024 * 1024

WT_ROWS = 4 * BRANCH
ZC_COLS = 3 * BRANCH
ZD_COLS = 2 * GLA_HEADS * GLA_DK + 2 * BRANCH + LR_PAD
WN_COLS = 4 * BRANCH + ZC_COLS + ZD_COLS

_NT = (((1,), (1,)), ((), ()))
_TN = (((0,), (0,)), ((), ()))


def _params(n_grid):
    return pltpu.CompilerParams(dimension_semantics=("arbitrary",) * n_grid,
                                vmem_limit_bytes=VMEM_LIMIT)


def _silu(x):
    return x * jax.nn.sigmoid(x)


def _group_mean_sq(z, bd):
    z2 = z * z
    hi = z2.astype(BF16)
    lo = (z2 - hi.astype(F32)).astype(BF16)
    return (jnp.dot(hi, bd, preferred_element_type=F32)
            + jnp.dot(lo, bd, preferred_element_type=F32))


def _inproj_kernel(tiles_per_seq, x_ref, ng_ref, wn_ref, wt_ref, kgain_ref, qgain_ref, bd_ref, kx_ref,
                   qat_ref, ka_ref, vat_ref, ga_ref, sel_ref,
                   qbt_ref, kb_ref, vbt_ref, gb_ref, zc_ref, zd_ref, km_buf):
    tile = pl.program_id(0) % tiles_per_seq

    @pl.when(tile == 0)
    def _():
        km_buf[...] = jnp.zeros_like(km_buf)

    x = x_ref[...]
    ms = jnp.mean(x * x, axis=-1, keepdims=True)
    h = (x * lax.rsqrt(ms + EPS) * ng_ref[...]).astype(BF16)
    bd = bd_ref[...]
    kx = kx_ref[...]

    def proj(c0, width):
        return jnp.dot(h, wn_ref[:, c0:c0 + width], preferred_element_type=F32)

    def proj_t(r0):
        return lax.dot_general(wt_ref[r0:r0 + BRANCH, :], h, _NT, preferred_element_type=F32)

    def head_norm(z, row):
        return z * lax.rsqrt(_group_mean_sq(z, bd) + EPS) * kgain_ref[row:row + 1, :]

    def head_norm_t(zt, idx):
        parts = []
        for g in range(N_HEADS):
            part = zt[g * HEAD_DIM:(g + 1) * HEAD_DIM]
            parts.append(part * lax.rsqrt(jnp.mean(part * part, axis=0, keepdims=True) + EPS))
        return jnp.concatenate(parts, axis=0) * qgain_ref[idx]

    def store_keys(ref, kn):
        for hp in range(BRANCH // LANES):
            ref[:, 2 * hp * LANES:(2 * hp + 1) * LANES] = kn[:, hp * LANES:(hp + 1) * LANES].astype(BF16)
            ref[:, (2 * hp + 1) * LANES:(2 * hp + 2) * LANES] = kx

    qa = head_norm_t(proj_t(0), 0)
    ka = head_norm(proj(0, BRANCH), 0)
    store_keys(ka_ref, ka)
    vat_ref[0] = proj_t(BRANCH).astype(BF16)
    ga_ref[...] = proj(BRANCH, BRANCH)

    for blk in range(BLOCKS_PER_TILE):
        km_buf[pl.ds(tile * BLOCKS_PER_TILE + blk, 1), :] = jnp.mean(
            ka[blk * MOBA_BLOCK:(blk + 1) * MOBA_BLOCK], axis=0, keepdims=True)
    km = km_buf[...]
    n_blk = km.shape[0]
    blk = lax.broadcasted_iota(jnp.int32, (n_blk, ROW_TILE), 0)
    own = (tile * BLOCKS_PER_TILE
           + lax.broadcasted_iota(jnp.int32, (n_blk, ROW_TILE), 1) // MOBA_BLOCK)
    blk_f = blk.astype(F32)
    past = blk < own
    for head, qh in enumerate(_head_operands(qa)):
        gate = jnp.dot(km[:, (head // 2) * LANES:(head // 2 + 1) * LANES], qh,
                       precision=lax.Precision.HIGHEST, preferred_element_type=F32)
        keep = (past & _top_k_rows(jnp.where(past, gate, -jnp.inf), blk_f)) | (blk == own)
        sel_ref[0, head] = jnp.where(keep, 0.0, -MASK_BIAS).astype(BF16)

    qat_ref[0] = (qa * (HEAD_DIM ** -0.5 * LOG2E)).astype(BF16)
    qbt_ref[0] = (head_norm_t(proj_t(2 * BRANCH), 1) * (HEAD_DIM ** -0.5 * LOG2E)).astype(BF16)
    store_keys(kb_ref, head_norm(proj(2 * BRANCH, BRANCH), 1))
    vbt_ref[0] = proj_t(3 * BRANCH).astype(BF16)
    gb_ref[...] = proj(3 * BRANCH, BRANCH)

    zc_ref[...] = proj(4 * BRANCH, ZC_COLS)
    zd_ref[...] = proj(4 * BRANCH + ZC_COLS, ZD_COLS)


def _inproj(x2, ng, wn_all, wt_all, layer, kgains, qgains, bd, kx, seq):
    n = x2.shape[0]
    nt = n // ROW_TILE
    per_seq = seq // ROW_TILE
    row = lambda i: (i, 0)
    const = lambda i: (0, 0)

    def nat(cols, dtype):
        return (jax.ShapeDtypeStruct((n, cols), dtype), pl.BlockSpec((ROW_TILE, cols), row))

    def tr(dtype):
        return (jax.ShapeDtypeStruct((nt, BRANCH, ROW_TILE), dtype),
                pl.BlockSpec((1, BRANCH, ROW_TILE), lambda i: (i, 0, 0)))

    n_blk = seq // MOBA_BLOCK
    choice = (jax.ShapeDtypeStruct((nt, N_HEADS, n_blk, ROW_TILE), BF16),
              pl.BlockSpec((1, N_HEADS, n_blk, ROW_TILE), lambda i: (i, 0, 0, 0)))
    outs = [tr(BF16), nat(2 * BRANCH, BF16), tr(BF16), nat(BRANCH, F32), choice,
            tr(BF16), nat(2 * BRANCH, BF16), tr(BF16), nat(BRANCH, F32),
            nat(ZC_COLS, F32), nat(ZD_COLS, F32)]
    return pl.pallas_call(
        functools.partial(_inproj_kernel, per_seq),
        grid=(nt,),
        in_specs=[pl.BlockSpec((ROW_TILE, D_MODEL), row),
                  pl.BlockSpec((1, D_MODEL), const),
                  pl.BlockSpec((None, D_MODEL, WN_COLS), lambda i: (layer, 0, 0)),
                  pl.BlockSpec((None, WT_ROWS, D_MODEL), lambda i: (layer, 0, 0)),
                  pl.BlockSpec((2, BRANCH), const),
                  pl.BlockSpec((2, BRANCH, 1), lambda i: (0, 0, 0)),
                  pl.BlockSpec((BRANCH, BRANCH), const),
                  pl.BlockSpec((ROW_TILE, LANES), lambda i: (i % per_seq, 0))],
        out_specs=[o[1] for o in outs],
        out_shape=[o[0] for o in outs],
        scratch_shapes=[pltpu.VMEM((n_blk, BRANCH), F32)],
        compiler_params=_params(1),
        name="inproj",
    )(x2, ng, wn_all, wt_all, kgains, qgains, bd, kx)


def _key_position_lanes(seq):
    pos = np.arange(seq)
    c, n = pos % MOBA_BLOCK, pos // MOBA_BLOCK
    kx = np.zeros((seq, LANES), np.float32)
    p = ALIBI_PIECES
    kx[:, 0:p] = (c // 16)[:, None]
    kx[:, p:2 * p] = (c % 16)[:, None]
    kx[:, 2 * p:3 * p] = n[:, None]
    kx[pos, SEL_LANE0 + n] = 1.0
    kx[:, RES_LANE0:RES_LANE0 + DIL_RESIDUES] = -1.0
    kx[pos, RES_LANE0 + pos % DIL_RESIDUES] = 0.0
    kx[:, DEAD_LANE] = 1.0
    return jnp.asarray(kx, BF16)


def _query_alibi_rows(slopes):
    pieces, rest = [], LOG2E
    for _ in range(ALIBI_PIECES):
        piece = float(np.asarray(rest, dtype=BF16).astype(np.float64))
        pieces.append(piece)
        rest -= piece
    p = ALIBI_PIECES
    qx = np.zeros((len(slopes), LANES, ATT_TILE), np.float32)
    for h, slope in enumerate(slopes):
        for g, weight in enumerate((16.0, 1.0, float(MOBA_BLOCK))):
            qx[h, g * p:(g + 1) * p, :] = np.asarray([weight * slope * piece for piece in pieces])[:, None]
    return jnp.asarray(qx, F32)


class _SweepOps:
    def __init__(self, k_ref, vt_ref, qft_buf, s_bufs, p_bufs, a_bufs, smax_bufs, m_buf, acc_buf):
        self.k_ref, self.vt_ref, self.qft_buf = k_ref, vt_ref, qft_buf
        self.s_bufs, self.p_bufs, self.a_bufs, self.smax_bufs = s_bufs, p_bufs, a_bufs, smax_bufs
        self.m_buf, self.acc_buf = m_buf, acc_buf

    def init(self):
        self.m_buf[...] = jnp.full(self.m_buf.shape, M_INIT, F32)
        self.acc_buf[...] = jnp.zeros(self.acc_buf.shape, F32)

    def issue_scores(self, group, x, bias=None, variant=None):
        rows = pl.ds(pl.multiple_of(group * ATT_TILE, ATT_TILE), ATT_TILE)
        keys = [self.k_ref[rows, 2 * pair * LANES:2 * (pair + 1) * LANES]
                for pair in range(N_HEADS // 2)]
        for h in range(N_HEADS):
            qft = self.qft_buf[h] if variant is None else self.qft_buf[variant, h]
            s = jnp.dot(keys[h // 2], qft, preferred_element_type=F32)
            if bias is None:
                self.smax_bufs[x][h] = jnp.max(s, axis=0, keepdims=True)
            else:
                s = s + bias
            self.s_bufs[x][h] = s

    def softmax(self, x, mask=None, issued_max=True):
        for h in range(N_HEADS):
            s = self.s_bufs[x][h]
            if mask is not None:
                s = mask(s)
            if mask is None and issued_max:
                group_max = self.smax_bufs[x][h]
            else:
                group_max = jnp.max(s, axis=0, keepdims=True)
            m_old = self.m_buf[h]
            m_new = jnp.maximum(m_old, group_max)
            self.m_buf[h] = m_new
            self.a_bufs[x][h] = jnp.exp2(m_old - m_new)
            self.p_bufs[x][h] = jnp.exp2(s - m_new).astype(BF16)

    def fold_values(self, group, x):
        ones = jnp.ones((SUBLANES, ATT_TILE), BF16)
        vt = self.vt_ref[group]
        for h in range(N_HEADS):
            lhs = jnp.concatenate([vt[h * HEAD_DIM:(h + 1) * HEAD_DIM, :], ones], axis=0)
            self.acc_buf[h] = (self.a_bufs[x][h] * self.acc_buf[h]
                               + jnp.dot(lhs, self.p_bufs[x][h], preferred_element_type=F32))

    def finish(self, g_ref, o_ref):
        acc = self.acc_buf
        out_t = jnp.concatenate(
            [acc[h, 0:HEAD_DIM, :] / acc[h, HEAD_DIM:HEAD_DIM + 1, :] for h in range(N_HEADS)], axis=0)
        o_ref[...] = out_t.T * _silu(g_ref[...])


def _flash_sweep(ops, n_steps, last_mask, g_ref, o_ref):
    def regular_step(t, x):
        ops.issue_scores(t + 1, 1 - x)
        ops.fold_values(jnp.maximum(t - 1, 0), 1 - x)
        ops.softmax(x)

    def neutral_fold(x):
        ops.a_bufs[x][...] = jnp.ones(ops.a_bufs[x].shape, F32)
        ops.p_bufs[x][...] = jnp.zeros(ops.p_bufs[x].shape, BF16)

    ops.init()
    n_regular = n_steps - 1
    odd = n_regular % 2

    @pl.when(odd == 1)
    def _():
        neutral_fold(0)
        ops.issue_scores(0, 1)
        regular_step(0, 1)

    @pl.when(odd == 0)
    def _():
        neutral_fold(1)
        ops.issue_scores(0, 0)

    def step_pair(u, _):
        t = odd + 2 * u
        regular_step(t, 0)
        regular_step(t + 1, 1)
        return 0

    lax.fori_loop(0, n_regular // 2, step_pair, 0)
    ops.fold_values(jnp.maximum(n_steps - 2, 0), 1)
    ops.softmax(0, last_mask)
    ops.fold_values(n_steps - 1, 0)
    ops.finish(g_ref, o_ref)


def _head_operands(qt):
    first = lax.broadcasted_iota(jnp.int32, (LANES, qt.shape[1]), 0) < HEAD_DIM
    zero = jnp.zeros((), qt.dtype)
    out = []
    for h in range(N_HEADS):
        pair = qt[(h // 2) * LANES:(h // 2 + 1) * LANES]
        out.append(jnp.where(first, pair, zero) if h % 2 == 0 else jnp.where(first, zero, pair))
    return out


def _attn_scratch():
    stat = pltpu.VMEM((N_HEADS, 1, ATT_TILE), F32)
    return ([pltpu.VMEM((N_HEADS, 2 * LANES, ATT_TILE), BF16)]
            + [pltpu.VMEM((N_HEADS, ATT_TILE, ATT_TILE), F32)] * 2
            + [pltpu.VMEM((N_HEADS, ATT_TILE, ATT_TILE), BF16)] * 2
            + [stat, stat]
            + [stat, stat]
            + [stat]
            + [pltpu.VMEM((N_HEADS, HEAD_DIM + SUBLANES, ATT_TILE), F32)])


def _attn_specs(seq):
    nq = seq // ATT_TILE
    q_tile = pl.BlockSpec((1, BRANCH, ATT_TILE), lambda b, i: (b * nq + i, 0, 0))
    keys = pl.BlockSpec((seq, 2 * BRANCH), lambda b, i: (b, 0))
    values = pl.BlockSpec((nq, BRANCH, ATT_TILE), lambda b, i: (b, 0, 0))
    gate = pl.BlockSpec((ATT_TILE, BRANCH), lambda b, i: (b * nq + i, 0))
    qx = pl.BlockSpec((N_HEADS, LANES, ATT_TILE), lambda b, i: (0, 0, 0),
                      pipeline_mode=pl.Buffered(1))
    return nq, q_tile, keys, values, gate, qx


def _top_k_rows(gate, row_f):
    sel = jnp.zeros(gate.shape, F32)
    for _ in range(MOBA_TOPK):
        top = jnp.max(gate, axis=0, keepdims=True)
        first = jnp.min(jnp.where(gate == top, row_f, 1e9), axis=0, keepdims=True)
        pick = row_f == first
        sel = jnp.where(pick, 1.0, sel)
        gate = jnp.where(pick, -jnp.inf, gate)
    return sel > 0.5


def _moba_kernel(qt_ref, k_ref, vt_ref, sel_ref, g_ref, qx_ref, o_ref, qft_buf, *bufs):
    i = pl.program_id(1)
    n_blk = sel_ref.shape[2]
    unused = jnp.zeros((LANES - SEL_LANE0 - n_blk, ATT_TILE), BF16)
    for h, qh in enumerate(_head_operands(qt_ref[0])):
        qft_buf[h] = jnp.concatenate(
            [qh, qx_ref[h, 0:SEL_LANE0, :].astype(BF16), sel_ref[0, h], unused], axis=0)

    causal = (lax.broadcasted_iota(jnp.int32, (ATT_TILE, ATT_TILE), 0)
              <= lax.broadcasted_iota(jnp.int32, (ATT_TILE, ATT_TILE), 1))
    ops = _SweepOps(k_ref, vt_ref, qft_buf, bufs[0:2], bufs[2:4], bufs[4:6], bufs[6:8], *bufs[8:])
    _flash_sweep(ops, i + 1, lambda s: jnp.where(causal, s, NEG), g_ref, o_ref)


def _moba(qt, k, vt, sel, g, qx, batch, seq):
    nq, q_tile, keys, values, gate, qx_spec = _attn_specs(seq)
    n_blk = seq // MOBA_BLOCK
    assert SEL_LANE0 + n_blk <= LANES and 3 * ALIBI_PIECES <= SEL_LANE0
    return pl.pallas_call(
        _moba_kernel,
        grid=(batch, nq),
        in_specs=[q_tile, keys, values,
                  pl.BlockSpec((1, N_HEADS, n_blk, ATT_TILE), lambda b, i: (b * nq + i, 0, 0, 0)),
                  gate, qx_spec],
        out_specs=gate,
        out_shape=jax.ShapeDtypeStruct(g.shape, F32),
        scratch_shapes=_attn_scratch(),
        compiler_params=_params(2),
        name="moba",
    )(qt, k, vt, sel, g, qx)


def _dilated_multiplicity_table():
    idx = np.arange(ATT_TILE)
    delta = (np.arange(DIL_GROUPS_BACK + 1)[:, None, None] * ATT_TILE
             + idx[None, None, :] - idx[None, :, None])
    mult = np.zeros(delta.shape, np.float64)
    for window, dil in DIL_PATTERNS:
        mult += (delta >= 0) & (delta <= window) & (delta % dil == 0)
    table = np.where(mult > 0, np.log2(np.maximum(mult, 1.0)), NEG)
    same_residue = (idx[None, :] - idx[:, None]) % DIL_RESIDUES == 0
    residue_only = tuple(o for o in range(DIL_GROUPS_BACK + 1)
                         if np.array_equal(mult[o], same_residue.astype(np.float64)))
    return jnp.asarray(np.concatenate([table, np.full_like(table[:1], NEG)]), F32), residue_only


def _dilated_query_variants():
    c = np.arange(ATT_TILE)
    rows = np.zeros((2, LANES, ATT_TILE), np.float32)
    rows[0, RES_LANE0 + c % DIL_RESIDUES, c] = MASK_BIAS
    rows[1, DEAD_LANE, :] = -MASK_BIAS
    return jnp.asarray(rows, F32)


PLAIN, RESIDUE, DEAD = range(3)


def _dilated_kernel(residue_only, qt_ref, qt_next_ref, k_ref, vt_ref, t_ref, g_ref, qx_ref, qv_ref,
                    o_ref, qft_buf, *bufs):
    i = pl.program_id(1)
    n_steps = DIL_GROUPS_BACK + 1
    ops = _SweepOps(k_ref, vt_ref, qft_buf, bufs[0:2], bufs[2:4], bufs[4:6], bufs[6:8], *bufs[8:])

    def group_of(tile, t):
        return jnp.where(t <= tile, tile - t, 0)

    def issue(tile, t, x):
        if t in residue_only:
            ops.issue_scores(group_of(tile, t), x, variant=jnp.where(t <= tile, RESIDUE, DEAD))
        else:
            ops.issue_scores(group_of(tile, t), x, t_ref[jnp.where(t <= tile, t, n_steps)], PLAIN)

    def prepare(tile, q):
        for h, qh in enumerate(_head_operands(q)):
            extra = qx_ref[h]
            for variant, rows in ((PLAIN, extra), (RESIDUE, extra + qv_ref[0]), (DEAD, extra + qv_ref[1])):
                qft_buf[variant, h] = jnp.concatenate([qh, rows.astype(BF16)], axis=0)
        issue(tile, 0, 0)

    @pl.when(i == 0)
    def _():
        prepare(0, qt_ref[0])

    n_tiles = pl.num_programs(1) * DIL_TILES_PER_STEP
    for sub in range(DIL_TILES_PER_STEP):
        tile = i * DIL_TILES_PER_STEP + sub
        rows = pl.ds(sub * ATT_TILE, ATT_TILE)
        ops.init()
        for t in range(n_steps):
            x = t % 2
            if t + 1 < n_steps:
                issue(tile, t + 1, 1 - x)
            if t >= 1:
                ops.fold_values(group_of(tile, t - 1), 1 - x)
            ops.softmax(x, issued_max=t in residue_only)
        q_after = qt_ref[sub + 1] if sub + 1 < DIL_TILES_PER_STEP else qt_next_ref[0]
        prepare(jnp.minimum(tile + 1, n_tiles - 1), q_after)
        ops.fold_values(group_of(tile, n_steps - 1), (n_steps - 1) % 2)
        ops.finish(g_ref.at[rows], o_ref.at[rows])


def _dilated(qt, k, vt, table_and_steps, g, qx, qv, batch, seq):
    table, residue_only = table_and_steps
    nq, _, keys, values, _, qx_spec = _attn_specs(seq)
    assert DIL_GROUPS_BACK % 2 == 0
    assert 0 not in residue_only and SEL_LANE0 + seq // MOBA_BLOCK <= RES_LANE0 and DEAD_LANE < LANES
    once = lambda a: pl.BlockSpec(a.shape, lambda b, i: (0,) * a.ndim, pipeline_mode=pl.Buffered(1))
    scratch = _attn_scratch()
    scratch[0] = pltpu.VMEM((3, N_HEADS, 2 * LANES, ATT_TILE), BF16)
    per = DIL_TILES_PER_STEP
    assert nq % per == 0
    ns = nq // per
    q_tiles = pl.BlockSpec((per, BRANCH, ATT_TILE), lambda b, i: (b * ns + i, 0, 0))
    q_next = pl.BlockSpec((1, BRANCH, ATT_TILE),
                          lambda b, i: (b * nq + jnp.minimum((i + 1) * per, nq - 1), 0, 0))
    gate = pl.BlockSpec((per * ATT_TILE, BRANCH), lambda b, i: (b * ns + i, 0))
    return pl.pallas_call(
        functools.partial(_dilated_kernel, residue_only),
        grid=(batch, ns),
        in_specs=[q_tiles, q_next, keys, values, once(table), gate, qx_spec, once(qv)],
        out_specs=gate,
        out_shape=jax.ShapeDtypeStruct(g.shape, F32),
        scratch_shapes=scratch,
        compiler_params=_params(2),
        name="dilated",
    )(qt, qt, k, vt, table, g, qx, qv)


def _conv_module(z, halo, has_history, w_ref, b_ref, lng_ref, lnb_ref, pw_ref, pwb_ref, u_buf):
    def glu(z):
        return z[:, 0:BRANCH] * jax.nn.sigmoid(z[:, BRANCH:2 * BRANCH])

    u_buf[0, 0:CONV_HALO, :] = jnp.where(has_history, glu(halo), 0.0)
    u_buf[0, CONV_HALO:, :] = glu(z)
    shifted = CONV_HALO + ROW_TILE - SUBLANES
    for phase in range(1, SUBLANES):
        u_buf[phase, 0:shifted, :] = u_buf[0, phase:phase + shifted, :]
    acc = jnp.zeros((ROW_TILE, BRANCH), F32) + b_ref[...]
    first = CONV_HALO - (CONV_WIDTH - 1)
    for tap in range(CONV_WIDTH):
        phase, start = (first + tap) % SUBLANES, (first + tap) // SUBLANES * SUBLANES
        acc = acc + w_ref[tap:tap + 1, :] * u_buf[phase, start:start + ROW_TILE, :]
    mu = jnp.mean(acc, axis=-1, keepdims=True)
    cen = acc - mu
    var = jnp.mean(cen * cen, axis=-1, keepdims=True)
    un = cen * lax.rsqrt(var + EPS) * lng_ref[...] + lnb_ref[...]
    y = jnp.dot(_silu(un).astype(BF16), pw_ref[...], preferred_element_type=F32) + pwb_ref[...]
    return y * _silu(z[:, 2 * BRANCH:3 * BRANCH])


def _gla_sum_matrices():
    c = GLA_CHUNK
    i = np.arange(c)[:, None]
    t = np.arange(c)[None, :]
    mats = [t <= i]
    for l in range(GLA_LEVELS):
        h = (c // 2) >> l
        mid = (i // (2 * h)) * (2 * h) + h
        later = (i & h) != 0
        mats.append((later & (t >= mid) & (t <= i)) | (~later & (t > i) & (t < mid)))
    return jnp.asarray(np.concatenate(mats, axis=0), BF16)


def _gla_tile(z_ref, sums_ref, wg_ref, bg_ref, gn_ref, bd_ref, o_ref, state_ref):
    c = GLA_CHUNK
    nh = GLA_HEADS
    kw = nh * GLA_DK
    vw = nh * GLA_DV

    row = lax.broadcasted_iota(jnp.int32, (c, kw), 0)
    qi = lax.broadcasted_iota(jnp.int32, (c, nh * c), 0)
    kj = lax.broadcasted_iota(jnp.int32, (c, nh * c), 1) % c
    level_mask = [(qi >> (GLA_LEVELS - l)) == (kj >> (GLA_LEVELS - l)) for l in range(GLA_LEVELS)]
    diag_mask = qi == kj
    k_head = (lax.broadcasted_iota(jnp.int32, (nh * c, kw), 0) // c
              == lax.broadcasted_iota(jnp.int32, (nh * c, kw), 1) // GLA_DK)
    v_head = (lax.broadcasted_iota(jnp.int32, (nh * c, vw), 0) // c
              == lax.broadcasted_iota(jnp.int32, (nh * c, vw), 1) // GLA_DV)
    s_head = (lax.broadcasted_iota(jnp.int32, (vw, kw), 0) // GLA_DV
              == lax.broadcasted_iota(jnp.int32, (vw, kw), 1) // GLA_DK)

    k_head_bf = jnp.where(k_head, 1.0, 0.0).astype(BF16)

    def per_head_keys(kt):
        return jnp.concatenate([kt.astype(BF16)] * nh, axis=0) * k_head_bf

    def chunk_of(b, rows):
        q = z_ref[b, rows, 0:kw] * GLA_DK ** -0.5
        k = z_ref[b, rows, kw:2 * kw]
        v = z_ref[b, rows, 2 * kw:2 * kw + vw]
        gd = z_ref[b, rows, 2 * kw + vw:2 * kw + 2 * vw]
        lr = z_ref[b, rows, 2 * kw + 2 * vw:2 * kw + 2 * vw + LR_PAD]

        g = jnp.dot(lr.astype(BF16), wg_ref[...], preferred_element_type=F32) + bg_ref[...]
        la = (jnp.minimum(g, 0.0) - jnp.log(1.0 + jnp.exp(-jnp.abs(g)))) / GLA_TAU
        a1 = la.astype(BF16)
        a2 = (la - a1.astype(F32)).astype(BF16)
        parts = jnp.dot(sums_ref[...], jnp.concatenate([a1, a2], axis=1),
                        preferred_element_type=F32)
        sums = parts[:, 0:kw] + parts[:, kw:2 * kw]
        bc = sums[0:c]

        attn = jnp.where(diag_mask,
                         lax.dot_general(q.astype(BF16), per_head_keys(k), _NT,
                                         preferred_element_type=F32), 0.0)
        for l in range(GLA_LEVELS):
            later = (row & ((c // 2) >> l)) != 0
            scaled = jnp.where(later, q, k) * jnp.exp(sums[(1 + l) * c:(2 + l) * c])
            qt = jnp.where(later, scaled, 0.0).astype(BF16)
            a = lax.dot_general(qt, per_head_keys(jnp.where(later, 0.0, scaled)), _NT,
                                preferred_element_type=F32)
            attn = attn + (a if l == 0 else jnp.where(level_mask[l], a, 0.0))

        vb = v.astype(BF16)
        v_stack = jnp.where(v_head, jnp.concatenate([vb] * nh, axis=0), jnp.zeros((), BF16))
        o = jnp.dot(attn.astype(BF16), v_stack, preferred_element_type=F32)

        state = state_ref[b]
        o = o + lax.dot_general((q * jnp.exp(bc)).astype(BF16), state.astype(BF16), _NT,
                                preferred_element_type=F32)
        b_last = bc[c - 1:c, :]
        k_dec = (k * jnp.exp(b_last - bc)).astype(BF16)
        upd = lax.dot_general(vb, k_dec, _TN, preferred_element_type=F32)
        state_ref[b] = state * jnp.exp(b_last) + jnp.where(s_head, upd, 0.0)

        on = o * lax.rsqrt(_group_mean_sq(o, bd_ref[...]) + EPS) * gn_ref[...]
        o_ref[b, rows, :] = on * _silu(gd)

    for ci in range(ROW_TILE // c):
        for b in range(z_ref.shape[0]):
            chunk_of(b, pl.ds(ci * c, c))


def _tail_kernel(x_ref, ya_ref, yb_ref, zc_ref, halo_ref, zd_ref,
                 cw_ref, cb_ref, lng_ref, lnb_ref, pw_ref, pwb_ref,
                 sums_ref, wg_ref, bg_ref, gn_ref, bd_ref, wo_ref,
                 o_ref, u_buf, yd_buf, state_ref):
    j = pl.program_id(0)

    @pl.when(j == 0)
    def _():
        state_ref[...] = jnp.zeros_like(state_ref)

    def project(y, g):
        return jnp.dot(y.astype(BF16), wo_ref[g * BRANCH:(g + 1) * BRANCH, :],
                       preferred_element_type=F32)

    _gla_tile(zd_ref, sums_ref, wg_ref, bg_ref, gn_ref, bd_ref, yd_buf, state_ref)
    for b in range(x_ref.shape[0]):
        yc = _conv_module(zc_ref[b], halo_ref[b], j > 0, cw_ref, cb_ref, lng_ref, lnb_ref,
                          pw_ref, pwb_ref, u_buf.at[b])
        o_ref[b] = (x_ref[b] + project(ya_ref[b], 0) + project(yb_ref[b], 1) + project(yc, 2)
                    + project(yd_buf[b], 3))


def _tail(x2, ya, yb, zc, zd, conv_consts, gla_consts, w_out_all, layer, batch, seq):
    per = ROW_TILE // CONV_HALO
    tile = lambda cols: pl.BlockSpec((batch, ROW_TILE, cols), lambda j: (0, j, 0))
    whole = lambda a: pl.BlockSpec(a.shape, lambda j: (0,) * a.ndim, pipeline_mode=pl.Buffered(1))
    by_seq = lambda a: a.reshape(batch, seq, a.shape[-1])
    consts = (*conv_consts, *gla_consts, w_out_all)
    w_out_spec = pl.BlockSpec((None,) + w_out_all.shape[1:], lambda j: (layer, 0, 0),
                              pipeline_mode=pl.Buffered(1))
    out = pl.pallas_call(
        _tail_kernel,
        grid=(seq // ROW_TILE,),
        in_specs=[tile(D_MODEL), tile(BRANCH), tile(BRANCH), tile(ZC_COLS),
                  pl.BlockSpec((batch, CONV_HALO, ZC_COLS),
                               lambda j: (0, jnp.maximum(j * per - 1, 0), 0)),
                  tile(ZD_COLS)] + [whole(a) for a in consts[:-1]] + [w_out_spec],
        out_specs=tile(D_MODEL),
        out_shape=jax.ShapeDtypeStruct((batch, seq, D_MODEL), F32),
        scratch_shapes=[pltpu.VMEM((batch, SUBLANES, CONV_HALO + ROW_TILE, BRANCH), F32),
                        pltpu.VMEM((batch, ROW_TILE, BRANCH), F32),
                        pltpu.VMEM((batch, GLA_HEADS * GLA_DV, GLA_HEADS * GLA_DK), F32)],
        compiler_params=_params(1),
        name="tail",
    )(by_seq(x2), by_seq(ya), by_seq(yb), by_seq(zc), by_seq(zc), by_seq(zd), *consts)
    return out.reshape(batch * seq, D_MODEL)


def _pack_w_in(w_in):
    col = lambda j: w_in[..., j * BRANCH:(j + 1) * BRANCH]
    gla0 = 11 * BRANCH
    qkv = 2 * GLA_HEADS * GLA_DK + BRANCH
    pad = jnp.zeros(w_in.shape[:-1] + (LR_PAD - GLA_RANK,), w_in.dtype)
    wn = jnp.concatenate([col(1), col(3), col(5), col(7), w_in[..., 8 * BRANCH:gla0 + qkv],
                          w_in[..., gla0 + qkv + GLA_RANK:], w_in[..., gla0 + qkv:gla0 + qkv + GLA_RANK],
                          pad], axis=-1)
    wt = jnp.swapaxes(jnp.concatenate([col(0), col(2), col(4), col(6)], axis=-1), -1, -2)
    return wn.astype(BF16), wt.astype(BF16)


def _layer(x2, batch, seq, consts, layer, big_weights, norm_g, q_gain_a, k_gain_a, q_gain_b, k_gain_b,
           conv_w, conv_b, conv_ln_g, conv_ln_b, conv_pw_w, conv_pw_b, gla_gate_w, gla_gate_b, gla_norm_g):
    bd, kx, qx_moba, qx_dil, dil_table, dil_variants, gla_sums = consts
    wn_all, wt_all, w_out_all = big_weights
    kgains = jnp.stack([jnp.tile(k_gain_a, N_HEADS), jnp.tile(k_gain_b, N_HEADS)])
    qgains = jnp.stack([jnp.tile(q_gain_a, N_HEADS), jnp.tile(q_gain_b, N_HEADS)])[:, :, None]
    (qat, ka, vat, ga, sel, qbt, kb, vbt, gb, zc, zd) = _inproj(
        x2, norm_g[None, :], wn_all, wt_all, layer, kgains, qgains, bd, kx, seq)
    ya = _moba(qat, ka, vat, sel, ga, qx_moba, batch, seq)
    yb = _dilated(qbt, kb, vbt, dil_table, gb, qx_dil, dil_variants, batch, seq)
    wg = jnp.concatenate([gla_gate_w, jnp.zeros((LR_PAD - GLA_RANK, gla_gate_w.shape[1]), F32)],
                         axis=0).astype(BF16)
    conv_consts = (conv_w, conv_b[None, :], conv_ln_g[None, :], conv_ln_b[None, :],
                   conv_pw_w.astype(BF16), conv_pw_b[None, :])
    gla_consts = (gla_sums, wg, gla_gate_b[None, :], jnp.tile(gla_norm_g, GLA_HEADS)[None, :], bd)
    return _tail(x2, ya, yb, zc, zd, conv_consts, gla_consts, w_out_all, layer, batch, seq)


def kernel(x, norm_g, w_in, q_gain_a, k_gain_a, q_gain_b, k_gain_b, conv_w, conv_b, conv_ln_g, conv_ln_b,
           conv_pw_w, conv_pw_b, gla_gate_w, gla_gate_b, gla_norm_g, w_out):
    batch, seq, d = x.shape
    assert d == D_MODEL and seq % ROW_TILE == 0 and ROW_TILE == ATT_TILE
    group = np.arange(BRANCH) // HEAD_DIM
    bd = jnp.asarray((group[:, None] == group[None, :]) / HEAD_DIM, BF16)
    heads = np.arange(N_HEADS)
    consts = (bd, _key_position_lanes(seq),
              _query_alibi_rows(2.0 ** -(1.0 + 2 * heads)),
              _query_alibi_rows(2.0 ** -(2.0 + 2 * heads)),
              _dilated_multiplicity_table(), _dilated_query_variants(), _gla_sum_matrices())
    x2 = x.reshape(batch * seq, d)
    big_weights = (*_pack_w_in(w_in), w_out.astype(BF16))
    params = (norm_g, q_gain_a, k_gain_a, q_gain_b, k_gain_b, conv_w, conv_b, conv_ln_g,
              conv_ln_b, conv_pw_w, conv_pw_b, gla_gate_w, gla_gate_b, gla_norm_g)
    for layer in range(norm_g.shape[0]):
        x2 = _layer(x2, batch, seq, consts, layer, big_weights, *(p[layer] for p in params))
    return x2.reshape(batch, seq, d)
```

```python
import functools

import numpy as np
import jax
import jax.numpy as jnp
from jax import lax
from jax.experimental import pallas as pl
from jax.experimental.pallas import tpu as pltpu

F32 = jnp.float32
BF16 = jnp.bfloat16

D_MODEL = 1024
BRANCH = 256
HEAD_DIM = 64
N_HEADS = BRANCH // HEAD_DIM
MOBA_BLOCK = 256
MOBA_TOPK = 3
DIL_PATTERNS = ((128, 1), (512, 4), (2048, 16))
CONV_WIDTH = 31
GLA_HEADS = 4
GLA_DK = 32
GLA_DV = 64
GLA_RANK = 16
GLA_TAU = 16.0
EPS = 1e-6
NEG = -1e30
LOG2E = 1.4426950408889634

LANES = 128
SUBLANES = 8
ROW_TILE = 512
ATT_TILE = 512
BLOCKS_PER_TILE = ATT_TILE // MOBA_BLOCK
DIL_GROUPS_BACK = max(w for w, _ in DIL_PATTERNS) // ATT_TILE
ALIBI_PIECES = 4
SEL_LANE0 = 16
MASK_BIAS = 2.0 ** 100
M_INIT = -1e29
GLA_CHUNK = 128
GLA_LEVELS = 7
CONV_HALO = 32
LR_PAD = 128
VMEM_LIMIT = 56 * 1024 * 1024

WT_ROWS = 4 * BRANCH
ZC_COLS = 3 * BRANCH
ZD_COLS = 2 * GLA_HEADS * GLA_DK + 2 * BRANCH + LR_PAD
WN_COLS = 4 * BRANCH + ZC_COLS + ZD_COLS

_NT = (((1,), (1,)), ((), ()))
_TN = (((0,), (0,)), ((), ()))


def _params(n_grid):
    return pltpu.CompilerParams(dimension_semantics=("arbitrary",) * n_grid,
                                vmem_limit_bytes=VMEM_LIMIT)


def _silu(x):
    return x * jax.nn.sigmoid(x)


def _group_mean_sq(z, bd):
    z2 = z * z
    hi = z2.astype(BF16)
    lo = (z2 - hi.astype(F32)).astype(BF16)
    return (jnp.dot(hi, bd, preferred_element_type=F32)
            + jnp.dot(lo, bd, preferred_element_type=F32))


def _inproj_kernel(tiles_per_seq, x_ref, ng_ref, wn_ref, wt_ref, kgain_ref, qgain_ref, bd_ref, kx_ref,
                   qat_ref, ka_ref, vat_ref, ga_ref, sel_ref,
                   qbt_ref, kb_ref, vbt_ref, gb_ref, zc_ref, zd_ref, km_buf):
    tile = pl.program_id(0) % tiles_per_seq

    @pl.when(tile == 0)
    def _():
        km_buf[...] = jnp.zeros_like(km_buf)

    x = x_ref[...]
    ms = jnp.mean(x * x, axis=-1, keepdims=True)
    h = (x * lax.rsqrt(ms + EPS) * ng_ref[...]).astype(BF16)
    bd = bd_ref[...]
    kx = kx_ref[...]

    def proj(c0, width):
        return jnp.dot(h, wn_ref[:, c0:c0 + width], preferred_element_type=F32)

    def proj_t(r0):
        return lax.dot_general(wt_ref[r0:r0 + BRANCH, :], h, _NT, preferred_element_type=F32)

    def head_norm(z, row):
        return z * lax.rsqrt(_group_mean_sq(z, bd) + EPS) * kgain_ref[row:row + 1, :]

    def head_norm_t(zt, idx):
        parts = []
        for g in range(N_HEADS):
            part = zt[g * HEAD_DIM:(g + 1) * HEAD_DIM]
            parts.append(part * lax.rsqrt(jnp.mean(part * part, axis=0, keepdims=True) + EPS))
        return jnp.concatenate(parts, axis=0) * qgain_ref[idx]

    def store_keys(ref, kn):
        for hp in range(BRANCH // LANES):
            ref[:, 2 * hp * LANES:(2 * hp + 1) * LANES] = kn[:, hp * LANES:(hp + 1) * LANES].astype(BF16)
            ref[:, (2 * hp + 1) * LANES:(2 * hp + 2) * LANES] = kx

    qa = head_norm_t(proj_t(0), 0)
    ka = head_norm(proj(0, BRANCH), 0)
    store_keys(ka_ref, ka)
    vat_ref[0] = proj_t(BRANCH).astype(BF16)
    ga_ref[...] = proj(BRANCH, BRANCH)

    for blk in range(BLOCKS_PER_TILE):
        km_buf[pl.ds(tile * BLOCKS_PER_TILE + blk, 1), :] = jnp.mean(
            ka[blk * MOBA_BLOCK:(blk + 1) * MOBA_BLOCK], axis=0, keepdims=True)
    km = km_buf[...]
    n_blk = km.shape[0]
    blk = lax.broadcasted_iota(jnp.int32, (n_blk, ROW_TILE), 0)
    own = (tile * BLOCKS_PER_TILE
           + lax.broadcasted_iota(jnp.int32, (n_blk, ROW_TILE), 1) // MOBA_BLOCK)
    blk_f = blk.astype(F32)
    past = blk < own
    for head, qh in enumerate(_head_operands(qa)):
        gate = jnp.dot(km[:, (head // 2) * LANES:(head // 2 + 1) * LANES], qh,
                       precision=lax.Precision.HIGHEST, preferred_element_type=F32)
        keep = (past & _top_k_rows(jnp.where(past, gate, -jnp.inf), blk_f)) | (blk == own)
        sel_ref[0, head] = jnp.where(keep, 0.0, -MASK_BIAS).astype(BF16)

    qat_ref[0] = (qa * (HEAD_DIM ** -0.5 * LOG2E)).astype(BF16)
    qbt_ref[0] = (head_norm_t(proj_t(2 * BRANCH), 1) * (HEAD_DIM ** -0.5 * LOG2E)).astype(BF16)
    store_keys(kb_ref, head_norm(proj(2 * BRANCH, BRANCH), 1))
    vbt_ref[0] = proj_t(3 * BRANCH).astype(BF16)
    gb_ref[...] = proj(3 * BRANCH, BRANCH)

    zc_ref[...] = proj(4 * BRANCH, ZC_COLS)
    zd_ref[...] = proj(4 * BRANCH + ZC_COLS, ZD_COLS)


def _inproj(x2, ng, wn_all, wt_all, layer, kgains, qgains, bd, kx, seq):
    n = x2.shape[0]
    nt = n // ROW_TILE
    per_seq = seq // ROW_TILE
    row = lambda i: (i, 0)
    const = lambda i: (0, 0)

    def nat(cols, dtype):
        return (jax.ShapeDtypeStruct((n, cols), dtype), pl.BlockSpec((ROW_TILE, cols), row))

    def tr(dtype):
        return (jax.ShapeDtypeStruct((nt, BRANCH, ROW_TILE), dtype),
                pl.BlockSpec((1, BRANCH, ROW_TILE), lambda i: (i, 0, 0)))

    n_blk = seq // MOBA_BLOCK
    choice = (jax.ShapeDtypeStruct((nt, N_HEADS, n_blk, ROW_TILE), BF16),
              pl.BlockSpec((1, N_HEADS, n_blk, ROW_TILE), lambda i: (i, 0, 0, 0)))
    outs = [tr(BF16), nat(2 * BRANCH, BF16), tr(BF16), nat(BRANCH, F32), choice,
            tr(BF16), nat(2 * BRANCH, BF16), tr(BF16), nat(BRANCH, F32),
            nat(ZC_COLS, F32), nat(ZD_COLS, F32)]
    return pl.pallas_call(
        functools.partial(_inproj_kernel, per_seq),
        grid=(nt,),
        in_specs=[pl.BlockSpec((ROW_TILE, D_MODEL), row),
                  pl.BlockSpec((1, D_MODEL), const),
                  pl.BlockSpec((None, D_MODEL, WN_COLS), lambda i: (layer, 0, 0)),
                  pl.BlockSpec((None, WT_ROWS, D_MODEL), lambda i: (layer, 0, 0)),
                  pl.BlockSpec((2, BRANCH), const),
                  pl.BlockSpec((2, BRANCH, 1), lambda i: (0, 0, 0)),
                  pl.BlockSpec((BRANCH, BRANCH), const),
                  pl.BlockSpec((ROW_TILE, LANES), lambda i: (i % per_seq, 0))],
        out_specs=[o[1] for o in outs],
        out_shape=[o[0] for o in outs],
        scratch_shapes=[pltpu.VMEM((n_blk, BRANCH), F32)],
        compiler_params=_params(1),
        name="inproj",
    )(x2, ng, wn_all, wt_all, kgains, qgains, bd, kx)


def _key_position_lanes(seq):
    pos = np.arange(seq)
    c, n = pos % MOBA_BLOCK, pos // MOBA_BLOCK
    kx = np.zeros((seq, LANES), np.float32)
    p = ALIBI_PIECES
    kx[:, 0:p] = (c // 16)[:, None]
    kx[:, p:2 * p] = (c % 16)[:, None]
    kx[:, 2 * p:3 * p] = n[:, None]
    kx[pos, SEL_LANE0 + n] = 1.0
    return jnp.asarray(kx, BF16)


def _query_alibi_rows(slopes):
    pieces, rest = [], LOG2E
    for _ in range(ALIBI_PIECES):
        piece = float(np.asarray(rest, dtype=BF16).astype(np.float64))
        pieces.append(piece)
        rest -= piece
    p = ALIBI_PIECES
    qx = np.zeros((len(slopes), LANES, ATT_TILE), np.float32)
    for h, slope in enumerate(slopes):
        for g, weight in enumerate((16.0, 1.0, float(MOBA_BLOCK))):
            qx[h, g * p:(g + 1) * p, :] = np.asarray([weight * slope * piece for piece in pieces])[:, None]
    return jnp.asarray(qx, F32)


class _SweepOps:
    def __init__(self, k_ref, vt_ref, qft_buf, s_bufs, p_bufs, a_bufs, smax_bufs, m_buf, acc_buf):
        self.k_ref, self.vt_ref, self.qft_buf = k_ref, vt_ref, qft_buf
        self.s_bufs, self.p_bufs, self.a_bufs, self.smax_bufs = s_bufs, p_bufs, a_bufs, smax_bufs
        self.m_buf, self.acc_buf = m_buf, acc_buf

    def init(self):
        self.m_buf[...] = jnp.full(self.m_buf.shape, M_INIT, F32)
        self.acc_buf[...] = jnp.zeros(self.acc_buf.shape, F32)

    def issue_scores(self, group, x, bias=None):
        rows = pl.ds(pl.multiple_of(group * ATT_TILE, ATT_TILE), ATT_TILE)
        keys = [self.k_ref[rows, 2 * pair * LANES:2 * (pair + 1) * LANES]
                for pair in range(N_HEADS // 2)]
        for h in range(N_HEADS):
            s = jnp.dot(keys[h // 2], self.qft_buf[h], preferred_element_type=F32)
            if bias is None:
                self.smax_bufs[x][h] = jnp.max(s, axis=0, keepdims=True)
            else:
                s = s + bias
            self.s_bufs[x][h] = s

    def softmax(self, x, mask=None, issued_max=True):
        for h in range(N_HEADS):
            s = self.s_bufs[x][h]
            if mask is not None:
                s = mask(s)
            if mask is None and issued_max:
                group_max = self.smax_bufs[x][h]
            else:
                group_max = jnp.max(s, axis=0, keepdims=True)
            m_old = self.m_buf[h]
            m_new = jnp.maximum(m_old, group_max)
            self.m_buf[h] = m_new
            self.a_bufs[x][h] = jnp.exp2(m_old - m_new)
            self.p_bufs[x][h] = jnp.exp2(s - m_new).astype(BF16)

    def fold_values(self, group, x):
        ones = jnp.ones((SUBLANES, ATT_TILE), BF16)
        vt = self.vt_ref[group]
        for h in range(N_HEADS):
            lhs = jnp.concatenate([vt[h * HEAD_DIM:(h + 1) * HEAD_DIM, :], ones], axis=0)
            self.acc_buf[h] = (self.a_bufs[x][h] * self.acc_buf[h]
                               + jnp.dot(lhs, self.p_bufs[x][h], preferred_element_type=F32))

    def finish(self, g_ref, o_ref):
        acc = self.acc_buf
        out_t = jnp.concatenate(
            [acc[h, 0:HEAD_DIM, :] / acc[h, HEAD_DIM:HEAD_DIM + 1, :] for h in range(N_HEADS)], axis=0)
        o_ref[...] = out_t.T * _silu(g_ref[...])


def _flash_sweep(ops, n_steps, last_mask, g_ref, o_ref):
    def regular_step(t, x):
        ops.issue_scores(t + 1, 1 - x)
        ops.fold_values(jnp.maximum(t - 1, 0), 1 - x)
        ops.softmax(x)

    def neutral_fold(x):
        ops.a_bufs[x][...] = jnp.ones(ops.a_bufs[x].shape, F32)
        ops.p_bufs[x][...] = jnp.zeros(ops.p_bufs[x].shape, BF16)

    def final_step(x):
        ops.fold_values(jnp.maximum(n_steps - 2, 0), 1 - x)
        ops.softmax(x, last_mask)
        ops.fold_values(n_steps - 1, x)
        ops.finish(g_ref, o_ref)

    ops.init()
    neutral_fold(1)
    ops.issue_scores(0, 0)
    n_regular = n_steps - 1

    def step_pair(u, _):
        regular_step(2 * u, 0)
        regular_step(2 * u + 1, 1)
        return 0

    lax.fori_loop(0, n_regular // 2, step_pair, 0)
    odd = n_regular % 2

    @pl.when(odd == 1)
    def _():
        regular_step(n_regular - 1, 0)
        final_step(1)

    @pl.when(odd == 0)
    def _():
        final_step(0)


def _head_operands(qt):
    first = lax.broadcasted_iota(jnp.int32, (LANES, qt.shape[1]), 0) < HEAD_DIM
    zero = jnp.zeros((), qt.dtype)
    out = []
    for h in range(N_HEADS):
        pair = qt[(h // 2) * LANES:(h // 2 + 1) * LANES]
        out.append(jnp.where(first, pair, zero) if h % 2 == 0 else jnp.where(first, zero, pair))
    return out


def _attn_scratch():
    stat = pltpu.VMEM((N_HEADS, 1, ATT_TILE), F32)
    return ([pltpu.VMEM((N_HEADS, 2 * LANES, ATT_TILE), BF16)]
            + [pltpu.VMEM((N_HEADS, ATT_TILE, ATT_TILE), F32)] * 2
            + [pltpu.VMEM((N_HEADS, ATT_TILE, ATT_TILE), BF16)] * 2
            + [stat, stat]
            + [stat, stat]
            + [stat]
            + [pltpu.VMEM((N_HEADS, HEAD_DIM + SUBLANES, ATT_TILE), F32)])


def _attn_specs(seq):
    nq = seq // ATT_TILE
    q_tile = pl.BlockSpec((1, BRANCH, ATT_TILE), lambda b, i: (b * nq + i, 0, 0))
    keys = pl.BlockSpec((seq, 2 * BRANCH), lambda b, i: (b, 0))
    values = pl.BlockSpec((nq, BRANCH, ATT_TILE), lambda b, i: (b, 0, 0))
    gate = pl.BlockSpec((ATT_TILE, BRANCH), lambda b, i: (b * nq + i, 0))
    qx = pl.BlockSpec((N_HEADS, LANES, ATT_TILE), lambda b, i: (0, 0, 0),
                      pipeline_mode=pl.Buffered(1))
    return nq, q_tile, keys, values, gate, qx


def _top_k_rows(gate, row_f):
    sel = jnp.zeros(gate.shape, F32)
    for _ in range(MOBA_TOPK):
        top = jnp.max(gate, axis=0, keepdims=True)
        first = jnp.min(jnp.where(gate == top, row_f, 1e9), axis=0, keepdims=True)
        pick = row_f == first
        sel = jnp.where(pick, 1.0, sel)
        gate = jnp.where(pick, -jnp.inf, gate)
    return sel > 0.5


def _moba_kernel(qt_ref, k_ref, vt_ref, sel_ref, g_ref, qx_ref, o_ref, qft_buf, *bufs):
    i = pl.program_id(1)
    n_blk = sel_ref.shape[2]
    unused = jnp.zeros((LANES - SEL_LANE0 - n_blk, ATT_TILE), BF16)
    for h, qh in enumerate(_head_operands(qt_ref[0])):
        qft_buf[h] = jnp.concatenate(
            [qh, qx_ref[h, 0:SEL_LANE0, :].astype(BF16), sel_ref[0, h], unused], axis=0)

    causal = (lax.broadcasted_iota(jnp.int32, (ATT_TILE, ATT_TILE), 0)
              <= lax.broadcasted_iota(jnp.int32, (ATT_TILE, ATT_TILE), 1))
    ops = _SweepOps(k_ref, vt_ref, qft_buf, bufs[0:2], bufs[2:4], bufs[4:6], bufs[6:8], *bufs[8:])
    _flash_sweep(ops, i + 1, lambda s: jnp.where(causal, s, NEG), g_ref, o_ref)


def _moba(qt, k, vt, sel, g, qx, batch, seq):
    nq, q_tile, keys, values, gate, qx_spec = _attn_specs(seq)
    n_blk = seq // MOBA_BLOCK
    assert SEL_LANE0 + n_blk <= LANES and 3 * ALIBI_PIECES <= SEL_LANE0
    return pl.pallas_call(
        _moba_kernel,
        grid=(batch, nq),
        in_specs=[q_tile, keys, values,
                  pl.BlockSpec((1, N_HEADS, n_blk, ATT_TILE), lambda b, i: (b * nq + i, 0, 0, 0)),
                  gate, qx_spec],
        out_specs=gate,
        out_shape=jax.ShapeDtypeStruct(g.shape, F32),
        scratch_shapes=_attn_scratch(),
        compiler_params=_params(2),
        name="moba",
    )(qt, k, vt, sel, g, qx)


def _dilated_multiplicity_table():
    idx = np.arange(ATT_TILE)
    delta = (np.arange(DIL_GROUPS_BACK + 1)[:, None, None] * ATT_TILE
             + idx[None, None, :] - idx[None, :, None])
    mult = np.zeros(delta.shape, np.float64)
    for window, dil in DIL_PATTERNS:
        mult += (delta >= 0) & (delta <= window) & (delta % dil == 0)
    table = np.where(mult > 0, np.log2(np.maximum(mult, 1.0)), NEG)
    return jnp.asarray(np.concatenate([table, np.full_like(table[:1], NEG)]), F32)


def _dilated_kernel(qt_ref, qt_next_ref, k_ref, vt_ref, t_ref, g_ref, qx_ref, o_ref, qft_buf, *bufs):
    i = pl.program_id(1)
    n_steps = DIL_GROUPS_BACK + 1
    ops = _SweepOps(k_ref, vt_ref, qft_buf, bufs[0:2], bufs[2:4], bufs[4:6], bufs[6:8], *bufs[8:])

    def group_of(tile, t):
        return jnp.where(t <= tile, tile - t, 0)

    def table_of(tile, t):
        return t_ref[jnp.where(t <= tile, t, n_steps)]

    def prepare(tile, q_ref):
        for h, qh in enumerate(_head_operands(q_ref[0])):
            qft_buf[h] = jnp.concatenate([qh, qx_ref[h].astype(BF16)], axis=0)
        ops.issue_scores(tile, 0, table_of(tile, 0))

    ops.init()

    @pl.when(i == 0)
    def _():
        prepare(i, qt_ref)

    for t in range(n_steps):
        x = t % 2
        if t + 1 < n_steps:
            ops.issue_scores(group_of(i, t + 1), 1 - x, table_of(i, t + 1))
        if t >= 1:
            ops.fold_values(group_of(i, t - 1), 1 - x)
        ops.softmax(x, issued_max=False)
    prepare(jnp.minimum(i + 1, pl.num_programs(1) - 1), qt_next_ref)
    ops.fold_values(group_of(i, n_steps - 1), (n_steps - 1) % 2)
    ops.finish(g_ref, o_ref)


def _dilated(qt, k, vt, table, g, qx, batch, seq):
    nq, q_tile, keys, values, gate, qx_spec = _attn_specs(seq)
    assert DIL_GROUPS_BACK % 2 == 0
    q_next = pl.BlockSpec((1, BRANCH, ATT_TILE),
                          lambda b, i: (b * nq + jnp.minimum(i + 1, nq - 1), 0, 0))
    return pl.pallas_call(
        _dilated_kernel,
        grid=(batch, nq),
        in_specs=[q_tile, q_next, keys, values,
                  pl.BlockSpec(table.shape, lambda b, i: (0, 0, 0), pipeline_mode=pl.Buffered(1)),
                  gate, qx_spec],
        out_specs=gate,
        out_shape=jax.ShapeDtypeStruct(g.shape, F32),
        scratch_shapes=_attn_scratch(),
        compiler_params=_params(2),
        name="dilated",
    )(qt, qt, k, vt, table, g, qx)


def _conv_module(z, halo, has_history, w_ref, b_ref, lng_ref, lnb_ref, pw_ref, pwb_ref, u_buf):
    def glu(z):
        return z[:, 0:BRANCH] * jax.nn.sigmoid(z[:, BRANCH:2 * BRANCH])

    u_buf[0, 0:CONV_HALO, :] = jnp.where(has_history, glu(halo), 0.0)
    u_buf[0, CONV_HALO:, :] = glu(z)
    shifted = CONV_HALO + ROW_TILE - SUBLANES
    for phase in range(1, SUBLANES):
        u_buf[phase, 0:shifted, :] = u_buf[0, phase:phase + shifted, :]
    acc = jnp.zeros((ROW_TILE, BRANCH), F32) + b_ref[...]
    first = CONV_HALO - (CONV_WIDTH - 1)
    for tap in range(CONV_WIDTH):
        phase, start = (first + tap) % SUBLANES, (first + tap) // SUBLANES * SUBLANES
        acc = acc + w_ref[tap:tap + 1, :] * u_buf[phase, start:start + ROW_TILE, :]
    mu = jnp.mean(acc, axis=-1, keepdims=True)
    cen = acc - mu
    var = jnp.mean(cen * cen, axis=-1, keepdims=True)
    un = cen * lax.rsqrt(var + EPS) * lng_ref[...] + lnb_ref[...]
    y = jnp.dot(_silu(un).astype(BF16), pw_ref[...], preferred_element_type=F32) + pwb_ref[...]
    return y * _silu(z[:, 2 * BRANCH:3 * BRANCH])


def _gla_sum_matrices():
    c = GLA_CHUNK
    i = np.arange(c)[:, None]
    t = np.arange(c)[None, :]
    mats = [t <= i]
    for l in range(GLA_LEVELS):
        h = (c // 2) >> l
        mid = (i // (2 * h)) * (2 * h) + h
        later = (i & h) != 0
        mats.append((later & (t >= mid) & (t <= i)) | (~later & (t > i) & (t < mid)))
    return jnp.asarray(np.concatenate(mats, axis=0), BF16)


def _gla_tile(z_ref, sums_ref, wg_ref, bg_ref, gn_ref, bd_ref, o_ref, state_ref):
    c = GLA_CHUNK
    nh = GLA_HEADS
    kw = nh * GLA_DK
    vw = nh * GLA_DV

    row = lax.broadcasted_iota(jnp.int32, (c, kw), 0)
    qi = lax.broadcasted_iota(jnp.int32, (c, nh * c), 0)
    kj = lax.broadcasted_iota(jnp.int32, (c, nh * c), 1) % c
    level_mask = [(qi >> (GLA_LEVELS - l)) == (kj >> (GLA_LEVELS - l)) for l in range(GLA_LEVELS)]
    diag_mask = qi == kj
    k_head = (lax.broadcasted_iota(jnp.int32, (nh * c, kw), 0) // c
              == lax.broadcasted_iota(jnp.int32, (nh * c, kw), 1) // GLA_DK)
    v_head = (lax.broadcasted_iota(jnp.int32, (nh * c, vw), 0) // c
              == lax.broadcasted_iota(jnp.int32, (nh * c, vw), 1) // GLA_DV)
    s_head = (lax.broadcasted_iota(jnp.int32, (vw, kw), 0) // GLA_DV
              == lax.broadcasted_iota(jnp.int32, (vw, kw), 1) // GLA_DK)

    k_head_bf = jnp.where(k_head, 1.0, 0.0).astype(BF16)

    def per_head_keys(kt):
        return jnp.concatenate([kt.astype(BF16)] * nh, axis=0) * k_head_bf

    def chunk_of(b, rows):
        q = z_ref[b, rows, 0:kw] * GLA_DK ** -0.5
        k = z_ref[b, rows, kw:2 * kw]
        v = z_ref[b, rows, 2 * kw:2 * kw + vw]
        gd = z_ref[b, rows, 2 * kw + vw:2 * kw + 2 * vw]
        lr = z_ref[b, rows, 2 * kw + 2 * vw:2 * kw + 2 * vw + LR_PAD]

        g = jnp.dot(lr.astype(BF16), wg_ref[...], preferred_element_type=F32) + bg_ref[...]
        la = (jnp.minimum(g, 0.0) - jnp.log(1.0 + jnp.exp(-jnp.abs(g)))) / GLA_TAU
        a1 = la.astype(BF16)
        a2 = (la - a1.astype(F32)).astype(BF16)
        parts = jnp.dot(sums_ref[...], jnp.concatenate([a1, a2], axis=1),
                        preferred_element_type=F32)
        sums = parts[:, 0:kw] + parts[:, kw:2 * kw]
        bc = sums[0:c]

        attn = jnp.where(diag_mask,
                         lax.dot_general(q.astype(BF16), per_head_keys(k), _NT,
                                         preferred_element_type=F32), 0.0)
        for l in range(GLA_LEVELS):
            later = (row & ((c // 2) >> l)) != 0
            scaled = jnp.where(later, q, k) * jnp.exp(sums[(1 + l) * c:(2 + l) * c])
            qt = jnp.where(later, scaled, 0.0).astype(BF16)
            a = lax.dot_general(qt, per_head_keys(jnp.where(later, 0.0, scaled)), _NT,
                                preferred_element_type=F32)
            attn = attn + (a if l == 0 else jnp.where(level_mask[l], a, 0.0))

        vb = v.astype(BF16)
        v_stack = jnp.where(v_head, jnp.concatenate([vb] * nh, axis=0), jnp.zeros((), BF16))
        o = jnp.dot(attn.astype(BF16), v_stack, preferred_element_type=F32)

        state = state_ref[b]
        o = o + lax.dot_general((q * jnp.exp(bc)).astype(BF16), state.astype(BF16), _NT,
                                preferred_element_type=F32)
        b_last = bc[c - 1:c, :]
        k_dec = (k * jnp.exp(b_last - bc)).astype(BF16)
        upd = lax.dot_general(vb, k_dec, _TN, preferred_element_type=F32)
        state_ref[b] = state * jnp.exp(b_last) + jnp.where(s_head, upd, 0.0)

        on = o * lax.rsqrt(_group_mean_sq(o, bd_ref[...]) + EPS) * gn_ref[...]
        o_ref[b, rows, :] = on * _silu(gd)

    for ci in range(ROW_TILE // c):
        for b in range(z_ref.shape[0]):
            chunk_of(b, pl.ds(ci * c, c))


def _tail_kernel(x_ref, ya_ref, yb_ref, zc_ref, halo_ref, zd_ref,
                 cw_ref, cb_ref, lng_ref, lnb_ref, pw_ref, pwb_ref,
                 sums_ref, wg_ref, bg_ref, gn_ref, bd_ref, wo_ref,
                 o_ref, u_buf, yd_buf, state_ref):
    j = pl.program_id(0)

    @pl.when(j == 0)
    def _():
        state_ref[...] = jnp.zeros_like(state_ref)

    def project(y, g):
        return jnp.dot(y.astype(BF16), wo_ref[g * BRANCH:(g + 1) * BRANCH, :],
                       preferred_element_type=F32)

    _gla_tile(zd_ref, sums_ref, wg_ref, bg_ref, gn_ref, bd_ref, yd_buf, state_ref)
    for b in range(x_ref.shape[0]):
        yc = _conv_module(zc_ref[b], halo_ref[b], j > 0, cw_ref, cb_ref, lng_ref, lnb_ref,
                          pw_ref, pwb_ref, u_buf.at[b])
        o_ref[b] = (x_ref[b] + project(ya_ref[b], 0) + project(yb_ref[b], 1) + project(yc, 2)
                    + project(yd_buf[b], 3))


def _tail(x2, ya, yb, zc, zd, conv_consts, gla_consts, w_out_all, layer, batch, seq):
    per = ROW_TILE // CONV_HALO
    tile = lambda cols: pl.BlockSpec((batch, ROW_TILE, cols), lambda j: (0, j, 0))
    whole = lambda a: pl.BlockSpec(a.shape, lambda j: (0,) * a.ndim, pipeline_mode=pl.Buffered(1))
    by_seq = lambda a: a.reshape(batch, seq, a.shape[-1])
    consts = (*conv_consts, *gla_consts, w_out_all)
    w_out_spec = pl.BlockSpec((None,) + w_out_all.shape[1:], lambda j: (layer, 0, 0),
                              pipeline_mode=pl.Buffered(1))
    out = pl.pallas_call(
        _tail_kernel,
        grid=(seq // ROW_TILE,),
        in_specs=[tile(D_MODEL), tile(BRANCH), tile(BRANCH), tile(ZC_COLS),
                  pl.BlockSpec((batch, CONV_HALO, ZC_COLS),
                               lambda j: (0, jnp.maximum(j * per - 1, 0), 0)),
                  tile(ZD_COLS)] + [whole(a) for a in consts[:-1]] + [w_out_spec],
        out_specs=tile(D_MODEL),
        out_shape=jax.ShapeDtypeStruct((batch, seq, D_MODEL), F32),
        scratch_shapes=[pltpu.VMEM((batch, SUBLANES, CONV_HALO + ROW_TILE, BRANCH), F32),
                        pltpu.VMEM((batch, ROW_TILE, BRANCH), F32),
                        pltpu.VMEM((batch, GLA_HEADS * GLA_DV, GLA_HEADS * GLA_DK), F32)],
        compiler_params=_params(1),
        name="tail",
    )(by_seq(x2), by_seq(ya), by_seq(yb), by_seq(zc), by_seq(zc), by_seq(zd), *consts)
    return out.reshape(batch * seq, D_MODEL)


def _pack_w_in(w_in):
    col = lambda j: w_in[..., j * BRANCH:(j + 1) * BRANCH]
    gla0 = 11 * BRANCH
    qkv = 2 * GLA_HEADS * GLA_DK + BRANCH
    pad = jnp.zeros(w_in.shape[:-1] + (LR_PAD - GLA_RANK,), w_in.dtype)
    wn = jnp.concatenate([col(1), col(3), col(5), col(7), w_in[..., 8 * BRANCH:gla0 + qkv],
                          w_in[..., gla0 + qkv + GLA_RANK:], w_in[..., gla0 + qkv:gla0 + qkv + GLA_RANK],
                          pad], axis=-1)
    wt = jnp.swapaxes(jnp.concatenate([col(0), col(2), col(4), col(6)], axis=-1), -1, -2)
    return wn.astype(BF16), wt.astype(BF16)


def _layer(x2, batch, seq, consts, layer, big_weights, norm_g, q_gain_a, k_gain_a, q_gain_b, k_gain_b,
           conv_w, conv_b, conv_ln_g, conv_ln_b, conv_pw_w, conv_pw_b, gla_gate_w, gla_gate_b, gla_norm_g):
    bd, kx, qx_moba, qx_dil, dil_table, gla_sums = consts
    wn_all, wt_all, w_out_all = big_weights
    kgains = jnp.stack([jnp.tile(k_gain_a, N_HEADS), jnp.tile(k_gain_b, N_HEADS)])
    qgains = jnp.stack([jnp.tile(q_gain_a, N_HEADS), jnp.tile(q_gain_b, N_HEADS)])[:, :, None]
    (qat, ka, vat, ga, sel, qbt, kb, vbt, gb, zc, zd) = _inproj(
        x2, norm_g[None, :], wn_all, wt_all, layer, kgains, qgains, bd, kx, seq)
    ya = _moba(qat, ka, vat, sel, ga, qx_moba, batch, seq)
    yb = _dilated(qbt, kb, vbt, dil_table, gb, qx_dil, batch, seq)
    wg = jnp.concatenate([gla_gate_w, jnp.zeros((LR_PAD - GLA_RANK, gla_gate_w.shape[1]), F32)],
                         axis=0).astype(BF16)
    conv_consts = (conv_w, conv_b[None, :], conv_ln_g[None, :], conv_ln_b[None, :],
                   conv_pw_w.astype(BF16), conv_pw_b[None, :])
    gla_consts = (gla_sums, wg, gla_gate_b[None, :], jnp.tile(gla_norm_g, GLA_HEADS)[None, :], bd)
    return _tail(x2, ya, yb, zc, zd, conv_consts, gla_consts, w_out_all, layer, batch, seq)


def kernel(x, norm_g, w_in, q_gain_a, k_gain_a, q_gain_b, k_gain_b, conv_w, conv_b, conv_ln_g, conv_ln_b,
           conv_pw_w, conv_pw_b, gla_gate_w, gla_gate_b, gla_norm_g, w_out):
    batch, seq, d = x.shape
    assert d == D_MODEL and seq % ROW_TILE == 0 and ROW_TILE == ATT_TILE
    group = np.arange(BRANCH) // HEAD_DIM
    bd = jnp.asarray((group[:, None] == group[None, :]) / HEAD_DIM, BF16)
    heads = np.arange(N_HEADS)
    consts = (bd, _key_position_lanes(seq),
              _query_alibi_rows(2.0 ** -(1.0 + 2 * heads)),
              _query_alibi_rows(2.0 ** -(2.0 + 2 * heads)),
              _dilated_multiplicity_table(), _gla_sum_matrices())
    x2 = x.reshape(batch * seq, d)
    big_weights = (*_pack_w_in(w_in), w_out.astype(BF16))
    params = (norm_g, q_gain_a, k_gain_a, q_gain_b, k_gain_b, conv_w, conv_b, conv_ln_g,
              conv_ln_b, conv_pw_w, conv_pw_b, gla_gate_w, gla_gate_b, gla_norm_g)
    for layer in range(norm_g.shape[0]):
        x2 = _layer(x2, batch, seq, consts, layer, big_weights, *(p[layer] for p in params))
    return x2.reshape(batch, seq, d)
```

```python
import functools

import numpy as np
import jax
import jax.numpy as jnp
from jax import lax
from jax.experimental import pallas as pl
from jax.experimental.pallas import tpu as pltpu

F32 = jnp.float32
BF16 = jnp.bfloat16

D_MODEL = 1024
BRANCH = 256
HEAD_DIM = 64
N_HEADS = BRANCH // HEAD_DIM
MOBA_BLOCK = 256
MOBA_TOPK = 3
DIL_PATTERNS = ((128, 1), (512, 4), (2048, 16))
CONV_WIDTH = 31
GLA_HEADS = 4
GLA_DK = 32
GLA_DV = 64
GLA_RANK = 16
GLA_TAU = 16.0
EPS = 1e-6
NEG = -1e30
LOG2E = 1.4426950408889634

LANES = 128
SUBLANES = 8
ROW_TILE = 512
ATT_TILE = 512
BLOCKS_PER_TILE = ATT_TILE // MOBA_BLOCK
DIL_GROUPS_BACK = max(w for w, _ in DIL_PATTERNS) // ATT_TILE
ALIBI_PIECES = 4
SEL_LANE0 = 16
MASK_BIAS = 2.0 ** 100
M_INIT = -1e29
GLA_CHUNK = 128
GLA_LEVELS = 7
CONV_HALO = 32
LR_PAD = 128
VMEM_LIMIT = 56 * 1024 * 1024

WT_ROWS = 4 * BRANCH
ZC_COLS = 3 * BRANCH
ZD_COLS = 2 * GLA_HEADS * GLA_DK + 2 * BRANCH + LR_PAD
WN_COLS = 4 * BRANCH + ZC_COLS + ZD_COLS

_NT = (((1,), (1,)), ((), ()))
_TN = (((0,), (0,)), ((), ()))


def _params(n_grid):
    return pltpu.CompilerParams(dimension_semantics=("arbitrary",) * n_grid,
                                vmem_limit_bytes=VMEM_LIMIT)


def _silu(x):
    return x * jax.nn.sigmoid(x)


def _group_mean_sq(z, bd):
    z2 = z * z
    hi = z2.astype(BF16)
    lo = (z2 - hi.astype(F32)).astype(BF16)
    return (jnp.dot(hi, bd, preferred_element_type=F32)
            + jnp.dot(lo, bd, preferred_element_type=F32))


def _inproj_kernel(tiles_per_seq, x_ref, ng_ref, wn_ref, wt_ref, kgain_ref, qgain_ref, bd_ref, kx_ref,
                   qat_ref, ka_ref, vat_ref, ga_ref, sel_ref,
                   qbt_ref, kb_ref, vbt_ref, gb_ref, zc_ref, zd_ref, km_buf):
    tile = pl.program_id(0) % tiles_per_seq

    @pl.when(tile == 0)
    def _():
        km_buf[...] = jnp.zeros_like(km_buf)

    x = x_ref[...]
    ms = jnp.mean(x * x, axis=-1, keepdims=True)
    h = (x * lax.rsqrt(ms + EPS) * ng_ref[...]).astype(BF16)
    bd = bd_ref[...]
    kx = kx_ref[...]

    def proj(c0, width):
        return jnp.dot(h, wn_ref[:, c0:c0 + width], preferred_element_type=F32)

    def proj_t(r0):
        return lax.dot_general(wt_ref[r0:r0 + BRANCH, :], h, _NT, preferred_element_type=F32)

    def head_norm(z, row):
        return z * lax.rsqrt(_group_mean_sq(z, bd) + EPS) * kgain_ref[row:row + 1, :]

    def head_norm_t(zt, idx):
        parts = []
        for g in range(N_HEADS):
            part = zt[g * HEAD_DIM:(g + 1) * HEAD_DIM]
            parts.append(part * lax.rsqrt(jnp.mean(part * part, axis=0, keepdims=True) + EPS))
        return jnp.concatenate(parts, axis=0) * qgain_ref[idx]

    def store_keys(ref, kn):
        for hp in range(BRANCH // LANES):
            ref[:, 2 * hp * LANES:(2 * hp + 1) * LANES] = kn[:, hp * LANES:(hp + 1) * LANES].astype(BF16)
            ref[:, (2 * hp + 1) * LANES:(2 * hp + 2) * LANES] = kx

    qa = head_norm_t(proj_t(0), 0)
    ka = head_norm(proj(0, BRANCH), 0)
    store_keys(ka_ref, ka)
    vat_ref[0] = proj_t(BRANCH).astype(BF16)
    ga_ref[...] = proj(BRANCH, BRANCH)

    for blk in range(BLOCKS_PER_TILE):
        km_buf[pl.ds(tile * BLOCKS_PER_TILE + blk, 1), :] = jnp.mean(
            ka[blk * MOBA_BLOCK:(blk + 1) * MOBA_BLOCK], axis=0, keepdims=True)
    km = km_buf[...]
    n_blk = km.shape[0]
    blk = lax.broadcasted_iota(jnp.int32, (n_blk, ROW_TILE), 0)
    own = (tile * BLOCKS_PER_TILE
           + lax.broadcasted_iota(jnp.int32, (n_blk, ROW_TILE), 1) // MOBA_BLOCK)
    blk_f = blk.astype(F32)
    past = blk < own
    for head, qh in enumerate(_head_operands(qa)):
        gate = jnp.dot(km[:, (head // 2) * LANES:(head // 2 + 1) * LANES], qh,
                       precision=lax.Precision.HIGHEST, preferred_element_type=F32)
        keep = (past & _top_k_rows(jnp.where(past, gate, -jnp.inf), blk_f)) | (blk == own)
        sel_ref[0, head] = jnp.where(keep, 0.0, -MASK_BIAS).astype(BF16)

    qat_ref[0] = (qa * (HEAD_DIM ** -0.5 * LOG2E)).astype(BF16)
    qbt_ref[0] = (head_norm_t(proj_t(2 * BRANCH), 1) * (HEAD_DIM ** -0.5 * LOG2E)).astype(BF16)
    store_keys(kb_ref, head_norm(proj(2 * BRANCH, BRANCH), 1))
    vbt_ref[0] = proj_t(3 * BRANCH).astype(BF16)
    gb_ref[...] = proj(3 * BRANCH, BRANCH)

    zc_ref[...] = proj(4 * BRANCH, ZC_COLS)
    zd_ref[...] = proj(4 * BRANCH + ZC_COLS, ZD_COLS)


def _inproj(x2, ng, wn_all, wt_all, layer, kgains, qgains, bd, kx, seq):
    n = x2.shape[0]
    nt = n // ROW_TILE
    per_seq = seq // ROW_TILE
    row = lambda i: (i, 0)
    const = lambda i: (0, 0)

    def nat(cols, dtype):
        return (jax.ShapeDtypeStruct((n, cols), dtype), pl.BlockSpec((ROW_TILE, cols), row))

    def tr(dtype):
        return (jax.ShapeDtypeStruct((nt, BRANCH, ROW_TILE), dtype),
                pl.BlockSpec((1, BRANCH, ROW_TILE), lambda i: (i, 0, 0)))

    n_blk = seq // MOBA_BLOCK
    choice = (jax.ShapeDtypeStruct((nt, N_HEADS, n_blk, ROW_TILE), BF16),
              pl.BlockSpec((1, N_HEADS, n_blk, ROW_TILE), lambda i: (i, 0, 0, 0)))
    outs = [tr(BF16), nat(2 * BRANCH, BF16), tr(BF16), nat(BRANCH, F32), choice,
            tr(BF16), nat(2 * BRANCH, BF16), tr(BF16), nat(BRANCH, F32),
            nat(ZC_COLS, F32), nat(ZD_COLS, F32)]
    return pl.pallas_call(
        functools.partial(_inproj_kernel, per_seq),
        grid=(nt,),
        in_specs=[pl.BlockSpec((ROW_TILE, D_MODEL), row),
                  pl.BlockSpec((1, D_MODEL), const),
                  pl.BlockSpec((None, D_MODEL, WN_COLS), lambda i: (layer, 0, 0)),
                  pl.BlockSpec((None, WT_ROWS, D_MODEL), lambda i: (layer, 0, 0)),
                  pl.BlockSpec((2, BRANCH), const),
                  pl.BlockSpec((2, BRANCH, 1), lambda i: (0, 0, 0)),
                  pl.BlockSpec((BRANCH, BRANCH), const),
                  pl.BlockSpec((ROW_TILE, LANES), lambda i: (i % per_seq, 0))],
        out_specs=[o[1] for o in outs],
        out_shape=[o[0] for o in outs],
        scratch_shapes=[pltpu.VMEM((n_blk, BRANCH), F32)],
        compiler_params=_params(1),
        name="inproj",
    )(x2, ng, wn_all, wt_all, kgains, qgains, bd, kx)


def _key_position_lanes(seq):
    pos = np.arange(seq)
    c, n = pos % MOBA_BLOCK, pos // MOBA_BLOCK
    kx = np.zeros((seq, LANES), np.float32)
    p = ALIBI_PIECES
    kx[:, 0:p] = (c // 16)[:, None]
    kx[:, p:2 * p] = (c % 16)[:, None]
    kx[:, 2 * p:3 * p] = n[:, None]
    kx[pos, SEL_LANE0 + n] = 1.0
    return jnp.asarray(kx, BF16)


def _query_alibi_rows(slopes):
    pieces, rest = [], LOG2E
    for _ in range(ALIBI_PIECES):
        piece = float(np.asarray(rest, dtype=BF16).astype(np.float64))
        pieces.append(piece)
        rest -= piece
    p = ALIBI_PIECES
    qx = np.zeros((len(slopes), LANES, ATT_TILE), np.float32)
    for h, slope in enumerate(slopes):
        for g, weight in enumerate((16.0, 1.0, float(MOBA_BLOCK))):
            qx[h, g * p:(g + 1) * p, :] = np.asarray([weight * slope * piece for piece in pieces])[:, None]
    return jnp.asarray(qx, F32)


class _SweepOps:
    def __init__(self, k_ref, vt_ref, qft_buf, s_bufs, p_bufs, a_bufs, smax_bufs, m_buf, acc_buf):
        self.k_ref, self.vt_ref, self.qft_buf = k_ref, vt_ref, qft_buf
        self.s_bufs, self.p_bufs, self.a_bufs, self.smax_bufs = s_bufs, p_bufs, a_bufs, smax_bufs
        self.m_buf, self.acc_buf = m_buf, acc_buf

    def init(self):
        self.m_buf[...] = jnp.full(self.m_buf.shape, M_INIT, F32)
        self.acc_buf[...] = jnp.zeros(self.acc_buf.shape, F32)

    def issue_scores(self, group, x, bias=None):
        rows = pl.ds(pl.multiple_of(group * ATT_TILE, ATT_TILE), ATT_TILE)
        keys = [self.k_ref[rows, 2 * pair * LANES:2 * (pair + 1) * LANES]
                for pair in range(N_HEADS // 2)]
        for h in range(N_HEADS):
            s = jnp.dot(keys[h // 2], self.qft_buf[h], preferred_element_type=F32)
            if bias is None:
                self.smax_bufs[x][h] = jnp.max(s, axis=0, keepdims=True)
            else:
                s = s + bias
            self.s_bufs[x][h] = s

    def softmax(self, x, mask=None, issued_max=True):
        for h in range(N_HEADS):
            s = self.s_bufs[x][h]
            if mask is not None:
                s = mask(s)
            if mask is None and issued_max:
                group_max = self.smax_bufs[x][h]
            else:
                group_max = jnp.max(s, axis=0, keepdims=True)
            m_old = self.m_buf[h]
            m_new = jnp.maximum(m_old, group_max)
            self.m_buf[h] = m_new
            self.a_bufs[x][h] = jnp.exp2(m_old - m_new)
            self.p_bufs[x][h] = jnp.exp2(s - m_new).astype(BF16)

    def fold_values(self, group, x):
        ones = jnp.ones((SUBLANES, ATT_TILE), BF16)
        vt = self.vt_ref[group]
        for h in range(N_HEADS):
            lhs = jnp.concatenate([vt[h * HEAD_DIM:(h + 1) * HEAD_DIM, :], ones], axis=0)
            self.acc_buf[h] = (self.a_bufs[x][h] * self.acc_buf[h]
                               + jnp.dot(lhs, self.p_bufs[x][h], preferred_element_type=F32))

    def finish(self, g_ref, o_ref):
        acc = self.acc_buf
        out_t = jnp.concatenate(
            [acc[h, 0:HEAD_DIM, :] / acc[h, HEAD_DIM:HEAD_DIM + 1, :] for h in range(N_HEADS)], axis=0)
        o_ref[...] = out_t.T * _silu(g_ref[...])


def _flash_sweep(ops, n_steps, last_mask, g_ref, o_ref):
    def regular_step(t, x):
        ops.issue_scores(t + 1, 1 - x)
        ops.fold_values(jnp.maximum(t - 1, 0), 1 - x)
        ops.softmax(x)

    def neutral_fold(x):
        ops.a_bufs[x][...] = jnp.ones(ops.a_bufs[x].shape, F32)
        ops.p_bufs[x][...] = jnp.zeros(ops.p_bufs[x].shape, BF16)

    def final_step(x):
        ops.fold_values(jnp.maximum(n_steps - 2, 0), 1 - x)
        ops.softmax(x, last_mask)
        ops.fold_values(n_steps - 1, x)
        ops.finish(g_ref, o_ref)

    ops.init()
    neutral_fold(1)
    ops.issue_scores(0, 0)
    n_regular = n_steps - 1

    def step_pair(u, _):
        regular_step(2 * u, 0)
        regular_step(2 * u + 1, 1)
        return 0

    lax.fori_loop(0, n_regular // 2, step_pair, 0)
    odd = n_regular % 2

    @pl.when(odd == 1)
    def _():
        regular_step(n_regular - 1, 0)
        final_step(1)

    @pl.when(odd == 0)
    def _():
        final_step(0)


def _head_operands(qt):
    first = lax.broadcasted_iota(jnp.int32, (LANES, qt.shape[1]), 0) < HEAD_DIM
    zero = jnp.zeros((), qt.dtype)
    out = []
    for h in range(N_HEADS):
        pair = qt[(h // 2) * LANES:(h // 2 + 1) * LANES]
        out.append(jnp.where(first, pair, zero) if h % 2 == 0 else jnp.where(first, zero, pair))
    return out


def _attn_scratch():
    stat = pltpu.VMEM((N_HEADS, 1, ATT_TILE), F32)
    return ([pltpu.VMEM((N_HEADS, 2 * LANES, ATT_TILE), BF16)]
            + [pltpu.VMEM((N_HEADS, ATT_TILE, ATT_TILE), F32)] * 2
            + [pltpu.VMEM((N_HEADS, ATT_TILE, ATT_TILE), BF16)] * 2
            + [stat, stat]
            + [stat, stat]
            + [stat]
            + [pltpu.VMEM((N_HEADS, HEAD_DIM + SUBLANES, ATT_TILE), F32)])


def _attn_specs(seq):
    nq = seq // ATT_TILE
    q_tile = pl.BlockSpec((1, BRANCH, ATT_TILE), lambda b, i: (b * nq + i, 0, 0))
    keys = pl.BlockSpec((seq, 2 * BRANCH), lambda b, i: (b, 0))
    values = pl.BlockSpec((nq, BRANCH, ATT_TILE), lambda b, i: (b, 0, 0))
    gate = pl.BlockSpec((ATT_TILE, BRANCH), lambda b, i: (b * nq + i, 0))
    qx = pl.BlockSpec((N_HEADS, LANES, ATT_TILE), lambda b, i: (0, 0, 0),
                      pipeline_mode=pl.Buffered(1))
    return nq, q_tile, keys, values, gate, qx


def _top_k_rows(gate, row_f):
    sel = jnp.zeros(gate.shape, F32)
    for _ in range(MOBA_TOPK):
        top = jnp.max(gate, axis=0, keepdims=True)
        first = jnp.min(jnp.where(gate == top, row_f, 1e9), axis=0, keepdims=True)
        pick = row_f == first
        sel = jnp.where(pick, 1.0, sel)
        gate = jnp.where(pick, -jnp.inf, gate)
    return sel > 0.5


def _moba_kernel(qt_ref, k_ref, vt_ref, sel_ref, g_ref, qx_ref, o_ref, qft_buf, *bufs):
    i = pl.program_id(1)
    n_blk = sel_ref.shape[2]
    unused = jnp.zeros((LANES - SEL_LANE0 - n_blk, ATT_TILE), BF16)
    for h, qh in enumerate(_head_operands(qt_ref[0])):
        qft_buf[h] = jnp.concatenate(
            [qh, qx_ref[h, 0:SEL_LANE0, :].astype(BF16), sel_ref[0, h], unused], axis=0)

    causal = (lax.broadcasted_iota(jnp.int32, (ATT_TILE, ATT_TILE), 0)
              <= lax.broadcasted_iota(jnp.int32, (ATT_TILE, ATT_TILE), 1))
    ops = _SweepOps(k_ref, vt_ref, qft_buf, bufs[0:2], bufs[2:4], bufs[4:6], bufs[6:8], *bufs[8:])
    _flash_sweep(ops, i + 1, lambda s: jnp.where(causal, s, NEG), g_ref, o_ref)


def _moba(qt, k, vt, sel, g, qx, batch, seq):
    nq, q_tile, keys, values, gate, qx_spec = _attn_specs(seq)
    n_blk = seq // MOBA_BLOCK
    assert SEL_LANE0 + n_blk <= LANES and 3 * ALIBI_PIECES <= SEL_LANE0
    return pl.pallas_call(
        _moba_kernel,
        grid=(batch, nq),
        in_specs=[q_tile, keys, values,
                  pl.BlockSpec((1, N_HEADS, n_blk, ATT_TILE), lambda b, i: (b * nq + i, 0, 0, 0)),
                  gate, qx_spec],
        out_specs=gate,
        out_shape=jax.ShapeDtypeStruct(g.shape, F32),
        scratch_shapes=_attn_scratch(),
        compiler_params=_params(2),
        name="moba",
    )(qt, k, vt, sel, g, qx)


def _dilated_multiplicity_table():
    idx = np.arange(ATT_TILE)
    delta = (np.arange(DIL_GROUPS_BACK + 1)[:, None, None] * ATT_TILE
             + idx[None, None, :] - idx[None, :, None])
    mult = np.zeros(delta.shape, np.float64)
    for window, dil in DIL_PATTERNS:
        mult += (delta >= 0) & (delta <= window) & (delta % dil == 0)
    table = np.where(mult > 0, np.log2(np.maximum(mult, 1.0)), NEG)
    return jnp.asarray(np.concatenate([table, np.full_like(table[:1], NEG)]), F32)


def _dilated_kernel(qt_ref, qt_next_ref, k_ref, vt_ref, t_ref, g_ref, qx_ref, o_ref, qft_buf, *bufs):
    i = pl.program_id(1)
    n_steps = DIL_GROUPS_BACK + 1
    ops = _SweepOps(k_ref, vt_ref, qft_buf, bufs[0:2], bufs[2:4], bufs[4:6], bufs[6:8], *bufs[8:])

    def group_of(tile, t):
        return jnp.where(t <= tile, tile - t, 0)

    def table_of(tile, t):
        return t_ref[jnp.where(t <= tile, t, n_steps)]

    def prepare(tile, q_ref):
        for h, qh in enumerate(_head_operands(q_ref[0])):
            qft_buf[h] = jnp.concatenate([qh, qx_ref[h].astype(BF16)], axis=0)
        ops.issue_scores(tile, 0, table_of(tile, 0))

    ops.init()

    @pl.when(i == 0)
    def _():
        prepare(i, qt_ref)

    for t in range(n_steps):
        x = t % 2
        if t + 1 < n_steps:
            ops.issue_scores(group_of(i, t + 1), 1 - x, table_of(i, t + 1))
        if t >= 1:
            ops.fold_values(group_of(i, t - 1), 1 - x)
        ops.softmax(x, issued_max=False)
    prepare(jnp.minimum(i + 1, pl.num_programs(1) - 1), qt_next_ref)
    ops.fold_values(group_of(i, n_steps - 1), (n_steps - 1) % 2)
    ops.finish(g_ref, o_ref)


def _dilated(qt, k, vt, table, g, qx, batch, seq):
    nq, q_tile, keys, values, gate, qx_spec = _attn_specs(seq)
    assert DIL_GROUPS_BACK % 2 == 0
    q_next = pl.BlockSpec((1, BRANCH, ATT_TILE),
                          lambda b, i: (b * nq + jnp.minimum(i + 1, nq - 1), 0, 0))
    return pl.pallas_call(
        _dilated_kernel,
        grid=(batch, nq),
        in_specs=[q_tile, q_next, keys, values,
                  pl.BlockSpec(table.shape, lambda b, i: (0, 0, 0), pipeline_mode=pl.Buffered(1)),
                  gate, qx_spec],
        out_specs=gate,
        out_shape=jax.ShapeDtypeStruct(g.shape, F32),
        scratch_shapes=_attn_scratch(),
        compiler_params=_params(2),
        name="dilated",
    )(qt, qt, k, vt, table, g, qx)


def _conv_module(z, halo, has_history, w_ref, b_ref, lng_ref, lnb_ref, pw_ref, pwb_ref, u_buf):
    def glu(z):
        return z[:, 0:BRANCH] * jax.nn.sigmoid(z[:, BRANCH:2 * BRANCH])

    u_buf[0, 0:CONV_HALO, :] = jnp.where(has_history, glu(halo), 0.0)
    u_buf[0, CONV_HALO:, :] = glu(z)
    shifted = CONV_HALO + ROW_TILE - SUBLANES
    for phase in range(1, SUBLANES):
        u_buf[phase, 0:shifted, :] = u_buf[0, phase:phase + shifted, :]
    acc = jnp.zeros((ROW_TILE, BRANCH), F32) + b_ref[...]
    first = CONV_HALO - (CONV_WIDTH - 1)
    for tap in range(CONV_WIDTH):
        phase, start = (first + tap) % SUBLANES, (first + tap) // SUBLANES * SUBLANES
        acc = acc + w_ref[tap:tap + 1, :] * u_buf[phase, start:start + ROW_TILE, :]
    mu = jnp.mean(acc, axis=-1, keepdims=True)
    cen = acc - mu
    var = jnp.mean(cen * cen, axis=-1, keepdims=True)
    un = cen * lax.rsqrt(var + EPS) * lng_ref[...] + lnb_ref[...]
    y = jnp.dot(_silu(un).astype(BF16), pw_ref[...], preferred_element_type=F32) + pwb_ref[...]
    return y * _silu(z[:, 2 * BRANCH:3 * BRANCH])


def _gla_sum_matrices():
    c = GLA_CHUNK
    i = np.arange(c)[:, None]
    t = np.arange(c)[None, :]
    mats = [t <= i]
    for l in range(GLA_LEVELS):
        h = (c // 2) >> l
        mid = (i // (2 * h)) * (2 * h) + h
        later = (i & h) != 0
        mats.append((later & (t >= mid) & (t <= i)) | (~later & (t > i) & (t < mid)))
    return jnp.asarray(np.concatenate(mats, axis=0), BF16)


def _gla_tile(z_ref, sums_ref, wg_ref, bg_ref, gn_ref, bd_ref, o_ref, state_ref):
    c = GLA_CHUNK
    nh = GLA_HEADS
    kw = nh * GLA_DK
    vw = nh * GLA_DV

    row = lax.broadcasted_iota(jnp.int32, (c, kw), 0)
    qi = lax.broadcasted_iota(jnp.int32, (c, nh * c), 0)
    kj = lax.broadcasted_iota(jnp.int32, (c, nh * c), 1) % c
    level_mask = [(qi >> (GLA_LEVELS - l)) == (kj >> (GLA_LEVELS - l)) for l in range(GLA_LEVELS)]
    diag_mask = qi == kj
    k_head = (lax.broadcasted_iota(jnp.int32, (nh * c, kw), 0) // c
              == lax.broadcasted_iota(jnp.int32, (nh * c, kw), 1) // GLA_DK)
    v_head = (lax.broadcasted_iota(jnp.int32, (nh * c, vw), 0) // c
              == lax.broadcasted_iota(jnp.int32, (nh * c, vw), 1) // GLA_DV)
    s_head = (lax.broadcasted_iota(jnp.int32, (vw, kw), 0) // GLA_DV
              == lax.broadcasted_iota(jnp.int32, (vw, kw), 1) // GLA_DK)

    k_head_bf = jnp.where(k_head, 1.0, 0.0).astype(BF16)

    def per_head_keys(kt):
        return jnp.concatenate([kt.astype(BF16)] * nh, axis=0) * k_head_bf

    def chunk_of(b, rows):
        q = z_ref[b, rows, 0:kw] * GLA_DK ** -0.5
        k = z_ref[b, rows, kw:2 * kw]
        v = z_ref[b, rows, 2 * kw:2 * kw + vw]
        gd = z_ref[b, rows, 2 * kw + vw:2 * kw + 2 * vw]
        lr = z_ref[b, rows, 2 * kw + 2 * vw:2 * kw + 2 * vw + LR_PAD]

        g = jnp.dot(lr.astype(BF16), wg_ref[...], preferred_element_type=F32) + bg_ref[...]
        la = (jnp.minimum(g, 0.0) - jnp.log(1.0 + jnp.exp(-jnp.abs(g)))) / GLA_TAU
        a1 = la.astype(BF16)
        a2 = (la - a1.astype(F32)).astype(BF16)
        parts = jnp.dot(sums_ref[...], jnp.concatenate([a1, a2], axis=1),
                        preferred_element_type=F32)
        sums = parts[:, 0:kw] + parts[:, kw:2 * kw]
        bc = sums[0:c]

        attn = jnp.where(diag_mask,
                         lax.dot_general(q.astype(BF16), per_head_keys(k), _NT,
                                         preferred_element_type=F32), 0.0)
        for l in range(GLA_LEVELS):
            later = (row & ((c // 2) >> l)) != 0
            scaled = jnp.where(later, q, k) * jnp.exp(sums[(1 + l) * c:(2 + l) * c])
            qt = jnp.where(later, scaled, 0.0).astype(BF16)
            a = lax.dot_general(qt, per_head_keys(jnp.where(later, 0.0, scaled)), _NT,
                                preferred_element_type=F32)
            attn = attn + (a if l == 0 else jnp.where(level_mask[l], a, 0.0))

        vb = v.astype(BF16)
        v_stack = jnp.where(v_head, jnp.concatenate([vb] * nh, axis=0), jnp.zeros((), BF16))
        o = jnp.dot(attn.astype(BF16), v_stack, preferred_element_type=F32)

        state = state_ref[b]
        o = o + lax.dot_general((q * jnp.exp(bc)).astype(BF16), state.astype(BF16), _NT,
                                preferred_element_type=F32)
        b_last = bc[c - 1:c, :]
        k_dec = (k * jnp.exp(b_last - bc)).astype(BF16)
        upd = lax.dot_general(vb, k_dec, _TN, preferred_element_type=F32)
        state_ref[b] = state * jnp.exp(b_last) + jnp.where(s_head, upd, 0.0)

        on = o * lax.rsqrt(_group_mean_sq(o, bd_ref[...]) + EPS) * gn_ref[...]
        o_ref[b, rows, :] = on * _silu(gd)

    for ci in range(ROW_TILE // c):
        for b in range(z_ref.shape[0]):
            chunk_of(b, pl.ds(ci * c, c))


def _tail_kernel(x_ref, ya_ref, yb_ref, zc_ref, halo_ref, zd_ref,
                 cw_ref, cb_ref, lng_ref, lnb_ref, pw_ref, pwb_ref,
                 sums_ref, wg_ref, bg_ref, gn_ref, bd_ref, wo_ref,
                 o_ref, u_buf, yd_buf, state_ref):
    j = pl.program_id(0)

    @pl.when(j == 0)
    def _():
        state_ref[...] = jnp.zeros_like(state_ref)

    def project(y, g):
        return jnp.dot(y.astype(BF16), wo_ref[g * BRANCH:(g + 1) * BRANCH, :],
                       preferred_element_type=F32)

    _gla_tile(zd_ref, sums_ref, wg_ref, bg_ref, gn_ref, bd_ref, yd_buf, state_ref)
    for b in range(x_ref.shape[0]):
        yc = _conv_module(zc_ref[b], halo_ref[b], j > 0, cw_ref, cb_ref, lng_ref, lnb_ref,
                          pw_ref, pwb_ref, u_buf.at[b])
        o_ref[b] = (x_ref[b] + project(ya_ref[b], 0) + project(yb_ref[b], 1) + project(yc, 2)
                    + project(yd_buf[b], 3))


def _tail(x2, ya, yb, zc, zd, conv_consts, gla_consts, w_out_all, layer, batch, seq):
    per = ROW_TILE // CONV_HALO
    tile = lambda cols: pl.BlockSpec((batch, ROW_TILE, cols), lambda j: (0, j, 0))
    whole = lambda a: pl.BlockSpec(a.shape, lambda j: (0,) * a.ndim, pipeline_mode=pl.Buffered(1))
    by_seq = lambda a: a.reshape(batch, seq, a.shape[-1])
    consts = (*conv_consts, *gla_consts, w_out_all)
    w_out_spec = pl.BlockSpec((None,) + w_out_all.shape[1:], lambda j: (layer, 0, 0),
                              pipeline_mode=pl.Buffered(1))
    out = pl.pallas_call(
        _tail_kernel,
        grid=(seq // ROW_TILE,),
        in_specs=[tile(D_MODEL), tile(BRANCH), tile(BRANCH), tile(ZC_COLS),
                  pl.BlockSpec((batch, CONV_HALO, ZC_COLS),
                               lambda j: (0, jnp.maximum(j * per - 1, 0), 0)),
                  tile(ZD_COLS)] + [whole(a) for a in consts[:-1]] + [w_out_spec],
        out_specs=tile(D_MODEL),
        out_shape=jax.ShapeDtypeStruct((batch, seq, D_MODEL), F32),
        scratch_shapes=[pltpu.VMEM((batch, SUBLANES, CONV_HALO + ROW_TILE, BRANCH), F32),
                        pltpu.VMEM((batch, ROW_TILE, BRANCH), F32),
                        pltpu.VMEM((batch, GLA_HEADS * GLA_DV, GLA_HEADS * GLA_DK), F32)],
        compiler_params=_params(1),
        name="tail",
    )(by_seq(x2), by_seq(ya), by_seq(yb), by_seq(zc), by_seq(zc), by_seq(zd), *consts)
    return out.reshape(batch * seq, D_MODEL)


def _pack_w_in(w_in):
    col = lambda j: w_in[..., j * BRANCH:(j + 1) * BRANCH]
    gla0 = 11 * BRANCH
    qkv = 2 * GLA_HEADS * GLA_DK + BRANCH
    pad = jnp.zeros(w_in.shape[:-1] + (LR_PAD - GLA_RANK,), w_in.dtype)
    wn = jnp.concatenate([col(1), col(3), col(5), col(7), w_in[..., 8 * BRANCH:gla0 + qkv],
                          w_in[..., gla0 + qkv + GLA_RANK:], w_in[..., gla0 + qkv:gla0 + qkv + GLA_RANK],
                          pad], axis=-1)
    wt = jnp.swapaxes(jnp.concatenate([col(0), col(2), col(4), col(6)], axis=-1).astype(BF16), -1, -2)
    return wn.astype(BF16), wt


def _layer(x2, batch, seq, consts, layer, big_weights, norm_g, q_gain_a, k_gain_a, q_gain_b, k_gain_b,
           conv_w, conv_b, conv_ln_g, conv_ln_b, conv_pw_w, conv_pw_b, gla_gate_w, gla_gate_b, gla_norm_g):
    bd, kx, qx_moba, qx_dil, dil_table, gla_sums = consts
    wn_all, wt_all, w_out_all = big_weights
    kgains = jnp.stack([jnp.tile(k_gain_a, N_HEADS), jnp.tile(k_gain_b, N_HEADS)])
    qgains = jnp.stack([jnp.tile(q_gain_a, N_HEADS), jnp.tile(q_gain_b, N_HEADS)])[:, :, None]
    (qat, ka, vat, ga, sel, qbt, kb, vbt, gb, zc, zd) = _inproj(
        x2, norm_g[None, :], wn_all, wt_all, layer, kgains, qgains, bd, kx, seq)
    ya = _moba(qat, ka, vat, sel, ga, qx_moba, batch, seq)
    yb = _dilated(qbt, kb, vbt, dil_table, gb, qx_dil, batch, seq)
    wg = jnp.concatenate([gla_gate_w, jnp.zeros((LR_PAD - GLA_RANK, gla_gate_w.shape[1]), F32)],
                         axis=0).astype(BF16)
    conv_consts = (conv_w, conv_b[None, :], conv_ln_g[None, :], conv_ln_b[None, :],
                   conv_pw_w.astype(BF16), conv_pw_b[None, :])
    gla_consts = (gla_sums, wg, gla_gate_b[None, :], jnp.tile(gla_norm_g, GLA_HEADS)[None, :], bd)
    return _tail(x2, ya, yb, zc, zd, conv_consts, gla_consts, w_out_all, layer, batch, seq)


def kernel(x, norm_g, w_in, q_gain_a, k_gain_a, q_gain_b, k_gain_b, conv_w, conv_b, conv_ln_g, conv_ln_b,
           conv_pw_w, conv_pw_b, gla_gate_w, gla_gate_b, gla_norm_g, w_out):
    batch, seq, d = x.shape
    assert d == D_MODEL and seq % ROW_TILE == 0 and ROW_TILE == ATT_TILE
    group = np.arange(BRANCH) // HEAD_DIM
    bd = jnp.asarray((group[:, None] == group[None, :]) / HEAD_DIM, BF16)
    heads = np.arange(N_HEADS)
    consts = (bd, _key_position_lanes(seq),
              _query_alibi_rows(2.0 ** -(1.0 + 2 * heads)),
              _query_alibi_rows(2.0 ** -(2.0 + 2 * heads)),
              _dilated_multiplicity_table(), _gla_sum_matrices())
    x2 = x.reshape(batch * seq, d)
    big_weights = (*_pack_w_in(w_in), w_out.astype(BF16))
    params = (norm_g, q_gain_a, k_gain_a, q_gain_b, k_gain_b, conv_w, conv_b, conv_ln_g,
              conv_ln_b, conv_pw_w, conv_pw_b, gla_gate_w, gla_gate_b, gla_norm_g)
    for layer in range(norm_g.shape[0]):
        x2 = _layer(x2, batch, seq, consts, layer, big_weights, *(p[layer] for p in params))
    return x2.reshape(batch, seq, d)
```

```python
import functools

import numpy as np
import jax
import jax.numpy as jnp
from jax import lax
from jax.experimental import pallas as pl
from jax.experimental.pallas import tpu as pltpu

F32 = jnp.float32
BF16 = jnp.bfloat16

D_MODEL = 1024
BRANCH = 256
HEAD_DIM = 64
N_HEADS = BRANCH // HEAD_DIM
MOBA_BLOCK = 256
MOBA_TOPK = 3
DIL_PATTERNS = ((128, 1), (512, 4), (2048, 16))
CONV_WIDTH = 31
GLA_HEADS = 4
GLA_DK = 32
GLA_DV = 64
GLA_RANK = 16
GLA_TAU = 16.0
EPS = 1e-6
NEG = -1e30
LOG2E = 1.4426950408889634

LANES = 128
SUBLANES = 8
ROW_TILE = 512
ATT_TILE = 512
BLOCKS_PER_TILE = ATT_TILE // MOBA_BLOCK
DIL_GROUPS_BACK = max(w for w, _ in DIL_PATTERNS) // ATT_TILE
ALIBI_PIECES = 4
SEL_LANE0 = 16
MASK_BIAS = 2.0 ** 100
M_INIT = -1e29
GLA_CHUNK = 128
GLA_LEVELS = 7
CONV_HALO = 32
LR_PAD = 128
VMEM_LIMIT = 56 * 1024 * 1024

WT_ROWS = 4 * BRANCH
ZC_COLS = 3 * BRANCH
ZD_COLS = 2 * GLA_HEADS * GLA_DK + 2 * BRANCH + LR_PAD
WN_COLS = 4 * BRANCH + ZC_COLS + ZD_COLS

_NT = (((1,), (1,)), ((), ()))
_TN = (((0,), (0,)), ((), ()))


def _params(n_grid):
    return pltpu.CompilerParams(dimension_semantics=("arbitrary",) * n_grid,
                                vmem_limit_bytes=VMEM_LIMIT)


def _silu(x):
    return x * jax.nn.sigmoid(x)


def _group_mean_sq(z, bd):
    z2 = z * z
    hi = z2.astype(BF16)
    lo = (z2 - hi.astype(F32)).astype(BF16)
    return (jnp.dot(hi, bd, preferred_element_type=F32)
            + jnp.dot(lo, bd, preferred_element_type=F32))


def _inproj_kernel(tiles_per_seq, x_ref, ng_ref, wn_ref, wq_ref, kgain_ref, qgain_ref, bd_ref, kx_ref,
                   qat_ref, ka_ref, vat_ref, ga_ref, sel_ref,
                   qbt_ref, kb_ref, vbt_ref, gb_ref, zc_ref, zd_ref, km_buf, wt_ref):
    tile = pl.program_id(0) % tiles_per_seq

    @pl.when(tile == 0)
    def _():
        km_buf[...] = jnp.zeros_like(km_buf)

    @pl.when(pl.program_id(0) == 0)
    def _():
        for r0 in range(0, WT_ROWS, BRANCH):
            wt_ref[r0:r0 + BRANCH, :] = wq_ref[:, r0:r0 + BRANCH].T

    x = x_ref[...]
    ms = jnp.mean(x * x, axis=-1, keepdims=True)
    h = (x * lax.rsqrt(ms + EPS) * ng_ref[...]).astype(BF16)
    bd = bd_ref[...]
    kx = kx_ref[...]

    def proj(c0, width):
        return jnp.dot(h, wn_ref[:, c0:c0 + width], preferred_element_type=F32)

    def proj_t(r0):
        return lax.dot_general(wt_ref[r0:r0 + BRANCH, :], h, _NT, preferred_element_type=F32)

    def head_norm(z, row):
        return z * lax.rsqrt(_group_mean_sq(z, bd) + EPS) * kgain_ref[row:row + 1, :]

    def head_norm_t(zt, idx):
        parts = []
        for g in range(N_HEADS):
            part = zt[g * HEAD_DIM:(g + 1) * HEAD_DIM]
            parts.append(part * lax.rsqrt(jnp.mean(part * part, axis=0, keepdims=True) + EPS))
        return jnp.concatenate(parts, axis=0) * qgain_ref[idx]

    def store_keys(ref, kn):
        for hp in range(BRANCH // LANES):
            ref[:, 2 * hp * LANES:(2 * hp + 1) * LANES] = kn[:, hp * LANES:(hp + 1) * LANES].astype(BF16)
            ref[:, (2 * hp + 1) * LANES:(2 * hp + 2) * LANES] = kx

    qa = head_norm_t(proj_t(0), 0)
    ka = head_norm(proj(0, BRANCH), 0)
    store_keys(ka_ref, ka)
    vat_ref[0] = proj_t(BRANCH).astype(BF16)
    ga_ref[...] = proj(BRANCH, BRANCH)

    for blk in range(BLOCKS_PER_TILE):
        km_buf[pl.ds(tile * BLOCKS_PER_TILE + blk, 1), :] = jnp.mean(
            ka[blk * MOBA_BLOCK:(blk + 1) * MOBA_BLOCK], axis=0, keepdims=True)
    km = km_buf[...]
    n_blk = km.shape[0]
    blk = lax.broadcasted_iota(jnp.int32, (n_blk, ROW_TILE), 0)
    own = (tile * BLOCKS_PER_TILE
           + lax.broadcasted_iota(jnp.int32, (n_blk, ROW_TILE), 1) // MOBA_BLOCK)
    blk_f = blk.astype(F32)
    past = blk < own
    for head, qh in enumerate(_head_operands(qa)):
        gate = jnp.dot(km[:, (head // 2) * LANES:(head // 2 + 1) * LANES], qh,
                       precision=lax.Precision.HIGHEST, preferred_element_type=F32)
        keep = (past & _top_k_rows(jnp.where(past, gate, -jnp.inf), blk_f)) | (blk == own)
        sel_ref[0, head] = jnp.where(keep, 0.0, -MASK_BIAS).astype(BF16)

    qat_ref[0] = (qa * (HEAD_DIM ** -0.5 * LOG2E)).astype(BF16)
    qbt_ref[0] = (head_norm_t(proj_t(2 * BRANCH), 1) * (HEAD_DIM ** -0.5 * LOG2E)).astype(BF16)
    store_keys(kb_ref, head_norm(proj(2 * BRANCH, BRANCH), 1))
    vbt_ref[0] = proj_t(3 * BRANCH).astype(BF16)
    gb_ref[...] = proj(3 * BRANCH, BRANCH)

    zc_ref[...] = proj(4 * BRANCH, ZC_COLS)
    zd_ref[...] = proj(4 * BRANCH + ZC_COLS, ZD_COLS)


def _inproj(x2, ng, wn_all, wt_all, layer, kgains, qgains, bd, kx, seq):
    n = x2.shape[0]
    nt = n // ROW_TILE
    per_seq = seq // ROW_TILE
    row = lambda i: (i, 0)
    const = lambda i: (0, 0)

    def nat(cols, dtype):
        return (jax.ShapeDtypeStruct((n, cols), dtype), pl.BlockSpec((ROW_TILE, cols), row))

    def tr(dtype):
        return (jax.ShapeDtypeStruct((nt, BRANCH, ROW_TILE), dtype),
                pl.BlockSpec((1, BRANCH, ROW_TILE), lambda i: (i, 0, 0)))

    n_blk = seq // MOBA_BLOCK
    choice = (jax.ShapeDtypeStruct((nt, N_HEADS, n_blk, ROW_TILE), BF16),
              pl.BlockSpec((1, N_HEADS, n_blk, ROW_TILE), lambda i: (i, 0, 0, 0)))
    outs = [tr(BF16), nat(2 * BRANCH, BF16), tr(BF16), nat(BRANCH, F32), choice,
            tr(BF16), nat(2 * BRANCH, BF16), tr(BF16), nat(BRANCH, F32),
            nat(ZC_COLS, F32), nat(ZD_COLS, F32)]
    return pl.pallas_call(
        functools.partial(_inproj_kernel, per_seq),
        grid=(nt,),
        in_specs=[pl.BlockSpec((ROW_TILE, D_MODEL), row),
                  pl.BlockSpec((1, D_MODEL), const),
                  pl.BlockSpec((None, D_MODEL, WN_COLS), lambda i: (layer, 0, 0)),
                  pl.BlockSpec((None, D_MODEL, WT_ROWS), lambda i: (layer, 0, 0)),
                  pl.BlockSpec((2, BRANCH), const),
                  pl.BlockSpec((2, BRANCH, 1), lambda i: (0, 0, 0)),
                  pl.BlockSpec((BRANCH, BRANCH), const),
                  pl.BlockSpec((ROW_TILE, LANES), lambda i: (i % per_seq, 0))],
        out_specs=[o[1] for o in outs],
        out_shape=[o[0] for o in outs],
        scratch_shapes=[pltpu.VMEM((n_blk, BRANCH), F32), pltpu.VMEM((WT_ROWS, D_MODEL), BF16)],
        compiler_params=_params(1),
        name="inproj",
    )(x2, ng, wn_all, wt_all, kgains, qgains, bd, kx)


def _key_position_lanes(seq):
    pos = np.arange(seq)
    c, n = pos % MOBA_BLOCK, pos // MOBA_BLOCK
    kx = np.zeros((seq, LANES), np.float32)
    p = ALIBI_PIECES
    kx[:, 0:p] = (c // 16)[:, None]
    kx[:, p:2 * p] = (c % 16)[:, None]
    kx[:, 2 * p:3 * p] = n[:, None]
    kx[pos, SEL_LANE0 + n] = 1.0
    return jnp.asarray(kx, BF16)


def _query_alibi_rows(slopes):
    pieces, rest = [], LOG2E
    for _ in range(ALIBI_PIECES):
        piece = float(np.asarray(rest, dtype=BF16).astype(np.float64))
        pieces.append(piece)
        rest -= piece
    p = ALIBI_PIECES
    qx = np.zeros((len(slopes), LANES, ATT_TILE), np.float32)
    for h, slope in enumerate(slopes):
        for g, weight in enumerate((16.0, 1.0, float(MOBA_BLOCK))):
            qx[h, g * p:(g + 1) * p, :] = np.asarray([weight * slope * piece for piece in pieces])[:, None]
    return jnp.asarray(qx, F32)


class _SweepOps:
    def __init__(self, k_ref, vt_ref, qft_buf, s_bufs, p_bufs, a_bufs, smax_bufs, m_buf, acc_buf):
        self.k_ref, self.vt_ref, self.qft_buf = k_ref, vt_ref, qft_buf
        self.s_bufs, self.p_bufs, self.a_bufs, self.smax_bufs = s_bufs, p_bufs, a_bufs, smax_bufs
        self.m_buf, self.acc_buf = m_buf, acc_buf

    def init(self):
        self.m_buf[...] = jnp.full(self.m_buf.shape, M_INIT, F32)
        self.acc_buf[...] = jnp.zeros(self.acc_buf.shape, F32)

    def issue_scores(self, group, x, bias=None):
        rows = pl.ds(pl.multiple_of(group * ATT_TILE, ATT_TILE), ATT_TILE)
        keys = [self.k_ref[rows, 2 * pair * LANES:2 * (pair + 1) * LANES]
                for pair in range(N_HEADS // 2)]
        for h in range(N_HEADS):
            s = jnp.dot(keys[h // 2], self.qft_buf[h], preferred_element_type=F32)
            if bias is None:
                self.smax_bufs[x][h] = jnp.max(s, axis=0, keepdims=True)
            else:
                s = s + bias
            self.s_bufs[x][h] = s

    def softmax(self, x, mask=None, issued_max=True):
        for h in range(N_HEADS):
            s = self.s_bufs[x][h]
            if mask is not None:
                s = mask(s)
            if mask is None and issued_max:
                group_max = self.smax_bufs[x][h]
            else:
                group_max = jnp.max(s, axis=0, keepdims=True)
            m_old = self.m_buf[h]
            m_new = jnp.maximum(m_old, group_max)
            self.m_buf[h] = m_new
            self.a_bufs[x][h] = jnp.exp2(m_old - m_new)
            self.p_bufs[x][h] = jnp.exp2(s - m_new).astype(BF16)

    def fold_values(self, group, x):
        ones = jnp.ones((SUBLANES, ATT_TILE), BF16)
        vt = self.vt_ref[group]
        for h in range(N_HEADS):
            lhs = jnp.concatenate([vt[h * HEAD_DIM:(h + 1) * HEAD_DIM, :], ones], axis=0)
            self.acc_buf[h] = (self.a_bufs[x][h] * self.acc_buf[h]
                               + jnp.dot(lhs, self.p_bufs[x][h], preferred_element_type=F32))

    def finish(self, g_ref, o_ref):
        acc = self.acc_buf
        out_t = jnp.concatenate(
            [acc[h, 0:HEAD_DIM, :] / acc[h, HEAD_DIM:HEAD_DIM + 1, :] for h in range(N_HEADS)], axis=0)
        o_ref[...] = out_t.T * _silu(g_ref[...])


def _flash_sweep(ops, n_steps, last_mask, g_ref, o_ref):
    def regular_step(t, x):
        ops.issue_scores(t + 1, 1 - x)
        ops.fold_values(jnp.maximum(t - 1, 0), 1 - x)
        ops.softmax(x)

    def neutral_fold(x):
        ops.a_bufs[x][...] = jnp.ones(ops.a_bufs[x].shape, F32)
        ops.p_bufs[x][...] = jnp.zeros(ops.p_bufs[x].shape, BF16)

    def final_step(x):
        ops.fold_values(jnp.maximum(n_steps - 2, 0), 1 - x)
        ops.softmax(x, last_mask)
        ops.fold_values(n_steps - 1, x)
        ops.finish(g_ref, o_ref)

    ops.init()
    neutral_fold(1)
    ops.issue_scores(0, 0)
    n_regular = n_steps - 1

    def step_pair(u, _):
        regular_step(2 * u, 0)
        regular_step(2 * u + 1, 1)
        return 0

    lax.fori_loop(0, n_regular // 2, step_pair, 0)
    odd = n_regular % 2

    @pl.when(odd == 1)
    def _():
        regular_step(n_regular - 1, 0)
        final_step(1)

    @pl.when(odd == 0)
    def _():
        final_step(0)


def _head_operands(qt):
    first = lax.broadcasted_iota(jnp.int32, (LANES, qt.shape[1]), 0) < HEAD_DIM
    zero = jnp.zeros((), qt.dtype)
    out = []
    for h in range(N_HEADS):
        pair = qt[(h // 2) * LANES:(h // 2 + 1) * LANES]
        out.append(jnp.where(first, pair, zero) if h % 2 == 0 else jnp.where(first, zero, pair))
    return out


def _attn_scratch():
    stat = pltpu.VMEM((N_HEADS, 1, ATT_TILE), F32)
    return ([pltpu.VMEM((N_HEADS, 2 * LANES, ATT_TILE), BF16)]
            + [pltpu.VMEM((N_HEADS, ATT_TILE, ATT_TILE), F32)] * 2
            + [pltpu.VMEM((N_HEADS, ATT_TILE, ATT_TILE), BF16)] * 2
            + [stat, stat]
            + [stat, stat]
            + [stat]
            + [pltpu.VMEM((N_HEADS, HEAD_DIM + SUBLANES, ATT_TILE), F32)])


def _attn_specs(seq):
    nq = seq // ATT_TILE
    q_tile = pl.BlockSpec((1, BRANCH, ATT_TILE), lambda b, i: (b * nq + i, 0, 0))
    keys = pl.BlockSpec((seq, 2 * BRANCH), lambda b, i: (b, 0))
    values = pl.BlockSpec((nq, BRANCH, ATT_TILE), lambda b, i: (b, 0, 0))
    gate = pl.BlockSpec((ATT_TILE, BRANCH), lambda b, i: (b * nq + i, 0))
    qx = pl.BlockSpec((N_HEADS, LANES, ATT_TILE), lambda b, i: (0, 0, 0),
                      pipeline_mode=pl.Buffered(1))
    return nq, q_tile, keys, values, gate, qx


def _top_k_rows(gate, row_f):
    sel = jnp.zeros(gate.shape, F32)
    for _ in range(MOBA_TOPK):
        top = jnp.max(gate, axis=0, keepdims=True)
        first = jnp.min(jnp.where(gate == top, row_f, 1e9), axis=0, keepdims=True)
        pick = row_f == first
        sel = jnp.where(pick, 1.0, sel)
        gate = jnp.where(pick, -jnp.inf, gate)
    return sel > 0.5


def _moba_kernel(qt_ref, k_ref, vt_ref, sel_ref, g_ref, qx_ref, o_ref, qft_buf, *bufs):
    i = pl.program_id(1)
    n_blk = sel_ref.shape[2]
    unused = jnp.zeros((LANES - SEL_LANE0 - n_blk, ATT_TILE), BF16)
    for h, qh in enumerate(_head_operands(qt_ref[0])):
        qft_buf[h] = jnp.concatenate(
            [qh, qx_ref[h, 0:SEL_LANE0, :].astype(BF16), sel_ref[0, h], unused], axis=0)

    causal = (lax.broadcasted_iota(jnp.int32, (ATT_TILE, ATT_TILE), 0)
              <= lax.broadcasted_iota(jnp.int32, (ATT_TILE, ATT_TILE), 1))
    ops = _SweepOps(k_ref, vt_ref, qft_buf, bufs[0:2], bufs[2:4], bufs[4:6], bufs[6:8], *bufs[8:])
    _flash_sweep(ops, i + 1, lambda s: jnp.where(causal, s, NEG), g_ref, o_ref)


def _moba(qt, k, vt, sel, g, qx, batch, seq):
    nq, q_tile, keys, values, gate, qx_spec = _attn_specs(seq)
    n_blk = seq // MOBA_BLOCK
    assert SEL_LANE0 + n_blk <= LANES and 3 * ALIBI_PIECES <= SEL_LANE0
    return pl.pallas_call(
        _moba_kernel,
        grid=(batch, nq),
        in_specs=[q_tile, keys, values,
                  pl.BlockSpec((1, N_HEADS, n_blk, ATT_TILE), lambda b, i: (b * nq + i, 0, 0, 0)),
                  gate, qx_spec],
        out_specs=gate,
        out_shape=jax.ShapeDtypeStruct(g.shape, F32),
        scratch_shapes=_attn_scratch(),
        compiler_params=_params(2),
        name="moba",
    )(qt, k, vt, sel, g, qx)


def _dilated_multiplicity_table():
    idx = np.arange(ATT_TILE)
    delta = (np.arange(DIL_GROUPS_BACK + 1)[:, None, None] * ATT_TILE
             + idx[None, None, :] - idx[None, :, None])
    mult = np.zeros(delta.shape, np.float64)
    for window, dil in DIL_PATTERNS:
        mult += (delta >= 0) & (delta <= window) & (delta % dil == 0)
    table = np.where(mult > 0, np.log2(np.maximum(mult, 1.0)), NEG)
    return jnp.asarray(np.concatenate([table, np.full_like(table[:1], NEG)]), F32)


def _dilated_kernel(qt_ref, qt_next_ref, k_ref, vt_ref, t_ref, g_ref, qx_ref, o_ref, qft_buf, *bufs):
    i = pl.program_id(1)
    n_steps = DIL_GROUPS_BACK + 1
    ops = _SweepOps(k_ref, vt_ref, qft_buf, bufs[0:2], bufs[2:4], bufs[4:6], bufs[6:8], *bufs[8:])

    def group_of(tile, t):
        return jnp.where(t <= tile, tile - t, 0)

    def table_of(tile, t):
        return t_ref[jnp.where(t <= tile, t, n_steps)]

    def prepare(tile, q_ref):
        for h, qh in enumerate(_head_operands(q_ref[0])):
            qft_buf[h] = jnp.concatenate([qh, qx_ref[h].astype(BF16)], axis=0)
        ops.issue_scores(tile, 0, table_of(tile, 0))

    ops.init()

    @pl.when(i == 0)
    def _():
        prepare(i, qt_ref)

    for t in range(n_steps):
        x = t % 2
        if t + 1 < n_steps:
            ops.issue_scores(group_of(i, t + 1), 1 - x, table_of(i, t + 1))
        if t >= 1:
            ops.fold_values(group_of(i, t - 1), 1 - x)
        ops.softmax(x, issued_max=False)
    prepare(jnp.minimum(i + 1, pl.num_programs(1) - 1), qt_next_ref)
    ops.fold_values(group_of(i, n_steps - 1), (n_steps - 1) % 2)
    ops.finish(g_ref, o_ref)


def _dilated(qt, k, vt, table, g, qx, batch, seq):
    nq, q_tile, keys, values, gate, qx_spec = _attn_specs(seq)
    assert DIL_GROUPS_BACK % 2 == 0
    q_next = pl.BlockSpec((1, BRANCH, ATT_TILE),
                          lambda b, i: (b * nq + jnp.minimum(i + 1, nq - 1), 0, 0))
    return pl.pallas_call(
        _dilated_kernel,
        grid=(batch, nq),
        in_specs=[q_tile, q_next, keys, values,
                  pl.BlockSpec(table.shape, lambda b, i: (0, 0, 0), pipeline_mode=pl.Buffered(1)),
                  gate, qx_spec],
        out_specs=gate,
        out_shape=jax.ShapeDtypeStruct(g.shape, F32),
        scratch_shapes=_attn_scratch(),
        compiler_params=_params(2),
        name="dilated",
    )(qt, qt, k, vt, table, g, qx)


def _conv_module(z, halo, has_history, w_ref, b_ref, lng_ref, lnb_ref, pw_ref, pwb_ref, u_buf):
    def glu(z):
        return z[:, 0:BRANCH] * jax.nn.sigmoid(z[:, BRANCH:2 * BRANCH])

    u_buf[0, 0:CONV_HALO, :] = jnp.where(has_history, glu(halo), 0.0)
    u_buf[0, CONV_HALO:, :] = glu(z)
    shifted = CONV_HALO + ROW_TILE - SUBLANES
    for phase in range(1, SUBLANES):
        u_buf[phase, 0:shifted, :] = u_buf[0, phase:phase + shifted, :]
    acc = jnp.zeros((ROW_TILE, BRANCH), F32) + b_ref[...]
    first = CONV_HALO - (CONV_WIDTH - 1)
    for tap in range(CONV_WIDTH):
        phase, start = (first + tap) % SUBLANES, (first + tap) // SUBLANES * SUBLANES
        acc = acc + w_ref[tap:tap + 1, :] * u_buf[phase, start:start + ROW_TILE, :]
    mu = jnp.mean(acc, axis=-1, keepdims=True)
    cen = acc - mu
    var = jnp.mean(cen * cen, axis=-1, keepdims=True)
    un = cen * lax.rsqrt(var + EPS) * lng_ref[...] + lnb_ref[...]
    y = jnp.dot(_silu(un).astype(BF16), pw_ref[...], preferred_element_type=F32) + pwb_ref[...]
    return y * _silu(z[:, 2 * BRANCH:3 * BRANCH])


def _gla_sum_matrices():
    c = GLA_CHUNK
    i = np.arange(c)[:, None]
    t = np.arange(c)[None, :]
    mats = [t <= i]
    for l in range(GLA_LEVELS):
        h = (c // 2) >> l
        mid = (i // (2 * h)) * (2 * h) + h
        later = (i & h) != 0
        mats.append((later & (t >= mid) & (t <= i)) | (~later & (t > i) & (t < mid)))
    return jnp.asarray(np.concatenate(mats, axis=0), BF16)


def _gla_tile(z_ref, sums_ref, wg_ref, bg_ref, gn_ref, bd_ref, o_ref, state_ref):
    c = GLA_CHUNK
    nh = GLA_HEADS
    kw = nh * GLA_DK
    vw = nh * GLA_DV

    row = lax.broadcasted_iota(jnp.int32, (c, kw), 0)
    qi = lax.broadcasted_iota(jnp.int32, (c, nh * c), 0)
    kj = lax.broadcasted_iota(jnp.int32, (c, nh * c), 1) % c
    level_mask = [(qi >> (GLA_LEVELS - l)) == (kj >> (GLA_LEVELS - l)) for l in range(GLA_LEVELS)]
    diag_mask = qi == kj
    k_head = (lax.broadcasted_iota(jnp.int32, (nh * c, kw), 0) // c
              == lax.broadcasted_iota(jnp.int32, (nh * c, kw), 1) // GLA_DK)
    v_head = (lax.broadcasted_iota(jnp.int32, (nh * c, vw), 0) // c
              == lax.broadcasted_iota(jnp.int32, (nh * c, vw), 1) // GLA_DV)
    s_head = (lax.broadcasted_iota(jnp.int32, (vw, kw), 0) // GLA_DV
              == lax.broadcasted_iota(jnp.int32, (vw, kw), 1) // GLA_DK)

    k_head_bf = jnp.where(k_head, 1.0, 0.0).astype(BF16)

    def per_head_keys(kt):
        return jnp.concatenate([kt.astype(BF16)] * nh, axis=0) * k_head_bf

    def chunk_of(b, rows):
        q = z_ref[b, rows, 0:kw] * GLA_DK ** -0.5
        k = z_ref[b, rows, kw:2 * kw]
        v = z_ref[b, rows, 2 * kw:2 * kw + vw]
        gd = z_ref[b, rows, 2 * kw + vw:2 * kw + 2 * vw]
        lr = z_ref[b, rows, 2 * kw + 2 * vw:2 * kw + 2 * vw + LR_PAD]

        g = jnp.dot(lr.astype(BF16), wg_ref[...], preferred_element_type=F32) + bg_ref[...]
        la = (jnp.minimum(g, 0.0) - jnp.log(1.0 + jnp.exp(-jnp.abs(g)))) / GLA_TAU
        a1 = la.astype(BF16)
        a2 = (la - a1.astype(F32)).astype(BF16)
        parts = jnp.dot(sums_ref[...], jnp.concatenate([a1, a2], axis=1),
                        preferred_element_type=F32)
        sums = parts[:, 0:kw] + parts[:, kw:2 * kw]
        bc = sums[0:c]

        attn = jnp.where(diag_mask,
                         lax.dot_general(q.astype(BF16), per_head_keys(k), _NT,
                                         preferred_element_type=F32), 0.0)
        for l in range(GLA_LEVELS):
            later = (row & ((c // 2) >> l)) != 0
            scaled = jnp.where(later, q, k) * jnp.exp(sums[(1 + l) * c:(2 + l) * c])
            qt = jnp.where(later, scaled, 0.0).astype(BF16)
            a = lax.dot_general(qt, per_head_keys(jnp.where(later, 0.0, scaled)), _NT,
                                preferred_element_type=F32)
            attn = attn + (a if l == 0 else jnp.where(level_mask[l], a, 0.0))

        vb = v.astype(BF16)
        v_stack = jnp.where(v_head, jnp.concatenate([vb] * nh, axis=0), jnp.zeros((), BF16))
        o = jnp.dot(attn.astype(BF16), v_stack, preferred_element_type=F32)

        state = state_ref[b]
        o = o + lax.dot_general((q * jnp.exp(bc)).astype(BF16), state.astype(BF16), _NT,
                                preferred_element_type=F32)
        b_last = bc[c - 1:c, :]
        k_dec = (k * jnp.exp(b_last - bc)).astype(BF16)
        upd = lax.dot_general(vb, k_dec, _TN, preferred_element_type=F32)
        state_ref[b] = state * jnp.exp(b_last) + jnp.where(s_head, upd, 0.0)

        on = o * lax.rsqrt(_group_mean_sq(o, bd_ref[...]) + EPS) * gn_ref[...]
        o_ref[b, rows, :] = on * _silu(gd)

    for ci in range(ROW_TILE // c):
        for b in range(z_ref.shape[0]):
            chunk_of(b, pl.ds(ci * c, c))


def _tail_kernel(x_ref, ya_ref, yb_ref, zc_ref, halo_ref, zd_ref,
                 cw_ref, cb_ref, lng_ref, lnb_ref, pw_ref, pwb_ref,
                 sums_ref, wg_ref, bg_ref, gn_ref, bd_ref, wo_ref,
                 o_ref, u_buf, yd_buf, state_ref):
    j = pl.program_id(0)

    @pl.when(j == 0)
    def _():
        state_ref[...] = jnp.zeros_like(state_ref)

    def project(y, g):
        return jnp.dot(y.astype(BF16), wo_ref[g * BRANCH:(g + 1) * BRANCH, :],
                       preferred_element_type=F32)

    _gla_tile(zd_ref, sums_ref, wg_ref, bg_ref, gn_ref, bd_ref, yd_buf, state_ref)
    for b in range(x_ref.shape[0]):
        yc = _conv_module(zc_ref[b], halo_ref[b], j > 0, cw_ref, cb_ref, lng_ref, lnb_ref,
                          pw_ref, pwb_ref, u_buf.at[b])
        o_ref[b] = (x_ref[b] + project(ya_ref[b], 0) + project(yb_ref[b], 1) + project(yc, 2)
                    + project(yd_buf[b], 3))


def _tail(x2, ya, yb, zc, zd, conv_consts, gla_consts, w_out_all, layer, batch, seq):
    per = ROW_TILE // CONV_HALO
    tile = lambda cols: pl.BlockSpec((batch, ROW_TILE, cols), lambda j: (0, j, 0))
    whole = lambda a: pl.BlockSpec(a.shape, lambda j: (0,) * a.ndim, pipeline_mode=pl.Buffered(1))
    by_seq = lambda a: a.reshape(batch, seq, a.shape[-1])
    consts = (*conv_consts, *gla_consts, w_out_all)
    w_out_spec = pl.BlockSpec((None,) + w_out_all.shape[1:], lambda j: (layer, 0, 0),
                              pipeline_mode=pl.Buffered(1))
    out = pl.pallas_call(
        _tail_kernel,
        grid=(seq // ROW_TILE,),
        in_specs=[tile(D_MODEL), tile(BRANCH), tile(BRANCH), tile(ZC_COLS),
                  pl.BlockSpec((batch, CONV_HALO, ZC_COLS),
                               lambda j: (0, jnp.maximum(j * per - 1, 0), 0)),
                  tile(ZD_COLS)] + [whole(a) for a in consts[:-1]] + [w_out_spec],
        out_specs=tile(D_MODEL),
        out_shape=jax.ShapeDtypeStruct((batch, seq, D_MODEL), F32),
        scratch_shapes=[pltpu.VMEM((batch, SUBLANES, CONV_HALO + ROW_TILE, BRANCH), F32),
                        pltpu.VMEM((batch, ROW_TILE, BRANCH), F32),
                        pltpu.VMEM((batch, GLA_HEADS * GLA_DV, GLA_HEADS * GLA_DK), F32)],
        compiler_params=_params(1),
        name="tail",
    )(by_seq(x2), by_seq(ya), by_seq(yb), by_seq(zc), by_seq(zc), by_seq(zd), *consts)
    return out.reshape(batch * seq, D_MODEL)


def _pack_w_in(w_in):
    col = lambda j: w_in[..., j * BRANCH:(j + 1) * BRANCH]
    gla0 = 11 * BRANCH
    qkv = 2 * GLA_HEADS * GLA_DK + BRANCH
    pad = jnp.zeros(w_in.shape[:-1] + (LR_PAD - GLA_RANK,), w_in.dtype)
    wn = jnp.concatenate([col(1), col(3), col(5), col(7), w_in[..., 8 * BRANCH:gla0 + qkv],
                          w_in[..., gla0 + qkv + GLA_RANK:], w_in[..., gla0 + qkv:gla0 + qkv + GLA_RANK],
                          pad], axis=-1)
    wq = jnp.concatenate([col(0), col(2), col(4), col(6)], axis=-1)
    return wn.astype(BF16), wq.astype(BF16)


def _layer(x2, batch, seq, consts, layer, big_weights, norm_g, q_gain_a, k_gain_a, q_gain_b, k_gain_b,
           conv_w, conv_b, conv_ln_g, conv_ln_b, conv_pw_w, conv_pw_b, gla_gate_w, gla_gate_b, gla_norm_g):
    bd, kx, qx_moba, qx_dil, dil_table, gla_sums = consts
    wn_all, wt_all, w_out_all = big_weights
    kgains = jnp.stack([jnp.tile(k_gain_a, N_HEADS), jnp.tile(k_gain_b, N_HEADS)])
    qgains = jnp.stack([jnp.tile(q_gain_a, N_HEADS), jnp.tile(q_gain_b, N_HEADS)])[:, :, None]
    (qat, ka, vat, ga, sel, qbt, kb, vbt, gb, zc, zd) = _inproj(
        x2, norm_g[None, :], wn_all, wt_all, layer, kgains, qgains, bd, kx, seq)
    ya = _moba(qat, ka, vat, sel, ga, qx_moba, batch, seq)
    yb = _dilated(qbt, kb, vbt, dil_table, gb, qx_dil, batch, seq)
    wg = jnp.concatenate([gla_gate_w, jnp.zeros((LR_PAD - GLA_RANK, gla_gate_w.shape[1]), F32)],
                         axis=0).astype(BF16)
    conv_consts = (conv_w, conv_b[None, :], conv_ln_g[None, :], conv_ln_b[None, :],
                   conv_pw_w.astype(BF16), conv_pw_b[None, :])
    gla_consts = (gla_sums, wg, gla_gate_b[None, :], jnp.tile(gla_norm_g, GLA_HEADS)[None, :], bd)
    return _tail(x2, ya, yb, zc, zd, conv_consts, gla_consts, w_out_all, layer, batch, seq)


def kernel(x, norm_g, w_in, q_gain_a, k_gain_a, q_gain_b, k_gain_b, conv_w, conv_b, conv_ln_g, conv_ln_b,
           conv_pw_w, conv_pw_b, gla_gate_w, gla_gate_b, gla_norm_g, w_out):
    batch, seq, d = x.shape
    assert d == D_MODEL and seq % ROW_TILE == 0 and ROW_TILE == ATT_TILE
    group = np.arange(BRANCH) // HEAD_DIM
    bd = jnp.asarray((group[:, None] == group[None, :]) / HEAD_DIM, BF16)
    heads = np.arange(N_HEADS)
    consts = (bd, _key_position_lanes(seq),
              _query_alibi_rows(2.0 ** -(1.0 + 2 * heads)),
              _query_alibi_rows(2.0 ** -(2.0 + 2 * heads)),
              _dilated_multiplicity_table(), _gla_sum_matrices())
    x2 = x.reshape(batch * seq, d)
    big_weights = (*_pack_w_in(w_in), w_out.astype(BF16))
    params = (norm_g, q_gain_a, k_gain_a, q_gain_b, k_gain_b, conv_w, conv_b, conv_ln_g,
              conv_ln_b, conv_pw_w, conv_pw_b, gla_gate_w, gla_gate_b, gla_norm_g)
    for layer in range(norm_g.shape[0]):
        x2 = _layer(x2, batch, seq, consts, layer, big_weights, *(p[layer] for p in params))
    return x2.reshape(batch, seq, d)
```
